```python
import math
import jax
import jax.numpy as jnp
from jax import lax
import numpy as np

D_MODEL = 1024
BATCH = 16
SEQ = 256
DEPTH = 2
DEC_BATCH = 4
DEC_SEQ = 4096
PAST_LEN = 512

GRID_W = 64
N_MIXERS = 2
N_LA_LAYERS = (DEPTH + 1) // 2
N_SWA_LAYERS = DEPTH // 2

LA_DK = 128
LA_DV = 128
LA_HEADS = D_MODEL // LA_DK
LA_QKV = LA_HEADS * (2 * LA_DK + LA_DV)
LA_PROJ = LA_QKV + LA_HEADS * LA_DV + 4 * LA_HEADS
CONV_K = 5
CHUNK = 64

SWA_DH = 64
SWA_HEADS = D_MODEL // SWA_DH
SWA_KV_HEADS = SWA_HEADS // 4
SWA_GROUP = SWA_HEADS // SWA_KV_HEADS
SWA_QKV = (SWA_HEADS + 2 * SWA_KV_HEADS) * SWA_DH
WINDOW = 128
Q_BLOCK = 128
ROT_FREQS = SWA_DH // 4
ROPE_BASE = 10000.0
NEG_INF = -1e30

N_EXPERTS = 64
TOP_K = 8
N_GROUPS = 8
TOPK_GROUPS = 4
D_EXPERT = 256
D_SHARED = 256
ROUTED_SCALE = 2.5
MOE_BLOCK = 128

ALPHA = (2 * DEPTH) ** 0.25
BETA_INIT = (8 * DEPTH) ** -0.25
LN_EPS = 1e-5
RMS_EPS = 1e-6

kernel_name = 'hybrid_deltanet_swa_moe_diffusion_step'


def _layer_norm(x, g, b):
    xf = x.astype(jnp.float32)
    mu = xf.mean(-1, keepdims=True)
    var = jnp.square(xf - mu).mean(-1, keepdims=True)
    return ((xf - mu) * lax.rsqrt(var + LN_EPS) * g.astype(jnp.float32) + b.astype(jnp.float32)).astype(x.dtype)


def _rms_norm(x, g):
    xf = x.astype(jnp.float32)
    return xf * lax.rsqrt(jnp.mean(xf * xf, -1, keepdims=True) + RMS_EPS) * g.astype(jnp.float32)


def _l2norm(x):
    xf = x.astype(jnp.float32)
    return xf * lax.rsqrt(jnp.sum(xf * xf, -1, keepdims=True) + 1e-6)


def _modulation(cond, w_mod, b_mod):
    m = jax.nn.silu(cond) @ w_mod + b_mod
    return jnp.split(m[:, None, :], 6, axis=-1)


def _centred_conv(x, w):
    return lax.conv_general_dilated(x, w[:, None, :].astype(x.dtype), window_strides=(1,),
                                    padding=[(CONV_K // 2, CONV_K // 2)],
                                    dimension_numbers=('NWC', 'WIO', 'NWC'),
                                    feature_group_count=x.shape[-1])


def _gated_delta_chunked(q, k, v, g, beta, s0):
    b, t, h, dk = q.shape
    dv = v.shape[-1]
    n = t // CHUNK

    def to_chunks(a):
        return a.reshape((b, n, CHUNK, h) + a.shape[3:]).swapaxes(2, 3)

    qc, kc, vc, gc, bc = (to_chunks(a) for a in (q, k, v, g, beta))
    gam = jnp.cumsum(gc, axis=-1)
    incl = jnp.tril(jnp.ones((CHUNK, CHUNK), bool))
    strict = jnp.tril(jnp.ones((CHUNK, CHUNK), bool), -1)
    dec = jnp.where(incl, jnp.exp(jnp.where(incl, gam[..., :, None] - gam[..., None, :], 0.0)), 0.0)
    kb = kc * bc[..., None]
    a_mat = jnp.where(strict, jnp.einsum('bnhik,bnhjk->bnhij', kb, kc) * dec, 0.0)
    rhs = jnp.concatenate([vc * bc[..., None], kb * jnp.exp(gam)[..., None]], -1)
    sol = lax.linalg.triangular_solve(a_mat, rhs, left_side=True, lower=True, unit_diagonal=True)
    u, w = sol[..., :dv], sol[..., dv:]
    qk = jnp.where(incl, jnp.einsum('bnhik,bnhjk->bnhij', qc, kc) * dec, 0.0)
    q_dec = qc * jnp.exp(gam)[..., None]
    k_dec = kc * jnp.exp(gam[..., -1:] - gam)[..., None]
    g_last = jnp.exp(gam[..., -1])

    def step(s, xs):
        u_i, w_i, qk_i, qd_i, kd_i, gl_i = xs
        v_new = u_i - jnp.einsum('bhck,bhkv->bhcv', w_i, s)
        o_i = jnp.einsum('bhck,bhkv->bhcv', qd_i, s) + jnp.einsum('bhij,bhjv->bhiv', qk_i, v_new)
        s = s * gl_i[..., None, None] + jnp.einsum('bhck,bhcv->bhkv', kd_i, v_new)
        return s, o_i

    xs = tuple(jnp.moveaxis(a, 1, 0) for a in (u, w, qk, q_dec, k_dec, g_last))
    s_fin, o = lax.scan(step, s0, xs)
    o = jnp.moveaxis(o, 0, 1).swapaxes(2, 3).reshape(b, t, h, dv)
    return o, s_fin


def _deltanet(h, s0, w_in, conv_w, a_log, dt_bias, norm_g, w_out):
    bsz, t, _ = h.shape
    proj = h @ w_in
    qkv, z, ab = jnp.split(proj, [LA_QKV, LA_QKV + LA_HEADS * LA_DV], axis=-1)
    qkv = jax.nn.silu(_centred_conv(qkv, conv_w))
    q, k, v = jnp.split(qkv, [LA_HEADS * LA_DK, 2 * LA_HEADS * LA_DK], axis=-1)
    q = _l2norm(q.reshape(bsz, t, LA_HEADS, LA_DK)) * LA_DK ** -0.5
    k = _l2norm(k.reshape(bsz, t, LA_HEADS, LA_DK))
    v = v.reshape(bsz, t, LA_HEADS, LA_DV).astype(jnp.float32)
    ab = ab.astype(jnp.float32).reshape(bsz, t, 4, LA_HEADS)
    g = -jnp.exp(a_log.astype(jnp.float32)) * jax.nn.softplus(ab[:, :, :2] + dt_bias.astype(jnp.float32))
    beta = jax.nn.sigmoid(ab[:, :, 2:])
    s0 = s0.astype(jnp.float32)
    o_f, s_f = _gated_delta_chunked(q, k, v, g[:, :, 0], beta[:, :, 0], s0[:, 0])
    o_b, s_b = _gated_delta_chunked(q[:, ::-1], k[:, ::-1], v[:, ::-1], g[:, ::-1, 1], beta[:, ::-1, 1], s0[:, 1])
    o = o_f + o_b[:, ::-1]
    o = _rms_norm(o, norm_g) * jax.nn.silu(z.reshape(bsz, t, LA_HEADS, LA_DV).astype(jnp.float32))
    out = o.reshape(bsz, t, LA_HEADS * LA_DV).astype(h.dtype) @ w_out
    return out, jnp.stack([s_f, s_b], axis=1).astype(h.dtype)


def _axial_rope_tables(t_len):
    rows = t_len // GRID_W
    r, col = jnp.meshgrid(jnp.arange(rows), jnp.arange(GRID_W), indexing='ij')
    r = r.reshape(-1).astype(jnp.float32)
    col = col.reshape(-1).astype(jnp.float32)
    inv = ROPE_BASE ** (-jnp.arange(ROT_FREQS, dtype=jnp.float32) / ROT_FREQS)
    ang_r = r[:, None] * inv
    ang_c = col[:, None] * inv
    ang = jnp.concatenate([ang_r, ang_r, ang_c, ang_c], -1)
    return jnp.cos(ang), jnp.sin(ang)


def _apply_rope(x, cos, sin):
    shape = (1, x.shape[1]) + (1,) * (x.ndim - 3) + (x.shape[-1],)
    x1, x2, x3, x4 = jnp.split(x, 4, axis=-1)
    rot = jnp.concatenate([-x2, x1, -x4, x3], -1)
    return (x * cos.reshape(shape) + rot * sin.reshape(shape)).astype(x.dtype)


def _swa_project(h, w_qkv):
    bsz, t, _ = h.shape
    qkv = h @ w_qkv
    q, k, v = jnp.split(qkv, [SWA_HEADS * SWA_DH, (SWA_HEADS + SWA_KV_HEADS) * SWA_DH], axis=-1)
    q = q.reshape(bsz, t, SWA_KV_HEADS, SWA_GROUP, SWA_DH)
    k = k.reshape(bsz, t, SWA_KV_HEADS, SWA_DH)
    v = v.reshape(bsz, t, SWA_KV_HEADS, SWA_DH)
    return q, k, v


def _attend_with_sink(logits, values, sink_l):
    sink_col = jnp.broadcast_to(sink_l[None, :, :, None, None], logits.shape[:-1] + (1,))
    p = jax.nn.softmax(jnp.concatenate([logits, sink_col], -1), axis=-1)[..., :-1]
    return jnp.einsum('bkgqs,bskd->bqkgd', p.astype(values.dtype), values)


def _swa_context(h, w_qkv, sink, w_out):
    q, k, v = _swa_project(h, w_qkv)
    bsz, t = h.shape[:2]
    sink_l = sink.astype(jnp.float32).reshape(SWA_KV_HEADS, SWA_GROUP)

    def block(bi):
        qb = lax.dynamic_slice_in_dim(q, bi * Q_BLOCK, Q_BLOCK, axis=1)
        s = jnp.einsum('bqkgd,bskd->bkgqs', qb, k, preferred_element_type=jnp.float32) * SWA_DH ** -0.5
        return _attend_with_sink(s, v, sink_l)

    o = lax.map(block, jnp.arange(t // Q_BLOCK))
    o = jnp.moveaxis(o, 0, 1).reshape(bsz, t, SWA_HEADS * SWA_DH)
    return o @ w_out, k, v


def _swa_latent(h, ck, cv, w_qkv, sink, w_out):
    q, k, v = _swa_project(h, w_qkv)
    bsz, t = h.shape[:2]
    cos, sin = _axial_rope_tables(t)
    q = _apply_rope(q, cos, sin)
    k = _apply_rope(k, cos, sin)
    pad = [(0, 0), (WINDOW, WINDOW), (0, 0), (0, 0)]
    k_pad = jnp.pad(k, pad)
    v_pad = jnp.pad(v, pad)
    span = Q_BLOCK + 2 * WINDOW
    sink_l = sink.astype(jnp.float32).reshape(SWA_KV_HEADS, SWA_GROUP)
    scale = SWA_DH ** -0.5

    def block(bi):
        q0 = bi * Q_BLOCK
        qb = lax.dynamic_slice_in_dim(q, q0, Q_BLOCK, axis=1)
        kb = lax.dynamic_slice_in_dim(k_pad, q0, span, axis=1)
        vb = lax.dynamic_slice_in_dim(v_pad, q0, span, axis=1)
        qpos = q0 + jnp.arange(Q_BLOCK)
        kpos = q0 - WINDOW + jnp.arange(span)
        valid = (kpos[None, :] >= 0) & (kpos[None, :] < t) & (jnp.abs(qpos[:, None] - kpos[None, :]) <= WINDOW)
        s_win = jnp.einsum('bqkgd,bskd->bkgqs', qb, kb, preferred_element_type=jnp.float32) * scale
        s_win = jnp.where(valid, s_win, NEG_INF)
        s_ctx = jnp.einsum('bqkgd,bskd->bkgqs', qb, ck, preferred_element_type=jnp.float32) * scale
        vals = jnp.concatenate([vb, cv.astype(vb.dtype)], axis=1)
        return _attend_with_sink(jnp.concatenate([s_win, s_ctx], -1), vals, sink_l)

    o = lax.map(block, jnp.arange(t // Q_BLOCK))
    o = jnp.moveaxis(o, 0, 1).reshape(bsz, t, SWA_HEADS * SWA_DH)
    return o @ w_out


def _moe(h, router, bias, w_gate, w_up, w_down, s_gate, s_up, s_down):
    shp = h.shape
    x = h.reshape(-1, shp[-1])
    n = x.shape[0]
    scores = jax.nn.sigmoid(jnp.dot(x, router, preferred_element_type=jnp.float32))
    biased = scores + bias.astype(jnp.float32)
    per = N_EXPERTS // N_GROUPS
    grp = lax.top_k(biased.reshape(n, N_GROUPS, per), 2)[0].sum(-1)
    _, top_g = lax.top_k(grp, TOPK_GROUPS)
    gmask = jax.nn.one_hot(top_g, N_GROUPS, dtype=jnp.float32).sum(1) > 0
    sel = jnp.where(jnp.repeat(gmask, per, axis=1), biased, -jnp.inf)
    _, top_e = lax.top_k(sel, TOP_K)
    wts = jnp.take_along_axis(scores, top_e, axis=1)
    wts = wts / wts.sum(-1, keepdims=True) * ROUTED_SCALE
    nk = n * TOP_K
    e_flat = top_e.reshape(nk)
    tok_flat = jnp.repeat(jnp.arange(n, dtype=jnp.int32), TOP_K)
    order = jnp.argsort(e_flat)
    e_s = e_flat[order]
    tok_s = tok_flat[order]
    g_s = wts.reshape(nk)[order]
    counts = jnp.zeros((N_EXPERTS,), jnp.int32).at[e_flat].add(1)
    padded = (counts + MOE_BLOCK - 1) // MOE_BLOCK * MOE_BLOCK
    pad_end = jnp.cumsum(padded)
    pad_start = pad_end - padded
    cnt_start = jnp.cumsum(counts) - counts
    dest = pad_start[e_s] + jnp.arange(nk, dtype=jnp.int32) - cnt_start[e_s]
    n_blocks = -(-nk // MOE_BLOCK) + N_EXPERTS
    tok_buf = jnp.full((n_blocks * MOE_BLOCK,), n, jnp.int32).at[dest].set(tok_s)
    gate_buf = jnp.zeros((n_blocks * MOE_BLOCK,), jnp.float32).at[dest].set(g_s)
    blk_expert = jnp.minimum(jnp.searchsorted(pad_end, jnp.arange(n_blocks, dtype=jnp.int32) * MOE_BLOCK, side='right'), N_EXPERTS - 1)
    x_pad = jnp.concatenate([x, jnp.zeros((1, shp[-1]), x.dtype)], 0)

    def body(acc, blk):
        tok, gate, e = blk
        xb = x_pad[tok]
        yb = (jax.nn.silu(xb @ w_gate[e]) * (xb @ w_up[e])) @ w_down[e]
        return acc.at[tok].add(yb * gate[:, None].astype(yb.dtype)), None

    routed, _ = lax.scan(body, jnp.zeros((n + 1, shp[-1]), x.dtype),
                         (tok_buf.reshape(n_blocks, MOE_BLOCK), gate_buf.reshape(n_blocks, MOE_BLOCK), blk_expert))
    shared = (jax.nn.silu(x @ s_gate) * (x @ s_up)) @ s_down
    return (routed[:n] + shared).reshape(shp)


def setup_inputs(seed: int = 0) -> dict:
    key = jax.random.key(seed)
    ks = jax.random.split(key, 30)
    f32 = jnp.float32
    d = D_MODEL

    def nrm(k, shape, scale):
        return jax.random.normal(k, shape, f32) * scale

    dt = jnp.exp(jax.random.uniform(ks[14], (N_LA_LAYERS, 2, LA_HEADS), f32, math.log(1e-3), math.log(1e-1)))
    return {
        'x_prompt': nrm(ks[0], (BATCH, SEQ, d), 1.0),
        'x_sample': nrm(ks[1], (DEC_BATCH, DEC_SEQ, d), 1.0),
        'state_la': nrm(ks[2], (DEC_BATCH, N_LA_LAYERS, 2, LA_HEADS, LA_DK, LA_DV), 0.5),
        'cache_k': nrm(ks[3], (DEC_BATCH, N_SWA_LAYERS, PAST_LEN, SWA_KV_HEADS, SWA_DH), 1.0),
        'cache_v': nrm(ks[4], (DEC_BATCH, N_SWA_LAYERS, PAST_LEN, SWA_KV_HEADS, SWA_DH), 1.0),
        'c': nrm(ks[5], (DEC_BATCH, d), 1.0),
        'c_ctx': nrm(ks[6], (d,), 1.0),
        'w_mod': nrm(ks[7], (DEPTH, d, 6 * d), d ** -0.5),
        'b_mod': nrm(ks[8], (DEPTH, 6 * d), 0.02),
        'ln_g': 1.0 + nrm(ks[9], (DEPTH, 2, d), 0.02),
        'ln_b': nrm(ks[10], (DEPTH, 2, d), 0.02),
        'la_w_in': nrm(ks[11], (N_LA_LAYERS, d, LA_PROJ), d ** -0.5),
        'la_conv': nrm(ks[12], (N_LA_LAYERS, CONV_K, LA_QKV), CONV_K ** -0.5),
        'la_a_log': jnp.log(jax.random.uniform(ks[13], (N_LA_LAYERS, 2, LA_HEADS), f32, 1.0, 16.0)),
        'la_dt_bias': dt + jnp.log(-jnp.expm1(-dt)),
        'la_norm_g': 1.0 + nrm(ks[15], (N_LA_LAYERS, LA_DV), 0.02),
        'la_w_out': nrm(ks[16], (N_LA_LAYERS, LA_HEADS * LA_DV, d), BETA_INIT * (LA_HEADS * LA_DV) ** -0.5),
        'swa_w_qkv': nrm(ks[17], (N_SWA_LAYERS, d, SWA_QKV), d ** -0.5),
        'swa_sink': nrm(ks[18], (N_SWA_LAYERS, SWA_HEADS), 1.0),
        'swa_w_out': nrm(ks[19], (N_SWA_LAYERS, SWA_HEADS * SWA_DH, d), BETA_INIT * (SWA_HEADS * SWA_DH) ** -0.5),
        'moe_router': nrm(ks[20], (DEPTH, d, N_EXPERTS), d ** -0.5),
        'moe_bias': nrm(ks[21], (DEPTH, N_EXPERTS), 0.01),
        'moe_w_gate': nrm(ks[22], (DEPTH, N_EXPERTS, d, D_EXPERT), d ** -0.5),
        'moe_w_up': nrm(ks[23], (DEPTH, N_EXPERTS, d, D_EXPERT), d ** -0.5),
        'moe_w_down': nrm(ks[24], (DEPTH, N_EXPERTS, D_EXPERT, d), BETA_INIT * D_EXPERT ** -0.5),
        'sh_w_gate': nrm(ks[25], (DEPTH, d, D_SHARED), d ** -0.5),
        'sh_w_up': nrm(ks[26], (DEPTH, d, D_SHARED), d ** -0.5),
        'sh_w_down': nrm(ks[27], (DEPTH, D_SHARED, d), BETA_INIT * D_SHARED ** -0.5),
    }


def reference(x_prompt, x_sample, state_la, cache_k, cache_v, c, c_ctx, w_mod, b_mod, ln_g, ln_b,
              la_w_in, la_conv, la_a_log, la_dt_bias, la_norm_g, la_w_out,
              swa_w_qkv, swa_sink, swa_w_out,
              moe_router, moe_bias, moe_w_gate, moe_w_up, moe_w_down, sh_w_gate, sh_w_up, sh_w_down):
    b_ctx = x_prompt.shape[0]
    c_ctx_b = jnp.broadcast_to(c_ctx, (b_ctx, D_MODEL))
    xp, xs = x_prompt, x_sample
    new_la, new_k, new_v = [], [], []
    for i in range(DEPTH):
        j = i // N_MIXERS
        sh1_p, sc1_p, g1_p, sh2_p, sc2_p, g2_p = _modulation(c_ctx_b, w_mod[i], b_mod[i])
        sh1_s, sc1_s, g1_s, sh2_s, sc2_s, g2_s = _modulation(c, w_mod[i], b_mod[i])
        hp = xp * (1.0 + sc1_p) + sh1_p
        hs = xs * (1.0 + sc1_s) + sh1_s
        if i % N_MIXERS == 0:
            s_zero = jnp.zeros((b_ctx, 2, LA_HEADS, LA_DK, LA_DV), hp.dtype)
            op, st = _deltanet(hp, s_zero, la_w_in[j], la_conv[j], la_a_log[j], la_dt_bias[j], la_norm_g[j], la_w_out[j])
            os_, _ = _deltanet(hs, state_la[:, j], la_w_in[j], la_conv[j], la_a_log[j], la_dt_bias[j], la_norm_g[j], la_w_out[j])
            new_la.append(st)
        else:
            op, kc, vc = _swa_context(hp, swa_w_qkv[j], swa_sink[j], swa_w_out[j])
            os_ = _swa_latent(hs, cache_k[:, j], cache_v[:, j], swa_w_qkv[j], swa_sink[j], swa_w_out[j])
            new_k.append(kc)
            new_v.append(vc)
        xp = _layer_norm(ALPHA * xp + g1_p * op, ln_g[i, 0], ln_b[i, 0])
        xs = _layer_norm(ALPHA * xs + g1_s * os_, ln_g[i, 0], ln_b[i, 0])
        hp = xp * (1.0 + sc2_p) + sh2_p
        hs = xs * (1.0 + sc2_s) + sh2_s
        mp = _moe(hp, moe_router[i], moe_bias[i], moe_w_gate[i], moe_w_up[i], moe_w_down[i], sh_w_gate[i], sh_w_up[i], sh_w_down[i])
        ms = _moe(hs, moe_router[i], moe_bias[i], moe_w_gate[i], moe_w_up[i], moe_w_down[i], sh_w_gate[i], sh_w_up[i], sh_w_down[i])
        xp = _layer_norm(ALPHA * xp + g2_p * mp, ln_g[i, 1], ln_b[i, 1])
        xs = _layer_norm(ALPHA * xs + g2_s * ms, ln_g[i, 1], ln_b[i, 1])
    return (xp, xs, jnp.stack(new_la, axis=1), jnp.stack(new_k, axis=1), jnp.stack(new_v, axis=1))
```

```python
import functools
import math

import jax
import jax.numpy as jnp
from jax import lax
from jax.experimental import pallas as pl
from jax.experimental.pallas import tpu as pltpu

F32 = jnp.float32
BF16 = jnp.bfloat16
HIGHEST = lax.Precision.HIGHEST

LANES = 128
SUBLANES = 8
VMEM_LIMIT_BYTES = 56 * 1024 * 1024

D_MODEL = 1024
DEPTH = 2
GRID_W = 64
LA_DK = 128
LA_DV = 128
LA_HEADS = D_MODEL // LA_DK
LA_QKV = LA_HEADS * (2 * LA_DK + LA_DV)
CONV_K = 5
CHUNK = 64
SUB = 16
SWA_DH = 64
SWA_HEADS = D_MODEL // SWA_DH
SWA_KV_HEADS = SWA_HEADS // 4
SWA_GROUP = SWA_HEADS // SWA_KV_HEADS
SWA_KVW = SWA_KV_HEADS * SWA_DH
WINDOW = 128
Q_BLOCK = 128
ROT_FREQS = SWA_DH // 4
ROPE_BASE = 10000.0
NEG_INF = -1e30
N_EXPERTS = 64
TOP_K = 8
N_GROUPS = 8
TOPK_GROUPS = 4
D_EXPERT = 256
D_SHARED = 256
ROUTED_SCALE = 2.5
ALPHA = (2 * DEPTH) ** 0.25
LN_EPS = 1e-5
RMS_EPS = 1e-6

ROW_BLOCK = 256
MOE_ROWS = 256
COND_ROWS = 8


def _cparams(*sem):
    return pltpu.CompilerParams(dimension_semantics=sem, vmem_limit_bytes=VMEM_LIMIT_BYTES)


def _bdot(a, b):
    return jnp.dot(a.astype(BF16), b.astype(BF16), preferred_element_type=F32)


def _bdot_nt(a, b):
    return lax.dot_general(a.astype(BF16), b.astype(BF16), (((1,), (1,)), ((), ())), preferred_element_type=F32)


def _bdot_tn(a, b):
    return lax.dot_general(a.astype(BF16), b.astype(BF16), (((0,), (0,)), ((), ())), preferred_element_type=F32)


def _silu(x):
    return x * jax.nn.sigmoid(x)


def _layer_norm_rows(x, g, b):
    mu = jnp.mean(x, axis=-1, keepdims=True)
    xc = x - mu
    var = jnp.mean(xc * xc, axis=-1, keepdims=True)
    return xc * lax.rsqrt(var + LN_EPS) * g + b


MOD_COLS = 512


def _mod_kernel(c_ref, w_ref, b_ref, o_ref):
    s = _silu(c_ref[...])
    o_ref[0] = jnp.dot(s, w_ref[0], precision=HIGHEST, preferred_element_type=F32) + b_ref[0]


def _modulation(cond, w_mod, b_mod):
    depth, d, n6 = w_mod.shape
    out = pl.pallas_call(
        _mod_kernel,
        grid=(depth, n6 // MOD_COLS),
        in_specs=[
            pl.BlockSpec((COND_ROWS, d), lambda l, j: (0, 0)),
            pl.BlockSpec((1, d, MOD_COLS), lambda l, j: (l, 0, j)),
            pl.BlockSpec((1, 1, MOD_COLS), lambda l, j: (l, 0, j)),
        ],
        out_specs=pl.BlockSpec((1, COND_ROWS, MOD_COLS), lambda l, j: (l, 0, j)),
        out_shape=jax.ShapeDtypeStruct((depth, COND_ROWS, n6), F32),
        compiler_params=_cparams("parallel", "parallel"),
        name="adaln_modulation",
    )(cond, w_mod, b_mod.reshape(depth, 1, n6))
    return out.reshape(depth * COND_ROWS * 6, 1, d)


def _mod_spec(layer, chunk, n_prompt, dec_seq, rows):
    def index(i, *_):
        tok = i * rows
        row = jnp.where(tok < n_prompt, 0, 1 + (tok - n_prompt) // dec_seq)
        return ((layer * COND_ROWS + row) * 6 + chunk, 0, 0)

    return pl.BlockSpec((1, 1, D_MODEL), index)


def _la_in_kernel(x_ref, sc_ref, sh_ref, w_ref, wab_ref, wabt_ref, pc_ref, pr_ref, o_ref, gbc_ref, gbr_ref):
    xm = x_ref[...] * (1.0 + sc_ref[0]) + sh_ref[0]
    o_ref[...] = _bdot(xm, w_ref[...])
    ab = jnp.dot(xm, wab_ref[...], precision=HIGHEST, preferred_element_type=F32)
    lane = lax.broadcasted_iota(jnp.int32, ab.shape, 1)
    neg_a = pc_ref[0:1, :]
    dt_b = pc_ref[1:2, :]
    z = ab + dt_b
    softplus = jnp.maximum(z, 0.0) + jnp.log1p(jnp.exp(-jnp.abs(z)))
    gbc_ref[...] = jnp.where(lane < 2 * LA_HEADS, neg_a * softplus, jax.nn.sigmoid(ab))
    rows = x_ref.shape[0]
    for c in range(rows // CHUNK):
        xc = xm[c * CHUNK:(c + 1) * CHUNK, :]
        abt = lax.dot_general(wabt_ref[...], xc, (((1,), (1,)), ((), ())), precision=HIGHEST,
                              preferred_element_type=F32)
        zt = abt + pr_ref[:, 1:2]
        spt = jnp.maximum(zt, 0.0) + jnp.log1p(jnp.exp(-jnp.abs(zt)))
        gbr_ref[c] = pr_ref[:, 0:1] * spt


def _la_in_proj(x, mod, layer, n_prompt, dec_seq, w_in, a_log, dt_bias):
    n, d = x.shape
    wide = LA_QKV + LA_HEADS * LA_DV
    w_main = w_in[:, :wide].astype(BF16)
    w_ab = w_in[:, wide:]
    n_ab = w_ab.shape[1]
    w_ab_pad = jnp.zeros((d, LANES), F32).at[:, :n_ab].set(w_ab)
    neg_a = -jnp.exp(a_log.astype(F32)).reshape(-1)
    dtb = dt_bias.astype(F32).reshape(-1)
    pc = jnp.zeros((SUBLANES, LANES), F32).at[0, :2 * LA_HEADS].set(neg_a).at[1, :2 * LA_HEADS].set(dtb)
    pr = jnp.zeros((n_ab, LANES), F32).at[:2 * LA_HEADS, 0].set(neg_a).at[:2 * LA_HEADS, 1].set(dtb)
    r = ROW_BLOCK
    return pl.pallas_call(
        _la_in_kernel,
        grid=(n // r,),
        in_specs=[
            pl.BlockSpec((r, d), lambda i: (i, 0)),
            _mod_spec(layer, 1, n_prompt, dec_seq, r),
            _mod_spec(layer, 0, n_prompt, dec_seq, r),
            pl.BlockSpec((d, wide), lambda i: (0, 0)),
            pl.BlockSpec((d, LANES), lambda i: (0, 0)),
            pl.BlockSpec((n_ab, d), lambda i: (0, 0)),
            pl.BlockSpec((SUBLANES, LANES), lambda i: (0, 0)),
            pl.BlockSpec((n_ab, LANES), lambda i: (0, 0)),
        ],
        out_specs=[
            pl.BlockSpec((r, wide), lambda i: (i, 0)),
            pl.BlockSpec((r, LANES), lambda i: (i, 0)),
            pl.BlockSpec((r // CHUNK, n_ab, CHUNK), lambda i: (i, 0, 0)),
        ],
        out_shape=[
            jax.ShapeDtypeStruct((n, wide), F32),
            jax.ShapeDtypeStruct((n, LANES), F32),
            jax.ShapeDtypeStruct((n // CHUNK, n_ab, CHUNK), F32),
        ],
        compiler_params=_cparams("parallel"),
        name="deltanet_in_proj",
    )(x, mod, mod, w_main, w_ab_pad, w_ab.T, pc, pr)


def _proj_kernel(x_ref, sc_ref, sh_ref, w_ref, o_ref):
    xm = x_ref[...] * (1.0 + sc_ref[0]) + sh_ref[0]
    o_ref[...] = _bdot(xm, w_ref[...])


def _mod_proj(x, mod, layer, n_prompt, dec_seq, w, name):
    n, d = x.shape
    cols = w.shape[1]
    r = ROW_BLOCK
    return pl.pallas_call(
        _proj_kernel,
        grid=(n // r,),
        in_specs=[
            pl.BlockSpec((r, d), lambda i: (i, 0)),
            _mod_spec(layer, 1, n_prompt, dec_seq, r),
            _mod_spec(layer, 0, n_prompt, dec_seq, r),
            pl.BlockSpec((d, cols), lambda i: (0, 0)),
        ],
        out_specs=pl.BlockSpec((r, cols), lambda i: (i, 0)),
        out_shape=jax.ShapeDtypeStruct((n, cols), F32),
        compiler_params=_cparams("parallel"),
        name=name,
    )(x, mod, mod, w.astype(BF16))


CONV_PAD = SUBLANES
CONV_ROWS = 256


def _conv_kernel(x_ref, w_ref, o_ref, pad_ref, *, seq):
    c = pl.program_id(1)
    zeros = jnp.zeros((CONV_PAD, LANES), F32)
    pad_ref[0:CONV_PAD, :] = zeros
    pad_ref[CONV_PAD + seq:, :] = zeros
    pad_ref[CONV_PAD:CONV_PAD + seq, :] = x_ref[...]
    is_qk = c < 2 * LA_HEADS
    scale = jnp.where(c < LA_HEADS, LA_DK ** -0.5, 1.0).astype(F32)
    half = CONV_K // 2
    for r0 in range(0, seq, CONV_ROWS):
        acc = jnp.zeros((CONV_ROWS, LANES), F32)
        for j in range(CONV_K):
            start = r0 + CONV_PAD + j - half
            acc = acc + w_ref[j:j + 1, :] * pad_ref[start:start + CONV_ROWS, :]
        y = _silu(acc)
        nrm = lax.rsqrt(jnp.sum(y * y, axis=-1, keepdims=True) + 1e-6) * scale
        o_ref[r0:r0 + CONV_ROWS, :] = y * jnp.where(is_qk, nrm, 1.0)


def _conv_qkv(proj, conv_w, row0, batch, seq):
    w = jnp.zeros((SUBLANES, LA_QKV), F32).at[:CONV_K].set(conv_w.astype(F32))
    blk0 = row0 // seq
    return pl.pallas_call(
        functools.partial(_conv_kernel, seq=seq),
        grid=(batch, LA_QKV // LANES),
        in_specs=[
            pl.BlockSpec((seq, LANES), lambda b, c: (blk0 + b, c)),
            pl.BlockSpec((SUBLANES, LANES), lambda b, c: (0, c)),
        ],
        out_specs=pl.BlockSpec((seq, LANES), lambda b, c: (b, c)),
        out_shape=jax.ShapeDtypeStruct((batch * seq, LA_QKV), F32),
        scratch_shapes=[pltpu.VMEM((seq + 2 * CONV_PAD, LANES), F32)],
        compiler_params=_cparams("parallel", "parallel"),
        name="deltanet_conv",
    )(proj, w)


DELTA_ROWS = 256


def _delta_chunk(qc, kc, vc, g_col, b_col, g_row, s, rev):
    ii = lax.broadcasted_iota(jnp.int32, (CHUNK, CHUNK), 0)
    jj = lax.broadcasted_iota(jnp.int32, (CHUNK, CHUNK), 1)
    if rev:
        before = jj >= ii
        strict = jj > ii
        before_t = ii >= jj
    else:
        before = jj <= ii
        strict = jj < ii
        before_t = ii <= jj
    same_sub = (ii // SUB) == (jj // SUB)
    gam_col = jnp.sum(jnp.where(before, g_row, 0.0), axis=1, keepdims=True)
    gam_row = jnp.sum(jnp.where(before_t, g_col, 0.0), axis=0, keepdims=True)
    gam_last = jnp.sum(g_row, axis=1, keepdims=True)
    dec = jnp.where(before, jnp.exp(jnp.where(before, gam_col - gam_row, 0.0)), 0.0)
    e_col = jnp.exp(gam_col)
    kb = kc * b_col
    kk = _bdot_nt(kb, kc)
    qk = _bdot_nt(qc, kc) * dec
    nm = jnp.where(strict, -(kk * dec), 0.0)
    nd = jnp.where(same_sub, nm, 0.0)
    ne = nm - nd
    yd = nd
    p = nd
    for _ in range(3):
        p = _bdot(p, p)
        yd = yd + p + _bdot(yd, p)
    f = ne + _bdot(yd, ne)
    f2 = _bdot(f, f)
    gm = f + f2 + _bdot(f, f2)
    yt = gm + yd + _bdot(gm, yd)
    rhs = jnp.concatenate([vc * b_col, kb * e_col], axis=1)
    sol = rhs + _bdot(yt, rhs)
    u = sol[:, :LA_DV]
    w = sol[:, LA_DV:]
    v_new = u - _bdot(w, s)
    o = _bdot(qc * e_col, s) + _bdot(qk, v_new)
    k_dec = kc * jnp.exp(gam_last - gam_col)
    s_new = s * jnp.exp(gam_last) + _bdot_tn(k_dec, v_new)
    return o, s_new


def _delta_kernel(*refs, rev, has_s0, dirn):
    if has_s0:
        q_ref, k_ref, v_ref, gbc_ref, gbr_ref, s0_ref, o_ref, sfin_ref, s_ref = refs
    else:
        q_ref, k_ref, v_ref, gbc_ref, gbr_ref, o_ref, sfin_ref, s_ref = refs
    h = pl.program_id(1)
    t = pl.program_id(2)

    @pl.when(t == 0)
    def _():
        if has_s0:
            s_ref[...] = s0_ref[0]
        else:
            s_ref[...] = jnp.zeros_like(s_ref)

    lane = lax.broadcasted_iota(jnp.int32, (CHUNK, LANES), 1)
    g_lane = dirn * LA_HEADS + h
    b_lane = (2 + dirn) * LA_HEADS + h
    n_chunks = DELTA_ROWS // CHUNK
    order = range(n_chunks - 1, -1, -1) if rev else range(n_chunks)
    s = s_ref[...]
    for c in order:
        rows = slice(c * CHUNK, (c + 1) * CHUNK)
        gbc = gbc_ref[rows, :]
        g_col = jnp.sum(jnp.where(lane == g_lane, gbc, 0.0), axis=1, keepdims=True)
        b_col = jnp.sum(jnp.where(lane == b_lane, gbc, 0.0), axis=1, keepdims=True)
        g_row = gbr_ref[c, pl.ds(g_lane, 1), :]
        o, s = _delta_chunk(q_ref[rows, :], k_ref[rows, :], v_ref[rows, :], g_col, b_col, g_row, s, rev)
        o_ref[rows, :] = o
    s_ref[...] = s

    @pl.when(t == pl.num_programs(2) - 1)
    def _():
        sfin_ref[0] = s


def _delta_dir(qkv, gbc, gbr, s0, row0, batch, seq, dirn):
    rev = dirn == 1
    nt = seq // DELTA_ROWS
    gblk0 = row0 // DELTA_ROWS
    n_ab = gbr.shape[1]

    def tt(t):
        return nt - 1 - t if rev else t

    in_specs = [
        pl.BlockSpec((DELTA_ROWS, LA_DK), lambda b, h, t: (b * nt + tt(t), h)),
        pl.BlockSpec((DELTA_ROWS, LA_DK), lambda b, h, t: (b * nt + tt(t), LA_HEADS + h)),
        pl.BlockSpec((DELTA_ROWS, LA_DV), lambda b, h, t: (b * nt + tt(t), 2 * LA_HEADS + h)),
        pl.BlockSpec((DELTA_ROWS, LANES), lambda b, h, t: (gblk0 + b * nt + tt(t), 0)),
        pl.BlockSpec((DELTA_ROWS // CHUNK, n_ab, CHUNK), lambda b, h, t: (gblk0 + b * nt + tt(t), 0, 0)),
    ]
    args = [qkv, qkv, qkv, gbc, gbr]
    has_s0 = s0 is not None
    if has_s0:
        in_specs.append(pl.BlockSpec((1, LA_DK, LA_DV), lambda b, h, t: (b * LA_HEADS + h, 0, 0)))
        args.append(s0)
    return pl.pallas_call(
        functools.partial(_delta_kernel, rev=rev, has_s0=has_s0, dirn=dirn),
        grid=(batch, LA_HEADS, nt),
        in_specs=in_specs,
        out_specs=[
            pl.BlockSpec((DELTA_ROWS, LA_DV), lambda b, h, t: (b * nt + tt(t), h)),
            pl.BlockSpec((1, LA_DK, LA_DV), lambda b, h, t: (b * LA_HEADS + h, 0, 0)),
        ],
        out_shape=[
            jax.ShapeDtypeStruct((batch * seq, LA_HEADS * LA_DV), F32),
            jax.ShapeDtypeStruct((batch * LA_HEADS, LA_DK, LA_DV), F32),
        ],
        scratch_shapes=[pltpu.VMEM((LA_DK, LA_DV), F32)],
        compiler_params=_cparams("parallel", "parallel", "arbitrary"),
        name=f"deltanet_scan_dir{dirn}",
    )(*args)


def _stream_specs(n_prompt, rows, cols):
    npb = n_prompt // rows
    return [
        pl.BlockSpec((rows, cols), lambda i: (jnp.minimum(i, npb - 1), 0)),
        pl.BlockSpec((rows, cols), lambda i: (jnp.maximum(i - npb, 0), 0)),
    ]


def _la_out_kernel(ofp_ref, obp_ref, ofs_ref, obs_ref, z_ref, x_ref, g1_ref, ng_ref, w_ref, lng_ref, lnb_ref,
                   y_ref, *, npb):
    is_prompt = pl.program_id(0) < npb
    o = jnp.where(is_prompt, ofp_ref[...] + obp_ref[...], ofs_ref[...] + obs_ref[...])
    ng = ng_ref[0:1, :]
    parts = []
    for h in range(LA_HEADS):
        oh = o[:, h * LA_DV:(h + 1) * LA_DV]
        r = lax.rsqrt(jnp.mean(oh * oh, axis=-1, keepdims=True) + RMS_EPS)
        parts.append(oh * r * ng)
    on = jnp.concatenate(parts, axis=1) * _silu(z_ref[...])
    out = _bdot(on, w_ref[...])
    xr = ALPHA * x_ref[...] + g1_ref[0] * out
    y_ref[...] = _layer_norm_rows(xr, lng_ref[0:1, :], lnb_ref[0:1, :])


def _pad_rows(v):
    return jnp.zeros((SUBLANES, v.shape[0]), F32).at[0].set(v.astype(F32))


def _la_out(o_p, o_s, proj, x, mod, layer, n_prompt, dec_seq, norm_g, w_out, ln_g, ln_b):
    n, d = x.shape
    r = ROW_BLOCK
    zblk = LA_QKV // d
    sp, ss = _stream_specs(n_prompt, r, d)
    return pl.pallas_call(
        functools.partial(_la_out_kernel, npb=n_prompt // r),
        grid=(n // r,),
        in_specs=[
            sp, sp, ss, ss,
            pl.BlockSpec((r, d), lambda i: (i, zblk)),
            pl.BlockSpec((r, d), lambda i: (i, 0)),
            _mod_spec(layer, 2, n_prompt, dec_seq, r),
            pl.BlockSpec((SUBLANES, LA_DV), lambda i: (0, 0)),
            pl.BlockSpec((d, d), lambda i: (0, 0)),
            pl.BlockSpec((SUBLANES, d), lambda i: (0, 0)),
            pl.BlockSpec((SUBLANES, d), lambda i: (0, 0)),
        ],
        out_specs=pl.BlockSpec((r, d), lambda i: (i, 0)),
        out_shape=jax.ShapeDtypeStruct((n, d), F32),
        compiler_params=_cparams("parallel"),
        name="deltanet_out",
    )(o_p[0], o_p[1], o_s[0], o_s[1], proj, x, mod, _pad_rows(norm_g), w_out.astype(BF16), _pad_rows(ln_g),
      _pad_rows(ln_b))


def _out_kernel(op_ref, os_ref, x_ref, g1_ref, w_ref, lng_ref, lnb_ref, y_ref, *, npb):
    is_prompt = pl.program_id(0) < npb
    o = jnp.where(is_prompt, op_ref[...], os_ref[...])
    out = _bdot(o, w_ref[...])
    xr = ALPHA * x_ref[...] + g1_ref[0] * out
    y_ref[...] = _layer_norm_rows(xr, lng_ref[0:1, :], lnb_ref[0:1, :])


def _swa_out(o_p, o_s, x, mod, layer, n_prompt, dec_seq, w_out, ln_g, ln_b):
    n, d = x.shape
    r = ROW_BLOCK
    sp, ss = _stream_specs(n_prompt, r, d)
    return pl.pallas_call(
        functools.partial(_out_kernel, npb=n_prompt // r),
        grid=(n // r,),
        in_specs=[
            sp, ss,
            pl.BlockSpec((r, d), lambda i: (i, 0)),
            _mod_spec(layer, 2, n_prompt, dec_seq, r),
            pl.BlockSpec((d, d), lambda i: (0, 0)),
            pl.BlockSpec((SUBLANES, d), lambda i: (0, 0)),
            pl.BlockSpec((SUBLANES, d), lambda i: (0, 0)),
        ],
        out_specs=pl.BlockSpec((r, d), lambda i: (i, 0)),
        out_shape=jax.ShapeDtypeStruct((n, d), F32),
        compiler_params=_cparams("parallel"),
        name="swa_out",
    )(o_p, o_s, x, mod, w_out.astype(BF16), _pad_rows(ln_g), _pad_rows(ln_b))


SWA_SCALE = SWA_DH ** -0.5


def _stack_group(q, kv):
    base = kv * SWA_GROUP * SWA_DH
    return jnp.concatenate([q[:, base + g * SWA_DH: base + (g + 1) * SWA_DH] for g in range(SWA_GROUP)], axis=0)


def _sink_col(sink_ref, kv, rows):
    return jnp.concatenate(
        [jnp.full((rows, 1), sink_ref[kv * SWA_GROUP + g], F32) for g in range(SWA_GROUP)], axis=0)


def _ctx_attn_kernel(sink_ref, q_ref, k_ref, v_ref, o_ref):
    q = q_ref[...]
    k = k_ref[...]
    v = v_ref[...]
    t = q.shape[0]
    outs = []
    for kv in range(SWA_KV_HEADS):
        kh = k[:, kv * SWA_DH:(kv + 1) * SWA_DH]
        vh = v[:, kv * SWA_DH:(kv + 1) * SWA_DH]
        s = _bdot_nt(_stack_group(q, kv), kh) * SWA_SCALE
        sink = _sink_col(sink_ref, kv, t)
        m = jnp.maximum(jnp.max(s, axis=1, keepdims=True), sink)
        p = jnp.exp(s - m)
        den = jnp.sum(p, axis=1, keepdims=True) + jnp.exp(sink - m)
        o = _bdot(p, vh) / den
        outs.extend(o[g * t:(g + 1) * t, :] for g in range(SWA_GROUP))
    o_ref[...] = jnp.concatenate(outs, axis=1)


def _ctx_attention(qkv, sink, batch, seq):
    kcol = SWA_HEADS * SWA_DH // SWA_KVW
    return pl.pallas_call(
        _ctx_attn_kernel,
        grid=(batch,),
        in_specs=[
            pl.BlockSpec(memory_space=pltpu.SMEM),
            pl.BlockSpec((seq, SWA_HEADS * SWA_DH), lambda b: (b, 0)),
            pl.BlockSpec((seq, SWA_KVW), lambda b: (b, kcol)),
            pl.BlockSpec((seq, SWA_KVW), lambda b: (b, kcol + 1)),
        ],
        out_specs=pl.BlockSpec((seq, SWA_HEADS * SWA_DH), lambda b: (b, 0)),
        out_shape=jax.ShapeDtypeStruct((batch * seq, SWA_HEADS * SWA_DH), F32),
        compiler_params=_cparams("parallel"),
        name="swa_context_attention",
    )(sink.astype(F32), qkv, qkv, qkv)


ROPE_COLS = SWA_HEADS * SWA_DH + SWA_KVW
ROPE_SHIFT = SWA_DH // 4


def _rope_tables(t_len):
    t = jnp.arange(t_len)
    r = (t // GRID_W).astype(F32)
    col = (t % GRID_W).astype(F32)
    inv = ROPE_BASE ** (-jnp.arange(ROT_FREQS, dtype=F32) / ROT_FREQS)
    ang_r = r[:, None] * inv
    ang_c = col[:, None] * inv
    ang = jnp.concatenate([ang_r, ang_r, ang_c, ang_c], -1)
    cos = jnp.cos(ang)
    sin = jnp.sin(ang)
    first = (jnp.arange(SWA_DH) % (2 * ROPE_SHIFT)) < ROPE_SHIFT
    sin_up = jnp.where(first, -sin, 0.0)
    sin_dn = jnp.where(first, 0.0, sin)
    reps = LANES // SWA_DH
    return tuple(jnp.tile(a, (1, reps)) for a in (cos, sin_up, sin_dn))


def _rope_kernel(x_ref, cos_ref, su_ref, sd_ref, o_ref):
    cos = cos_ref[...]
    su = su_ref[...]
    sd = sd_ref[...]
    for j in range(ROPE_COLS // LANES):
        x = x_ref[:, j * LANES:(j + 1) * LANES]
        up = pltpu.roll(x, LANES - ROPE_SHIFT, 1)
        dn = pltpu.roll(x, ROPE_SHIFT, 1)
        o_ref[:, j * LANES:(j + 1) * LANES] = x * cos + up * su + dn * sd


def _rope_qk(qkv, row0, batch, seq):
    r = ROW_BLOCK
    nb = seq // r
    blk0 = row0 // r
    tables = _rope_tables(seq)
    tspec = pl.BlockSpec((r, LANES), lambda b, i: (i, 0))
    return pl.pallas_call(
        _rope_kernel,
        grid=(batch, nb),
        in_specs=[pl.BlockSpec((r, ROPE_COLS), lambda b, i: (blk0 + b * nb + i, 0)), tspec, tspec, tspec],
        out_specs=pl.BlockSpec((r, ROPE_COLS), lambda b, i: (b * nb + i, 0)),
        out_shape=jax.ShapeDtypeStruct((batch * seq, ROPE_COLS), F32),
        compiler_params=_cparams("parallel", "parallel"),
        name="swa_rope",
    )(qkv, *tables)


def _lat_attn_kernel(sink_ref, q_ref, kp_ref, kc_ref, kn_ref, vp_ref, vc_ref, vn_ref, ck_ref, cv_ref, o_ref, *, seq):
    i = pl.program_id(1)
    q = q_ref[...]
    kw = jnp.concatenate([kp_ref[...], kc_ref[...], kn_ref[...]], axis=0)
    vw = jnp.concatenate([vp_ref[...], vc_ref[...], vn_ref[...]], axis=0)
    ck = ck_ref[...]
    cv = cv_ref[...]
    span = 3 * Q_BLOCK
    rows = SWA_GROUP * Q_BLOCK
    q0 = i * Q_BLOCK
    qpos = q0 + lax.broadcasted_iota(jnp.int32, (rows, span), 0) % Q_BLOCK
    kpos = q0 - WINDOW + lax.broadcasted_iota(jnp.int32, (rows, span), 1)
    valid = (kpos >= 0) & (kpos < seq) & (jnp.abs(qpos - kpos) <= WINDOW)
    outs = []
    for kv in range(SWA_KV_HEADS):
        cols = slice(kv * SWA_DH, (kv + 1) * SWA_DH)
        qg = _stack_group(q, kv)
        s_win = jnp.where(valid, _bdot_nt(qg, kw[:, cols]) * SWA_SCALE, NEG_INF)
        s_ctx = _bdot_nt(qg, ck[:, cols]) * SWA_SCALE
        sink = _sink_col(sink_ref, kv, Q_BLOCK)
        m = jnp.maximum(jnp.maximum(jnp.max(s_win, axis=1, keepdims=True), jnp.max(s_ctx, axis=1, keepdims=True)),
                        sink)
        p_win = jnp.exp(s_win - m)
        p_ctx = jnp.exp(s_ctx - m)
        den = (jnp.sum(p_win, axis=1, keepdims=True) + jnp.sum(p_ctx, axis=1, keepdims=True)
               + jnp.exp(sink - m))
        o = (_bdot(p_win, vw[:, cols]) + _bdot(p_ctx, cv[:, cols])) / den
        outs.extend(o[g * Q_BLOCK:(g + 1) * Q_BLOCK, :] for g in range(SWA_GROUP))
    o_ref[...] = jnp.concatenate(outs, axis=1)


def _lat_attention(qk_rope, qkv, cache_k, cache_v, sink, row0, batch, seq):
    nq = seq // Q_BLOCK
    blk0 = row0 // Q_BLOCK
    qw = SWA_HEADS * SWA_DH
    kcol = qw // SWA_KVW
    past = cache_k.shape[0] // batch

    def kspec(off):
        return pl.BlockSpec((Q_BLOCK, SWA_KVW), lambda b, i: (b * nq + jnp.clip(i + off, 0, nq - 1), kcol))

    def vspec(off):
        return pl.BlockSpec((Q_BLOCK, SWA_KVW),
                            lambda b, i: (blk0 + b * nq + jnp.clip(i + off, 0, nq - 1), kcol + 1))

    cspec = pl.BlockSpec((past, SWA_KVW), lambda b, i: (b, 0))
    return pl.pallas_call(
        functools.partial(_lat_attn_kernel, seq=seq),
        grid=(batch, nq),
        in_specs=[
            pl.BlockSpec(memory_space=pltpu.SMEM),
            pl.BlockSpec((Q_BLOCK, qw), lambda b, i: (b * nq + i, 0)),
            kspec(-1), kspec(0), kspec(1),
            vspec(-1), vspec(0), vspec(1),
            cspec, cspec,
        ],
        out_specs=pl.BlockSpec((Q_BLOCK, qw), lambda b, i: (b * nq + i, 0)),
        out_shape=jax.ShapeDtypeStruct((batch * seq, qw), F32),
        compiler_params=_cparams("parallel", "parallel"),
        name="swa_latent_attention",
    )(sink.astype(F32), qk_rope, qk_rope, qk_rope, qk_rope, qkv, qkv, qkv, cache_k, cache_v)


GROUP_SIZE = N_EXPERTS // N_GROUPS


def _first_argmax(v, iota, axis, size):
    m = jnp.max(v, axis=axis, keepdims=True)
    idx = jnp.min(jnp.where(v == m, iota, size), axis=axis, keepdims=True)
    return m, idx


def _router_kernel(x_ref, sc_ref, sh_ref, rt_ref, bias_ref, hs_ref, idx_ref, w_ref, pos_ref, wtm_ref, cnt_ref):
    i = pl.program_id(0)
    rows = x_ref.shape[0]

    @pl.when(i == 0)
    def _():
        cnt_ref[...] = jnp.zeros_like(cnt_ref)

    hs = x_ref[...] * (1.0 + sc_ref[0]) + sh_ref[0]
    hs_ref[...] = hs
    logits = lax.dot_general(rt_ref[...], hs, (((1,), (1,)), ((), ())), precision=HIGHEST,
                             preferred_element_type=F32)
    scores = jax.nn.sigmoid(logits)
    biased = scores + bias_ref[:, 0:1]
    b3 = biased.reshape(N_GROUPS, GROUP_SIZE, rows)
    mem = lax.broadcasted_iota(jnp.int32, b3.shape, 1)
    m1, i1 = _first_argmax(b3, mem, 1, GROUP_SIZE)
    m2 = jnp.max(jnp.where(mem == i1, -jnp.inf, b3), axis=1, keepdims=True)
    grp = (m1 + m2).reshape(N_GROUPS, rows)
    giota = lax.broadcasted_iota(jnp.int32, grp.shape, 0)
    gsel = jnp.zeros(grp.shape, jnp.bool_)
    for _ in range(TOPK_GROUPS):
        _, gi = _first_argmax(grp, giota, 0, N_GROUPS)
        hit = giota == gi
        gsel = gsel | hit
        grp = jnp.where(hit, -jnp.inf, grp)
    gmask = jnp.broadcast_to(gsel.reshape(N_GROUPS, 1, rows), b3.shape).reshape(N_EXPERTS, rows)
    sel = jnp.where(gmask, biased, -jnp.inf)
    eiota = lax.broadcasted_iota(jnp.int32, sel.shape, 0)
    chosen_any = jnp.zeros(sel.shape, jnp.bool_)
    idx_rows, w_rows, hits = [], [], []
    for _ in range(TOP_K):
        _, ei = _first_argmax(sel, eiota, 0, N_EXPERTS)
        hit = eiota == ei
        chosen_any = chosen_any | hit
        idx_rows.append(ei)
        w_rows.append(jnp.sum(jnp.where(hit, scores, 0.0), axis=0, keepdims=True))
        hits.append(hit)
        sel = jnp.where(hit, -jnp.inf, sel)
    wsum = w_rows[0]
    for w in w_rows[1:]:
        wsum = wsum + w
    w_rows = [w / wsum * ROUTED_SCALE for w in w_rows]
    onehot = chosen_any.astype(F32)
    s_i = lax.broadcasted_iota(jnp.int32, (rows, rows), 0)
    t_i = lax.broadcasted_iota(jnp.int32, (rows, rows), 1)
    earlier = (s_i < t_i).astype(F32)
    rank = cnt_ref[:, 0:1] + _bdot(onehot, earlier)
    cnt_ref[...] = cnt_ref[...] + jnp.sum(onehot, axis=1, keepdims=True)
    pos_rows = [jnp.sum(jnp.where(hit, rank, 0.0), axis=0, keepdims=True) for hit in hits]
    idx_ref[...] = jnp.concatenate(idx_rows, axis=0)
    w8 = jnp.concatenate(w_rows, axis=0)
    w_ref[...] = w8
    pos_ref[...] = jnp.concatenate(pos_rows, axis=0).astype(jnp.int32)
    wtm_ref[...] = jnp.concatenate([w8, jnp.zeros((LANES - TOP_K, rows), F32)], axis=0).T


def _router(x, mod, layer, n_prompt, dec_seq, router, bias):
    n, d = x.shape
    r = ROW_BLOCK
    bias_pad = jnp.zeros((N_EXPERTS, LANES), F32).at[:, 0].set(bias.astype(F32))
    kspec = pl.BlockSpec((TOP_K, r), lambda i: (0, i))
    return pl.pallas_call(
        _router_kernel,
        grid=(n // r,),
        in_specs=[
            pl.BlockSpec((r, d), lambda i: (i, 0)),
            _mod_spec(layer, 4, n_prompt, dec_seq, r),
            _mod_spec(layer, 3, n_prompt, dec_seq, r),
            pl.BlockSpec((N_EXPERTS, d), lambda i: (0, 0)),
            pl.BlockSpec((N_EXPERTS, LANES), lambda i: (0, 0)),
        ],
        out_specs=[
            pl.BlockSpec((r, d), lambda i: (i, 0)),
            kspec, kspec, kspec,
            pl.BlockSpec((r, LANES), lambda i: (i, 0)),
            pl.BlockSpec((N_EXPERTS, LANES), lambda i: (0, 0)),
        ],
        out_shape=[
            jax.ShapeDtypeStruct((n, d), F32),
            jax.ShapeDtypeStruct((TOP_K, n), jnp.int32),
            jax.ShapeDtypeStruct((TOP_K, n), F32),
            jax.ShapeDtypeStruct((TOP_K, n), jnp.int32),
            jax.ShapeDtypeStruct((n, LANES), F32),
            jax.ShapeDtypeStruct((N_EXPERTS, LANES), F32),
        ],
        compiler_params=_cparams("arbitrary"),
        name="moe_router",
    )(x, mod, mod, router.T.astype(F32), bias_pad)


DISPATCH_TOKENS = 256


def _row_copy(src_ref, src_row, dst_ref, dst_row, sem):
    return pltpu.make_async_copy(src_ref.at[pl.ds(src_row, 1), :], dst_ref.at[pl.ds(dst_row, 1), :], sem)


def _dispatch_kernel(dest_ref, hs_ref, xs_in_ref, xs_ref, sem):
    del xs_in_ref
    tokens = hs_ref.shape[0]

    def start(t, carry):
        for k in range(TOP_K):
            _row_copy(hs_ref, t, xs_ref, dest_ref[k, t], sem).start()
        return carry

    lax.fori_loop(0, tokens, start, 0)

    def wait(t, carry):
        for k in range(TOP_K):
            _row_copy(hs_ref, t, xs_ref, dest_ref[k, t], sem).wait()
        return carry

    lax.fori_loop(0, tokens, wait, 0)


def _dispatch(hs, dest, n_rows):
    n, d = hs.shape
    t = DISPATCH_TOKENS
    return pl.pallas_call(
        _dispatch_kernel,
        grid=(n // t,),
        in_specs=[
            pl.BlockSpec((TOP_K, t), lambda i: (0, i), memory_space=pltpu.SMEM),
            pl.BlockSpec((t, d), lambda i: (i, 0)),
            pl.BlockSpec(memory_space=pl.ANY),
        ],
        out_specs=pl.BlockSpec(memory_space=pl.ANY),
        out_shape=jax.ShapeDtypeStruct((n_rows, d), F32),
        scratch_shapes=[pltpu.SemaphoreType.DMA],
        input_output_aliases={2: 0},
        compiler_params=_cparams("arbitrary"),
        name="moe_dispatch",
    )(dest, hs, jnp.zeros((n_rows, d), F32))


def _expert_kernel(be_ref, na_ref, xs_ref, wg_ref, wu_ref, wd_ref, y_ref):
    del be_ref
    i = pl.program_id(0)

    @pl.when(i < na_ref[0])
    def _():
        x = xs_ref[...]
        h = _silu(_bdot(x, wg_ref[0])) * _bdot(x, wu_ref[0])
        y_ref[...] = _bdot(h, wd_ref[0])

    @pl.when(i >= na_ref[0])
    def _():
        y_ref[...] = jnp.zeros_like(y_ref)


def _experts(xs, blk_expert, n_active, w_gate, w_up, w_down):
    n_rows, d = xs.shape
    nb = n_rows // MOE_ROWS
    de = w_gate.shape[2]
    grid_spec = pltpu.PrefetchScalarGridSpec(
        num_scalar_prefetch=2,
        grid=(nb,),
        in_specs=[
            pl.BlockSpec((MOE_ROWS, d), lambda i, be, na: (i, 0)),
            pl.BlockSpec((1, d, de), lambda i, be, na: (be[i], 0, 0)),
            pl.BlockSpec((1, d, de), lambda i, be, na: (be[i], 0, 0)),
            pl.BlockSpec((1, de, d), lambda i, be, na: (be[i], 0, 0)),
        ],
        out_specs=pl.BlockSpec((MOE_ROWS, d), lambda i, be, na: (i, 0)),
    )
    return pl.pallas_call(
        _expert_kernel,
        grid_spec=grid_spec,
        out_shape=jax.ShapeDtypeStruct((n_rows, d), F32),
        compiler_params=_cparams("arbitrary"),
        name="moe_experts",
    )(blk_expert, n_active, xs, w_gate.astype(BF16), w_up.astype(BF16), w_down.astype(BF16))


COMBINE_TOKENS = 128


def _combine_kernel(dest_ref, y_ref, wtm_ref, hs_ref, x_ref, g2_ref, sg_ref, su_ref, sd_ref, lng_ref, lnb_ref,
                    o_ref, ybuf, sem):
    tokens = hs_ref.shape[0]

    def start(t, carry):
        for k in range(TOP_K):
            _row_copy(y_ref, dest_ref[k, t], ybuf.at[k], t, sem).start()
        return carry

    lax.fori_loop(0, tokens, start, 0)
    hs = hs_ref[...]
    shared = _bdot(_silu(_bdot(hs, sg_ref[...])) * _bdot(hs, su_ref[...]), sd_ref[...])

    def wait(t, carry):
        for k in range(TOP_K):
            _row_copy(y_ref, dest_ref[k, t], ybuf.at[k], t, sem).wait()
        return carry

    lax.fori_loop(0, tokens, wait, 0)
    wtm = wtm_ref[...]
    routed = ybuf[0] * wtm[:, 0:1]
    for k in range(1, TOP_K):
        routed = routed + ybuf[k] * wtm[:, k:k + 1]
    xr = ALPHA * x_ref[...] + g2_ref[0] * (routed + shared)
    o_ref[...] = _layer_norm_rows(xr, lng_ref[0:1, :], lnb_ref[0:1, :])


def _combine(y, dest, wtm, hs, x, mod, layer, n_prompt, dec_seq, s_gate, s_up, s_down, ln_g, ln_b):
    n, d = x.shape
    t = COMBINE_TOKENS
    ds = s_gate.shape[1]
    row = pl.BlockSpec((t, d), lambda i: (i, 0))
    return pl.pallas_call(
        _combine_kernel,
        grid=(n // t,),
        in_specs=[
            pl.BlockSpec((TOP_K, t), lambda i: (0, i), memory_space=pltpu.SMEM),
            pl.BlockSpec(memory_space=pl.ANY),
            pl.BlockSpec((t, LANES), lambda i: (i, 0)),
            row, row,
            _mod_spec(layer, 5, n_prompt, dec_seq, t),
            pl.BlockSpec((d, ds), lambda i: (0, 0)),
            pl.BlockSpec((d, ds), lambda i: (0, 0)),
            pl.BlockSpec((ds, d), lambda i: (0, 0)),
            pl.BlockSpec((SUBLANES, d), lambda i: (0, 0)),
            pl.BlockSpec((SUBLANES, d), lambda i: (0, 0)),
        ],
        out_specs=row,
        out_shape=jax.ShapeDtypeStruct((n, d), F32),
        scratch_shapes=[pltpu.VMEM((TOP_K, t, d), F32), pltpu.SemaphoreType.DMA],
        compiler_params=_cparams("arbitrary"),
        name="moe_combine",
    )(dest, y, wtm, hs, x, mod, s_gate.astype(BF16), s_up.astype(BF16), s_down.astype(BF16), _pad_rows(ln_g),
      _pad_rows(ln_b))


def _moe(x, mod, layer, n_prompt, dec_seq, router, bias, w_gate, w_up, w_down, s_gate, s_up, s_down, ln_g, ln_b):
    n, d = x.shape
    hs, idx, _, pos, wtm, cnt = _router(x, mod, layer, n_prompt, dec_seq, router, bias)
    counts = cnt[:, 0].astype(jnp.int32)
    padded = (counts + MOE_ROWS - 1) // MOE_ROWS * MOE_ROWS
    pad_end = jnp.cumsum(padded)
    pad_start = pad_end - padded
    dest = pad_start[idx] + pos
    nb = n * TOP_K // MOE_ROWS + N_EXPERTS
    blk_expert = jnp.minimum(
        jnp.searchsorted(pad_end, jnp.arange(nb, dtype=jnp.int32) * MOE_ROWS, side="right"), N_EXPERTS - 1
    ).astype(jnp.int32)
    n_active = (pad_end[-1:] // MOE_ROWS).astype(jnp.int32)
    xs = _dispatch(hs, dest, nb * MOE_ROWS)
    y = _experts(xs, blk_expert, n_active, w_gate, w_up, w_down)
    return _combine(y, dest, wtm, hs, x, mod, layer, n_prompt, dec_seq, s_gate, s_up, s_down, ln_g, ln_b)


def kernel(x_prompt, x_sample, state_la, cache_k, cache_v, c, c_ctx, w_mod, b_mod, ln_g, ln_b,
           la_w_in, la_conv, la_a_log, la_dt_bias, la_norm_g, la_w_out,
           swa_w_qkv, swa_sink, swa_w_out,
           moe_router, moe_bias, moe_w_gate, moe_w_up, moe_w_down, sh_w_gate, sh_w_up, sh_w_down):
    bp, tp, d = x_prompt.shape
    bs, ts, _ = x_sample.shape
    n_prompt = bp * tp
    assert bs + 1 <= COND_ROWS and d == D_MODEL
    assert n_prompt % ROW_BLOCK == 0 and ts % ROW_BLOCK == 0 and tp % DELTA_ROWS == 0

    cond = jnp.zeros((COND_ROWS, d), F32).at[0].set(c_ctx).at[1:1 + bs].set(c)
    mod = _modulation(cond, w_mod, b_mod)
    x = jnp.concatenate([x_prompt.reshape(n_prompt, d), x_sample.reshape(bs * ts, d)], axis=0)

    def moe(x, i):
        return _moe(x, mod, i, n_prompt, ts, moe_router[i], moe_bias[i], moe_w_gate[i], moe_w_up[i], moe_w_down[i],
                    sh_w_gate[i], sh_w_up[i], sh_w_down[i], ln_g[i, 1], ln_b[i, 1])

    proj, gbc, gbr = _la_in_proj(x, mod, 0, n_prompt, ts, la_w_in[0], la_a_log[0], la_dt_bias[0])
    qkv_p = _conv_qkv(proj, la_conv[0], 0, bp, tp)
    qkv_s = _conv_qkv(proj, la_conv[0], n_prompt, bs, ts)
    o_p, o_s, states = [], [], []
    for dirn in range(2):
        o, s_fin = _delta_dir(qkv_p, gbc, gbr, None, 0, bp, tp, dirn)
        o_p.append(o)
        states.append(s_fin.reshape(bp, LA_HEADS, LA_DK, LA_DV))
        s0 = state_la[:, 0, dirn].reshape(bs * LA_HEADS, LA_DK, LA_DV)
        o, _ = _delta_dir(qkv_s, gbc, gbr, s0, n_prompt, bs, ts, dirn)
        o_s.append(o)
    x = _la_out(o_p, o_s, proj, x, mod, 0, n_prompt, ts, la_norm_g[0], la_w_out[0], ln_g[0, 0], ln_b[0, 0])
    new_la = jnp.stack(states, axis=1)[:, None]
    x = moe(x, 0)

    qkv = _mod_proj(x, mod, 1, n_prompt, ts, swa_w_qkv[0], "swa_qkv_proj")
    qw = SWA_HEADS * SWA_DH
    new_k = qkv[:n_prompt, qw:qw + SWA_KVW].reshape(bp, 1, tp, SWA_KV_HEADS, SWA_DH)
    new_v = qkv[:n_prompt, qw + SWA_KVW:].reshape(bp, 1, tp, SWA_KV_HEADS, SWA_DH)
    a_p = _ctx_attention(qkv, swa_sink[0], bp, tp)
    qk_rope = _rope_qk(qkv, n_prompt, bs, ts)
    past = cache_k.shape[2]
    a_s = _lat_attention(qk_rope, qkv, cache_k[:, 0].reshape(bs * past, SWA_KVW),
                         cache_v[:, 0].reshape(bs * past, SWA_KVW), swa_sink[0], n_prompt, bs, ts)
    x = _swa_out(a_p, a_s, x, mod, 1, n_prompt, ts, swa_w_out[0], ln_g[1, 0], ln_b[1, 0])
    x = moe(x, 1)

    y_prompt = x[:n_prompt].reshape(bp, tp, d)
    y_sample = x[n_prompt:].reshape(bs, ts, d)
    return y_prompt, y_sample, new_la, new_k, new_v
```

```python
import functools
import math

import jax
import jax.numpy as jnp
from jax import lax
from jax.experimental import pallas as pl
from jax.experimental.pallas import tpu as pltpu

F32 = jnp.float32
BF16 = jnp.bfloat16
HIGHEST = lax.Precision.HIGHEST

LANES = 128
SUBLANES = 8
VMEM_LIMIT_BYTES = 56 * 1024 * 1024

D_MODEL = 1024
DEPTH = 2
GRID_W = 64
LA_DK = 128
LA_DV = 128
LA_HEADS = D_MODEL // LA_DK
LA_QKV = LA_HEADS * (2 * LA_DK + LA_DV)
CONV_K = 5
CHUNK = 64
SUB = 16
SWA_DH = 64
SWA_HEADS = D_MODEL // SWA_DH
SWA_KV_HEADS = SWA_HEADS // 4
SWA_GROUP = SWA_HEADS // SWA_KV_HEADS
SWA_KVW = SWA_KV_HEADS * SWA_DH
WINDOW = 128
Q_BLOCK = 128
ROT_FREQS = SWA_DH // 4
ROPE_BASE = 10000.0
NEG_INF = -1e30
N_EXPERTS = 64
TOP_K = 8
N_GROUPS = 8
TOPK_GROUPS = 4
D_EXPERT = 256
D_SHARED = 256
ROUTED_SCALE = 2.5
ALPHA = (2 * DEPTH) ** 0.25
LN_EPS = 1e-5
RMS_EPS = 1e-6

ROW_BLOCK = 256
MOE_ROWS = 256
COND_ROWS = 8


def _cparams(*sem):
    return pltpu.CompilerParams(dimension_semantics=sem, vmem_limit_bytes=VMEM_LIMIT_BYTES)


def _bdot(a, b):
    return jnp.dot(a.astype(BF16), b.astype(BF16), preferred_element_type=F32)


def _bdot_nt(a, b):
    return lax.dot_general(a.astype(BF16), b.astype(BF16), (((1,), (1,)), ((), ())), preferred_element_type=F32)


def _bdot_tn(a, b):
    return lax.dot_general(a.astype(BF16), b.astype(BF16), (((0,), (0,)), ((), ())), preferred_element_type=F32)


def _silu(x):
    return x * jax.nn.sigmoid(x)


def _layer_norm_rows(x, g, b):
    mu = jnp.mean(x, axis=-1, keepdims=True)
    xc = x - mu
    var = jnp.mean(xc * xc, axis=-1, keepdims=True)
    return xc * lax.rsqrt(var + LN_EPS) * g + b


MOD_COLS = 512


def _mod_kernel(c_ref, w_ref, b_ref, o_ref):
    s = _silu(c_ref[...])
    o_ref[0] = jnp.dot(s, w_ref[0], precision=HIGHEST, preferred_element_type=F32) + b_ref[0]


def _modulation(cond, w_mod, b_mod):
    depth, d, n6 = w_mod.shape
    out = pl.pallas_call(
        _mod_kernel,
        grid=(depth, n6 // MOD_COLS),
        in_specs=[
            pl.BlockSpec((COND_ROWS, d), lambda l, j: (0, 0)),
            pl.BlockSpec((1, d, MOD_COLS), lambda l, j: (l, 0, j)),
            pl.BlockSpec((1, 1, MOD_COLS), lambda l, j: (l, 0, j)),
        ],
        out_specs=pl.BlockSpec((1, COND_ROWS, MOD_COLS), lambda l, j: (l, 0, j)),
        out_shape=jax.ShapeDtypeStruct((depth, COND_ROWS, n6), F32),
        compiler_params=_cparams("parallel", "parallel"),
        name="adaln_modulation",
    )(cond, w_mod, b_mod.reshape(depth, 1, n6))
    return out.reshape(depth * COND_ROWS * 6, 1, d)


def _mod_spec(layer, chunk, n_prompt, dec_seq, rows):
    def index(i, *_):
        tok = i * rows
        row = jnp.where(tok < n_prompt, 0, 1 + (tok - n_prompt) // dec_seq)
        return ((layer * COND_ROWS + row) * 6 + chunk, 0, 0)

    return pl.BlockSpec((1, 1, D_MODEL), index)


def _la_in_kernel(x_ref, sc_ref, sh_ref, w_ref, wab_ref, wabt_ref, pc_ref, pr_ref, o_ref, gbc_ref, gbr_ref):
    xm = x_ref[...] * (1.0 + sc_ref[0]) + sh_ref[0]
    o_ref[...] = _bdot(xm, w_ref[...])
    ab = jnp.dot(xm, wab_ref[...], precision=HIGHEST, preferred_element_type=F32)
    lane = lax.broadcasted_iota(jnp.int32, ab.shape, 1)
    neg_a = pc_ref[0:1, :]
    dt_b = pc_ref[1:2, :]
    z = ab + dt_b
    softplus = jnp.maximum(z, 0.0) + jnp.log1p(jnp.exp(-jnp.abs(z)))
    gbc_ref[...] = jnp.where(lane < 2 * LA_HEADS, neg_a * softplus, jax.nn.sigmoid(ab))
    rows = x_ref.shape[0]
    for c in range(rows // CHUNK):
        xc = xm[c * CHUNK:(c + 1) * CHUNK, :]
        abt = lax.dot_general(wabt_ref[...], xc, (((1,), (1,)), ((), ())), precision=HIGHEST,
                              preferred_element_type=F32)
        zt = abt + pr_ref[:, 1:2]
        spt = jnp.maximum(zt, 0.0) + jnp.log1p(jnp.exp(-jnp.abs(zt)))
        gbr_ref[c] = pr_ref[:, 0:1] * spt


def _la_in_proj(x, mod, layer, n_prompt, dec_seq, w_in, a_log, dt_bias):
    n, d = x.shape
    wide = LA_QKV + LA_HEADS * LA_DV
    w_main = w_in[:, :wide].astype(BF16)
    w_ab = w_in[:, wide:]
    n_ab = w_ab.shape[1]
    w_ab_pad = jnp.zeros((d, LANES), F32).at[:, :n_ab].set(w_ab)
    neg_a = -jnp.exp(a_log.astype(F32)).reshape(-1)
    dtb = dt_bias.astype(F32).reshape(-1)
    pc = jnp.zeros((SUBLANES, LANES), F32).at[0, :2 * LA_HEADS].set(neg_a).at[1, :2 * LA_HEADS].set(dtb)
    pr = jnp.zeros((n_ab, LANES), F32).at[:2 * LA_HEADS, 0].set(neg_a).at[:2 * LA_HEADS, 1].set(dtb)
    r = ROW_BLOCK
    return pl.pallas_call(
        _la_in_kernel,
        grid=(n // r,),
        in_specs=[
            pl.BlockSpec((r, d), lambda i: (i, 0)),
            _mod_spec(layer, 1, n_prompt, dec_seq, r),
            _mod_spec(layer, 0, n_prompt, dec_seq, r),
            pl.BlockSpec((d, wide), lambda i: (0, 0)),
            pl.BlockSpec((d, LANES), lambda i: (0, 0)),
            pl.BlockSpec((n_ab, d), lambda i: (0, 0)),
            pl.BlockSpec((SUBLANES, LANES), lambda i: (0, 0)),
            pl.BlockSpec((n_ab, LANES), lambda i: (0, 0)),
        ],
        out_specs=[
            pl.BlockSpec((r, wide), lambda i: (i, 0)),
            pl.BlockSpec((r, LANES), lambda i: (i, 0)),
            pl.BlockSpec((r // CHUNK, n_ab, CHUNK), lambda i: (i, 0, 0)),
        ],
        out_shape=[
            jax.ShapeDtypeStruct((n, wide), F32),
            jax.ShapeDtypeStruct((n, LANES), F32),
            jax.ShapeDtypeStruct((n // CHUNK, n_ab, CHUNK), F32),
        ],
        compiler_params=_cparams("parallel"),
        name="deltanet_in_proj",
    )(x, mod, mod, w_main, w_ab_pad, w_ab.T, pc, pr)


def _proj_kernel(x_ref, sc_ref, sh_ref, w_ref, o_ref):
    xm = x_ref[...] * (1.0 + sc_ref[0]) + sh_ref[0]
    o_ref[...] = _bdot(xm, w_ref[...])


def _mod_proj(x, mod, layer, n_prompt, dec_seq, w, name):
    n, d = x.shape
    cols = w.shape[1]
    r = ROW_BLOCK
    return pl.pallas_call(
        _proj_kernel,
        grid=(n // r,),
        in_specs=[
            pl.BlockSpec((r, d), lambda i: (i, 0)),
            _mod_spec(layer, 1, n_prompt, dec_seq, r),
            _mod_spec(layer, 0, n_prompt, dec_seq, r),
            pl.BlockSpec((d, cols), lambda i: (0, 0)),
        ],
        out_specs=pl.BlockSpec((r, cols), lambda i: (i, 0)),
        out_shape=jax.ShapeDtypeStruct((n, cols), F32),
        compiler_params=_cparams("parallel"),
        name=name,
    )(x, mod, mod, w.astype(BF16))


CONV_PAD = SUBLANES
CONV_ROWS = 256


def _conv_kernel(x_ref, w_ref, o_ref, pad_ref, *, seq):
    c = pl.program_id(1)
    zeros = jnp.zeros((CONV_PAD, LANES), F32)
    pad_ref[0:CONV_PAD, :] = zeros
    pad_ref[CONV_PAD + seq:, :] = zeros
    pad_ref[CONV_PAD:CONV_PAD + seq, :] = x_ref[...]
    is_qk = c < 2 * LA_HEADS
    scale = jnp.where(c < LA_HEADS, LA_DK ** -0.5, 1.0).astype(F32)
    half = CONV_K // 2
    for r0 in range(0, seq, CONV_ROWS):
        acc = jnp.zeros((CONV_ROWS, LANES), F32)
        for j in range(CONV_K):
            start = r0 + CONV_PAD + j - half
            acc = acc + w_ref[j:j + 1, :] * pad_ref[start:start + CONV_ROWS, :]
        y = _silu(acc)
        nrm = lax.rsqrt(jnp.sum(y * y, axis=-1, keepdims=True) + 1e-6) * scale
        o_ref[r0:r0 + CONV_ROWS, :] = y * jnp.where(is_qk, nrm, 1.0)


def _conv_qkv(proj, conv_w, row0, batch, seq):
    w = jnp.zeros((SUBLANES, LA_QKV), F32).at[:CONV_K].set(conv_w.astype(F32))
    blk0 = row0 // seq
    return pl.pallas_call(
        functools.partial(_conv_kernel, seq=seq),
        grid=(batch, LA_QKV // LANES),
        in_specs=[
            pl.BlockSpec((seq, LANES), lambda b, c: (blk0 + b, c)),
            pl.BlockSpec((SUBLANES, LANES), lambda b, c: (0, c)),
        ],
        out_specs=pl.BlockSpec((seq, LANES), lambda b, c: (b, c)),
        out_shape=jax.ShapeDtypeStruct((batch * seq, LA_QKV), F32),
        scratch_shapes=[pltpu.VMEM((seq + 2 * CONV_PAD, LANES), F32)],
        compiler_params=_cparams("parallel", "parallel"),
        name="deltanet_conv",
    )(proj, w)


DELTA_ROWS = 256
PAIR = 2 * CHUNK
N_DIRS = 2


PREP_HEADS = 2


def _pair_masks(rev):
    ii = lax.broadcasted_iota(jnp.int32, (PAIR, PAIR), 0)
    jj = lax.broadcasted_iota(jnp.int32, (PAIR, PAIR), 1)
    same_chunk = (ii // CHUNK) == (jj // CHUNK)
    same_sub = (ii // SUB) == (jj // SUB)
    if rev:
        return same_chunk, same_sub, same_chunk & (jj >= ii), same_chunk & (jj > ii), same_chunk & (ii >= jj)
    return same_chunk, same_sub, same_chunk & (jj <= ii), same_chunk & (jj < ii), same_chunk & (ii <= jj)


def _delta_prep_kernel(q_ref, k_ref, v_ref, gbc_ref, gbr_ref, u_ref, wq_ref, kd_ref, qk_ref, gl_ref):
    hp = pl.program_id(1)
    lane = lax.broadcasted_iota(jnp.int32, (PAIR, LANES), 1)
    cpp = PAIR // CHUNK
    masks = [_pair_masks(rev) for rev in (False, True)]
    probs = []
    for hh in range(PREP_HEADS):
        cols = slice(hh * LA_DK, (hh + 1) * LA_DK)
        for pr in range(DELTA_ROWS // PAIR):
            rows = slice(pr * PAIR, (pr + 1) * PAIR)
            q = q_ref[rows, cols]
            k = k_ref[rows, cols]
            v = v_ref[rows, cols]
            gbc = gbc_ref[rows, :]
            gram = _bdot_nt(k, k)
            qk_raw = _bdot_nt(q, k)
            for d in range(N_DIRS):
                same_chunk, same_sub, before, strict, before_t = masks[d]
                h = hp * PREP_HEADS + hh
                g_lane = d * LA_HEADS + h
                b_lane = (N_DIRS + d) * LA_HEADS + h
                g_col = jnp.sum(jnp.where(lane == g_lane, gbc, 0.0), axis=1, keepdims=True)
                b_col = jnp.sum(jnp.where(lane == b_lane, gbc, 0.0), axis=1, keepdims=True)
                g_row = jnp.concatenate([gbr_ref[pr * cpp + c, pl.ds(g_lane, 1), :] for c in range(cpp)], axis=1)
                gam_col = jnp.sum(jnp.where(before, g_row, 0.0), axis=1, keepdims=True)
                gam_row = jnp.sum(jnp.where(before_t, g_col, 0.0), axis=0, keepdims=True)
                tot_col = jnp.sum(jnp.where(same_chunk, g_row, 0.0), axis=1, keepdims=True)
                tot_row = jnp.sum(jnp.where(same_chunk, g_col, 0.0), axis=0, keepdims=True)
                dec = jnp.where(before, jnp.exp(jnp.where(before, gam_col - gam_row, 0.0)), 0.0)
                e_col = jnp.exp(gam_col)
                nm = jnp.where(strict, -(gram * b_col * dec), 0.0)
                nd = jnp.where(same_sub, nm, 0.0)
                probs.append(dict(
                    hh=hh, pr=pr, d=d, rows=rows, nd=nd, ne=nm - nd,
                    rhs=jnp.concatenate([v * b_col, k * (b_col * e_col)], axis=1),
                    qd=q * e_col, kd=k * jnp.exp(tot_col - gam_col), qkm=qk_raw * dec, gl=jnp.exp(tot_row)))
    for p in probs:
        p["yd"] = p["nd"]
        p["p"] = p["nd"]
    for _ in range(3):
        for p in probs:
            p["p"] = _bdot(p["p"], p["p"])
        for p in probs:
            p["yd"] = p["yd"] + p["p"] + _bdot(p["yd"], p["p"])
    for p in probs:
        p["f"] = p["ne"] + _bdot(p["yd"], p["ne"])
    for p in probs:
        p["f2"] = _bdot(p["f"], p["f"])
    for p in probs:
        p["gm"] = p["f"] + p["f2"] + _bdot(p["f"], p["f2"])
    for p in probs:
        p["yt"] = p["gm"] + p["yd"] + _bdot(p["gm"], p["yd"])
    for p in probs:
        p["sol"] = p["rhs"] + _bdot(p["yt"], p["rhs"])
    for p in probs:
        d, hh, pr, rows = p["d"], p["hh"], p["pr"], p["rows"]
        u = p["sol"][:, :LA_DV]
        w = p["sol"][:, LA_DV:]
        u_ref[d, hh, rows, :] = u
        kd_ref[d, hh, rows, :] = p["kd"].T.astype(BF16)
        qk_ref[d, hh, rows, :] = p["qkm"].astype(BF16)
        for c in range(cpp):
            cg = pr * cpp + c
            cr = slice(c * CHUNK, (c + 1) * CHUNK)
            wq_ref[d, hh, cg * PAIR:cg * PAIR + CHUNK, :] = w[cr, :].astype(BF16)
            wq_ref[d, hh, cg * PAIR + CHUNK:(cg + 1) * PAIR, :] = p["qd"][cr, :].astype(BF16)
            gl_ref[d, hh, cg] = p["gl"][:, c * CHUNK:c * CHUNK + 1] + jnp.zeros((1, LANES), F32)


def _delta_prep(qkv, gbc, gbr, row0):
    n = qkv.shape[0]
    r = DELTA_ROWS
    g0 = row0 // r
    n_ab = gbr.shape[1]
    cpb = r // CHUNK
    hg = LA_HEADS // PREP_HEADS
    hw = PREP_HEADS * LA_DK
    hm = lambda rows, dt: jax.ShapeDtypeStruct((N_DIRS, LA_HEADS, rows, LANES), dt)
    hspec = lambda rows: pl.BlockSpec((N_DIRS, PREP_HEADS, rows, LANES), lambda i, h: (0, h, i, 0))
    return pl.pallas_call(
        _delta_prep_kernel,
        grid=(n // r, hg),
        in_specs=[
            pl.BlockSpec((r, hw), lambda i, h: (i, h)),
            pl.BlockSpec((r, hw), lambda i, h: (i, hg + h)),
            pl.BlockSpec((r, hw), lambda i, h: (i, 2 * hg + h)),
            pl.BlockSpec((r, LANES), lambda i, h: (g0 + i, 0)),
            pl.BlockSpec((cpb, n_ab, CHUNK), lambda i, h: (g0 + i, 0, 0)),
        ],
        out_specs=[
            hspec(r), hspec(2 * r), hspec(r), hspec(r),
            pl.BlockSpec((N_DIRS, PREP_HEADS, cpb, 1, LANES), lambda i, h: (0, h, i, 0, 0)),
        ],
        out_shape=[
            hm(n, F32), hm(2 * n, BF16), hm(n, BF16), hm(n, BF16),
            jax.ShapeDtypeStruct((N_DIRS, LA_HEADS, n // CHUNK, 1, LANES), F32),
        ],
        compiler_params=_cparams("parallel", "parallel"),
        name="deltanet_prep",
    )(qkv, qkv, qkv, gbc, gbr)


def _delta_scan_kernel(*refs, has_s0):
    u_refs, wq_refs, kd_refs, qk_refs, gl_refs = (refs[0:2], refs[2:4], refs[4:6], refs[6:8], refs[8:10])
    if has_s0:
        s0_ref, of_ref, ob_ref, sfin_ref, s_ref = refs[10:]
    else:
        of_ref, ob_ref, sfin_ref, s_ref = refs[10:]
    o_refs = (of_ref, ob_ref)
    t = pl.program_id(1)

    @pl.when(t == 0)
    def _():
        if has_s0:
            s_ref[...] = s0_ref[0]
        else:
            s_ref[...] = jnp.zeros_like(s_ref)

    n_pairs = DELTA_ROWS // PAIR
    cpp = PAIR // CHUNK
    chains = [(d, h) for d in range(N_DIRS) for h in range(LA_HEADS)]
    zeros = jnp.zeros((CHUNK, LA_DV), F32)
    for step in range(n_pairs):
        pair_of = [n_pairs - 1 - step if d == 1 else step for d in range(N_DIRS)]
        s = [s_ref[d * LA_HEADS + h] for d, h in chains]
        v_new = [[None] * cpp for _ in chains]
        o_inter = [[None] * cpp for _ in chains]
        for j in range(cpp):
            chunk_of = [cpp - 1 - j if d == 1 else j for d in range(N_DIRS)]
            r = []
            for i, (d, h) in enumerate(chains):
                cg = pair_of[d] * cpp + chunk_of[d]
                r.append(jnp.dot(wq_refs[d][0, h, cg * PAIR:(cg + 1) * PAIR, :], s[i].astype(BF16),
                                 preferred_element_type=F32))
            for i, (d, h) in enumerate(chains):
                c = chunk_of[d]
                cg = pair_of[d] * cpp + c
                v_new[i][c] = u_refs[d][0, h, cg * CHUNK:(cg + 1) * CHUNK, :] - r[i][:CHUNK]
                o_inter[i][c] = r[i][CHUNK:]
                vz = jnp.concatenate([v_new[i][c] if m == c else zeros for m in range(cpp)], axis=0)
                kd_t = kd_refs[d][0, h, pair_of[d] * PAIR:(pair_of[d] + 1) * PAIR, :]
                s[i] = s[i] * gl_refs[d][0, h, cg] + jnp.dot(kd_t, vz.astype(BF16), preferred_element_type=F32)
        for i, (d, h) in enumerate(chains):
            s_ref[d * LA_HEADS + h] = s[i]
            rows = slice(pair_of[d] * PAIR, (pair_of[d] + 1) * PAIR)
            o_intra = jnp.dot(qk_refs[d][0, h, rows, :], jnp.concatenate(v_new[i], axis=0).astype(BF16),
                              preferred_element_type=F32)
            o_refs[d][h, rows, :] = jnp.concatenate(o_inter[i], axis=0) + o_intra

    @pl.when(t == pl.num_programs(1) - 1)
    def _():
        sfin_ref[0] = s_ref[...]


def _delta_scan(prep, s0, batch, seq):
    u, wq, kd, qk, gl = prep
    nt = seq // DELTA_ROWS
    r = DELTA_ROWS
    cpb = r // CHUNK
    n_chain = N_DIRS * LA_HEADS

    def tt(d, t):
        return nt - 1 - t if d == 1 else t

    def dspecs(rows):
        return [pl.BlockSpec((1, LA_HEADS, rows, LANES), functools.partial(
            lambda b, t, d: (d, 0, b * nt + tt(d, t), 0), d=d)) for d in range(N_DIRS)]

    gl_specs = [pl.BlockSpec((1, LA_HEADS, cpb, 1, LANES), functools.partial(
        lambda b, t, d: (d, 0, b * nt + tt(d, t), 0, 0), d=d)) for d in range(N_DIRS)]
    in_specs = dspecs(r) + dspecs(2 * r) + dspecs(r) + dspecs(r) + gl_specs
    args = [u, u, wq, wq, kd, kd, qk, qk, gl, gl]
    has_s0 = s0 is not None
    if has_s0:
        in_specs.append(pl.BlockSpec((1, n_chain, LA_DK, LA_DV), lambda b, t: (b, 0, 0, 0)))
        args.append(s0)
    o_shape = jax.ShapeDtypeStruct((LA_HEADS, batch * seq, LA_DV), F32)
    o_specs = [pl.BlockSpec((LA_HEADS, r, LANES), functools.partial(
        lambda b, t, d: (0, b * nt + tt(d, t), 0), d=d)) for d in range(N_DIRS)]
    return pl.pallas_call(
        functools.partial(_delta_scan_kernel, has_s0=has_s0),
        grid=(batch, nt),
        in_specs=in_specs,
        out_specs=o_specs + [pl.BlockSpec((1, n_chain, LA_DK, LA_DV), lambda b, t: (b, 0, 0, 0))],
        out_shape=[o_shape, o_shape, jax.ShapeDtypeStruct((batch, n_chain, LA_DK, LA_DV), F32)],
        scratch_shapes=[pltpu.VMEM((n_chain, LA_DK, LA_DV), F32)],
        compiler_params=_cparams("parallel", "arbitrary"),
        name="deltanet_scan",
    )(*args)


def _stream_specs(n_prompt, rows, cols):
    npb = n_prompt // rows
    return [
        pl.BlockSpec((rows, cols), lambda i: (jnp.minimum(i, npb - 1), 0)),
        pl.BlockSpec((rows, cols), lambda i: (jnp.maximum(i - npb, 0), 0)),
    ]


def _la_out_kernel(ofp_ref, obp_ref, ofs_ref, obs_ref, z_ref, x_ref, g1_ref, ng_ref, w_ref, lng_ref, lnb_ref,
                   y_ref, *, npb):
    is_prompt = pl.program_id(0) < npb
    ng = ng_ref[0:1, :]
    parts = []
    for h in range(LA_HEADS):
        oh = jnp.where(is_prompt, ofp_ref[h] + obp_ref[h], ofs_ref[h] + obs_ref[h])
        r = lax.rsqrt(jnp.mean(oh * oh, axis=-1, keepdims=True) + RMS_EPS)
        parts.append(oh * r * ng)
    on = jnp.concatenate(parts, axis=1) * _silu(z_ref[...])
    out = _bdot(on, w_ref[...])
    xr = ALPHA * x_ref[...] + g1_ref[0] * out
    y_ref[...] = _layer_norm_rows(xr, lng_ref[0:1, :], lnb_ref[0:1, :])


def _pad_rows(v):
    return jnp.zeros((SUBLANES, v.shape[0]), F32).at[0].set(v.astype(F32))


def _la_out(o_p, o_s, proj, x, mod, layer, n_prompt, dec_seq, norm_g, w_out, ln_g, ln_b):
    n, d = x.shape
    r = ROW_BLOCK
    zblk = LA_QKV // d
    npb = n_prompt // r
    sp = pl.BlockSpec((LA_HEADS, r, LA_DV), lambda i: (0, jnp.minimum(i, npb - 1), 0))
    ss = pl.BlockSpec((LA_HEADS, r, LA_DV), lambda i: (0, jnp.maximum(i - npb, 0), 0))
    return pl.pallas_call(
        functools.partial(_la_out_kernel, npb=n_prompt // r),
        grid=(n // r,),
        in_specs=[
            sp, sp, ss, ss,
            pl.BlockSpec((r, d), lambda i: (i, zblk)),
            pl.BlockSpec((r, d), lambda i: (i, 0)),
            _mod_spec(layer, 2, n_prompt, dec_seq, r),
            pl.BlockSpec((SUBLANES, LA_DV), lambda i: (0, 0)),
            pl.BlockSpec((d, d), lambda i: (0, 0)),
            pl.BlockSpec((SUBLANES, d), lambda i: (0, 0)),
            pl.BlockSpec((SUBLANES, d), lambda i: (0, 0)),
        ],
        out_specs=pl.BlockSpec((r, d), lambda i: (i, 0)),
        out_shape=jax.ShapeDtypeStruct((n, d), F32),
        compiler_params=_cparams("parallel"),
        name="deltanet_out",
    )(o_p[0], o_p[1], o_s[0], o_s[1], proj, x, mod, _pad_rows(norm_g), w_out.astype(BF16), _pad_rows(ln_g),
      _pad_rows(ln_b))


def _out_kernel(op_ref, os_ref, x_ref, g1_ref, w_ref, lng_ref, lnb_ref, y_ref, *, npb):
    is_prompt = pl.program_id(0) < npb
    o = jnp.where(is_prompt, op_ref[...], os_ref[...])
    out = _bdot(o, w_ref[...])
    xr = ALPHA * x_ref[...] + g1_ref[0] * out
    y_ref[...] = _layer_norm_rows(xr, lng_ref[0:1, :], lnb_ref[0:1, :])


def _swa_out(o_p, o_s, x, mod, layer, n_prompt, dec_seq, w_out, ln_g, ln_b):
    n, d = x.shape
    r = ROW_BLOCK
    sp, ss = _stream_specs(n_prompt, r, d)
    return pl.pallas_call(
        functools.partial(_out_kernel, npb=n_prompt // r),
        grid=(n // r,),
        in_specs=[
            sp, ss,
            pl.BlockSpec((r, d), lambda i: (i, 0)),
            _mod_spec(layer, 2, n_prompt, dec_seq, r),
            pl.BlockSpec((d, d), lambda i: (0, 0)),
            pl.BlockSpec((SUBLANES, d), lambda i: (0, 0)),
            pl.BlockSpec((SUBLANES, d), lambda i: (0, 0)),
        ],
        out_specs=pl.BlockSpec((r, d), lambda i: (i, 0)),
        out_shape=jax.ShapeDtypeStruct((n, d), F32),
        compiler_params=_cparams("parallel"),
        name="swa_out",
    )(o_p, o_s, x, mod, w_out.astype(BF16), _pad_rows(ln_g), _pad_rows(ln_b))


SWA_SCALE = SWA_DH ** -0.5


def _stack_group(q, kv):
    base = kv * SWA_GROUP * SWA_DH
    return jnp.concatenate([q[:, base + g * SWA_DH: base + (g + 1) * SWA_DH] for g in range(SWA_GROUP)], axis=0)


def _sink_col(sink_ref, kv, rows):
    return jnp.concatenate(
        [jnp.full((rows, 1), sink_ref[kv * SWA_GROUP + g], F32) for g in range(SWA_GROUP)], axis=0)


def _ctx_attn_kernel(sink_ref, q_ref, k_ref, v_ref, o_ref):
    q = q_ref[...]
    k = k_ref[...]
    v = v_ref[...]
    t = q.shape[0]
    outs = []
    for kv in range(SWA_KV_HEADS):
        kh = k[:, kv * SWA_DH:(kv + 1) * SWA_DH]
        vh = v[:, kv * SWA_DH:(kv + 1) * SWA_DH]
        s = _bdot_nt(_stack_group(q, kv), kh) * SWA_SCALE
        sink = _sink_col(sink_ref, kv, t)
        m = jnp.maximum(jnp.max(s, axis=1, keepdims=True), sink)
        p = jnp.exp(s - m)
        den = jnp.sum(p, axis=1, keepdims=True) + jnp.exp(sink - m)
        o = _bdot(p, vh) / den
        outs.extend(o[g * t:(g + 1) * t, :] for g in range(SWA_GROUP))
    o_ref[...] = jnp.concatenate(outs, axis=1)


def _ctx_attention(qkv, sink, batch, seq):
    kcol = SWA_HEADS * SWA_DH // SWA_KVW
    return pl.pallas_call(
        _ctx_attn_kernel,
        grid=(batch,),
        in_specs=[
            pl.BlockSpec(memory_space=pltpu.SMEM),
            pl.BlockSpec((seq, SWA_HEADS * SWA_DH), lambda b: (b, 0)),
            pl.BlockSpec((seq, SWA_KVW), lambda b: (b, kcol)),
            pl.BlockSpec((seq, SWA_KVW), lambda b: (b, kcol + 1)),
        ],
        out_specs=pl.BlockSpec((seq, SWA_HEADS * SWA_DH), lambda b: (b, 0)),
        out_shape=jax.ShapeDtypeStruct((batch * seq, SWA_HEADS * SWA_DH), F32),
        compiler_params=_cparams("parallel"),
        name="swa_context_attention",
    )(sink.astype(F32), qkv, qkv, qkv)


ROPE_COLS = SWA_HEADS * SWA_DH + SWA_KVW
ROPE_SHIFT = SWA_DH // 4


def _rope_tables(t_len):
    t = jnp.arange(t_len)
    r = (t // GRID_W).astype(F32)
    col = (t % GRID_W).astype(F32)
    inv = ROPE_BASE ** (-jnp.arange(ROT_FREQS, dtype=F32) / ROT_FREQS)
    ang_r = r[:, None] * inv
    ang_c = col[:, None] * inv
    ang = jnp.concatenate([ang_r, ang_r, ang_c, ang_c], -1)
    cos = jnp.cos(ang)
    sin = jnp.sin(ang)
    first = (jnp.arange(SWA_DH) % (2 * ROPE_SHIFT)) < ROPE_SHIFT
    sin_up = jnp.where(first, -sin, 0.0)
    sin_dn = jnp.where(first, 0.0, sin)
    reps = LANES // SWA_DH
    return tuple(jnp.tile(a, (1, reps)) for a in (cos, sin_up, sin_dn))


def _rope_kernel(x_ref, cos_ref, su_ref, sd_ref, o_ref):
    cos = cos_ref[...]
    su = su_ref[...]
    sd = sd_ref[...]
    for j in range(ROPE_COLS // LANES):
        x = x_ref[:, j * LANES:(j + 1) * LANES]
        up = pltpu.roll(x, LANES - ROPE_SHIFT, 1)
        dn = pltpu.roll(x, ROPE_SHIFT, 1)
        o_ref[:, j * LANES:(j + 1) * LANES] = x * cos + up * su + dn * sd


def _rope_qk(qkv, row0, batch, seq):
    r = ROW_BLOCK
    nb = seq // r
    blk0 = row0 // r
    tables = _rope_tables(seq)
    tspec = pl.BlockSpec((r, LANES), lambda b, i: (i, 0))
    return pl.pallas_call(
        _rope_kernel,
        grid=(batch, nb),
        in_specs=[pl.BlockSpec((r, ROPE_COLS), lambda b, i: (blk0 + b * nb + i, 0)), tspec, tspec, tspec],
        out_specs=pl.BlockSpec((r, ROPE_COLS), lambda b, i: (b * nb + i, 0)),
        out_shape=jax.ShapeDtypeStruct((batch * seq, ROPE_COLS), F32),
        compiler_params=_cparams("parallel", "parallel"),
        name="swa_rope",
    )(qkv, *tables)


def _lat_attn_kernel(sink_ref, q_ref, kp_ref, kc_ref, kn_ref, vp_ref, vc_ref, vn_ref, ck_ref, cv_ref, o_ref, *, seq):
    i = pl.program_id(1)
    q = q_ref[...]
    kw = jnp.concatenate([kp_ref[...], kc_ref[...], kn_ref[...]], axis=0)
    vw = jnp.concatenate([vp_ref[...], vc_ref[...], vn_ref[...]], axis=0)
    ck = ck_ref[...]
    cv = cv_ref[...]
    span = 3 * Q_BLOCK
    rows = SWA_GROUP * Q_BLOCK
    q0 = i * Q_BLOCK
    qpos = q0 + lax.broadcasted_iota(jnp.int32, (rows, span), 0) % Q_BLOCK
    kpos = q0 - WINDOW + lax.broadcasted_iota(jnp.int32, (rows, span), 1)
    valid = (kpos >= 0) & (kpos < seq) & (jnp.abs(qpos - kpos) <= WINDOW)
    outs = []
    for kv in range(SWA_KV_HEADS):
        cols = slice(kv * SWA_DH, (kv + 1) * SWA_DH)
        qg = _stack_group(q, kv)
        s_win = jnp.where(valid, _bdot_nt(qg, kw[:, cols]) * SWA_SCALE, NEG_INF)
        s_ctx = _bdot_nt(qg, ck[:, cols]) * SWA_SCALE
        sink = _sink_col(sink_ref, kv, Q_BLOCK)
        m = jnp.maximum(jnp.maximum(jnp.max(s_win, axis=1, keepdims=True), jnp.max(s_ctx, axis=1, keepdims=True)),
                        sink)
        p_win = jnp.exp(s_win - m)
        p_ctx = jnp.exp(s_ctx - m)
        den = (jnp.sum(p_win, axis=1, keepdims=True) + jnp.sum(p_ctx, axis=1, keepdims=True)
               + jnp.exp(sink - m))
        o = (_bdot(p_win, vw[:, cols]) + _bdot(p_ctx, cv[:, cols])) / den
        outs.extend(o[g * Q_BLOCK:(g + 1) * Q_BLOCK, :] for g in range(SWA_GROUP))
    o_ref[...] = jnp.concatenate(outs, axis=1)


def _lat_attention(qk_rope, qkv, cache_k, cache_v, sink, row0, batch, seq):
    nq = seq // Q_BLOCK
    blk0 = row0 // Q_BLOCK
    qw = SWA_HEADS * SWA_DH
    kcol = qw // SWA_KVW
    past = cache_k.shape[0] // batch

    def kspec(off):
        return pl.BlockSpec((Q_BLOCK, SWA_KVW), lambda b, i: (b * nq + jnp.clip(i + off, 0, nq - 1), kcol))

    def vspec(off):
        return pl.BlockSpec((Q_BLOCK, SWA_KVW),
                            lambda b, i: (blk0 + b * nq + jnp.clip(i + off, 0, nq - 1), kcol + 1))

    cspec = pl.BlockSpec((past, SWA_KVW), lambda b, i: (b, 0))
    return pl.pallas_call(
        functools.partial(_lat_attn_kernel, seq=seq),
        grid=(batch, nq),
        in_specs=[
            pl.BlockSpec(memory_space=pltpu.SMEM),
            pl.BlockSpec((Q_BLOCK, qw), lambda b, i: (b * nq + i, 0)),
            kspec(-1), kspec(0), kspec(1),
            vspec(-1), vspec(0), vspec(1),
            cspec, cspec,
        ],
        out_specs=pl.BlockSpec((Q_BLOCK, qw), lambda b, i: (b * nq + i, 0)),
        out_shape=jax.ShapeDtypeStruct((batch * seq, qw), F32),
        compiler_params=_cparams("parallel", "parallel"),
        name="swa_latent_attention",
    )(sink.astype(F32), qk_rope, qk_rope, qk_rope, qk_rope, qkv, qkv, qkv, cache_k, cache_v)


GROUP_SIZE = N_EXPERTS // N_GROUPS


def _first_argmax(v, iota, axis, size):
    m = jnp.max(v, axis=axis, keepdims=True)
    idx = jnp.min(jnp.where(v == m, iota, size), axis=axis, keepdims=True)
    return m, idx


def _router_kernel(x_ref, sc_ref, sh_ref, rt_ref, bias_ref, hs_ref, idx_ref, w_ref, pos_ref, wtm_ref, cnt_ref):
    i = pl.program_id(0)
    rows = x_ref.shape[0]

    @pl.when(i == 0)
    def _():
        cnt_ref[...] = jnp.zeros_like(cnt_ref)

    hs = x_ref[...] * (1.0 + sc_ref[0]) + sh_ref[0]
    hs_ref[...] = hs
    logits = lax.dot_general(rt_ref[...], hs, (((1,), (1,)), ((), ())), precision=HIGHEST,
                             preferred_element_type=F32)
    scores = jax.nn.sigmoid(logits)
    biased = scores + bias_ref[:, 0:1]
    b3 = biased.reshape(N_GROUPS, GROUP_SIZE, rows)
    mem = lax.broadcasted_iota(jnp.int32, b3.shape, 1)
    m1, i1 = _first_argmax(b3, mem, 1, GROUP_SIZE)
    m2 = jnp.max(jnp.where(mem == i1, -jnp.inf, b3), axis=1, keepdims=True)
    grp = (m1 + m2).reshape(N_GROUPS, rows)
    giota = lax.broadcasted_iota(jnp.int32, grp.shape, 0)
    gsel = jnp.zeros(grp.shape, jnp.bool_)
    for _ in range(TOPK_GROUPS):
        _, gi = _first_argmax(grp, giota, 0, N_GROUPS)
        hit = giota == gi
        gsel = gsel | hit
        grp = jnp.where(hit, -jnp.inf, grp)
    gmask = jnp.broadcast_to(gsel.reshape(N_GROUPS, 1, rows), b3.shape).reshape(N_EXPERTS, rows)
    sel = jnp.where(gmask, biased, -jnp.inf)
    eiota = lax.broadcasted_iota(jnp.int32, sel.shape, 0)
    chosen_any = jnp.zeros(sel.shape, jnp.bool_)
    idx_rows, w_rows, hits = [], [], []
    for _ in range(TOP_K):
        _, ei = _first_argmax(sel, eiota, 0, N_EXPERTS)
        hit = eiota == ei
        chosen_any = chosen_any | hit
        idx_rows.append(ei)
        w_rows.append(jnp.sum(jnp.where(hit, scores, 0.0), axis=0, keepdims=True))
        hits.append(hit)
        sel = jnp.where(hit, -jnp.inf, sel)
    wsum = w_rows[0]
    for w in w_rows[1:]:
        wsum = wsum + w
    w_rows = [w / wsum * ROUTED_SCALE for w in w_rows]
    onehot = chosen_any.astype(F32)
    s_i = lax.broadcasted_iota(jnp.int32, (rows, rows), 0)
    t_i = lax.broadcasted_iota(jnp.int32, (rows, rows), 1)
    earlier = (s_i < t_i).astype(F32)
    rank = cnt_ref[:, 0:1] + _bdot(onehot, earlier)
    cnt_ref[...] = cnt_ref[...] + jnp.sum(onehot, axis=1, keepdims=True)
    pos_rows = [jnp.sum(jnp.where(hit, rank, 0.0), axis=0, keepdims=True) for hit in hits]
    idx_ref[...] = jnp.concatenate(idx_rows, axis=0)
    w8 = jnp.concatenate(w_rows, axis=0)
    w_ref[...] = w8
    pos_ref[...] = jnp.concatenate(pos_rows, axis=0).astype(jnp.int32)
    wtm_ref[...] = jnp.concatenate([w8, jnp.zeros((LANES - TOP_K, rows), F32)], axis=0).T


def _router(x, mod, layer, n_prompt, dec_seq, router, bias):
    n, d = x.shape
    r = ROW_BLOCK
    bias_pad = jnp.zeros((N_EXPERTS, LANES), F32).at[:, 0].set(bias.astype(F32))
    kspec = pl.BlockSpec((TOP_K, r), lambda i: (0, i))
    return pl.pallas_call(
        _router_kernel,
        grid=(n // r,),
        in_specs=[
            pl.BlockSpec((r, d), lambda i: (i, 0)),
            _mod_spec(layer, 4, n_prompt, dec_seq, r),
            _mod_spec(layer, 3, n_prompt, dec_seq, r),
            pl.BlockSpec((N_EXPERTS, d), lambda i: (0, 0)),
            pl.BlockSpec((N_EXPERTS, LANES), lambda i: (0, 0)),
        ],
        out_specs=[
            pl.BlockSpec((r, d), lambda i: (i, 0)),
            kspec, kspec, kspec,
            pl.BlockSpec((r, LANES), lambda i: (i, 0)),
            pl.BlockSpec((N_EXPERTS, LANES), lambda i: (0, 0)),
        ],
        out_shape=[
            jax.ShapeDtypeStruct((n, d), F32),
            jax.ShapeDtypeStruct((TOP_K, n), jnp.int32),
            jax.ShapeDtypeStruct((TOP_K, n), F32),
            jax.ShapeDtypeStruct((TOP_K, n), jnp.int32),
            jax.ShapeDtypeStruct((n, LANES), F32),
            jax.ShapeDtypeStruct((N_EXPERTS, LANES), F32),
        ],
        compiler_params=_cparams("arbitrary"),
        name="moe_router",
    )(x, mod, mod, router.T.astype(F32), bias_pad)


DISPATCH_TOKENS = 256


def _row_copy(src_ref, src_row, dst_ref, dst_row, sem):
    return pltpu.make_async_copy(src_ref.at[pl.ds(src_row, 1), :], dst_ref.at[pl.ds(dst_row, 1), :], sem)


def _sorted_row(start_ref, idx_ref, pos_ref, k, t):
    return start_ref[idx_ref[k, t]] + pos_ref[k, t]


def _dispatch_kernel(start_ref, idx_ref, pos_ref, hs_ref, xs_ref, sem):
    tokens = hs_ref.shape[0]

    def copy(k, t):
        return _row_copy(hs_ref, t, xs_ref, _sorted_row(start_ref, idx_ref, pos_ref, k, t), sem)

    def start(t, carry):
        for k in range(TOP_K):
            copy(k, t).start(priority=k % 2)
        return carry

    lax.fori_loop(0, tokens, start, 0)

    def wait(t, carry):
        for k in range(TOP_K):
            copy(k, t).wait()
        return carry

    lax.fori_loop(0, tokens, wait, 0)


def _dispatch(hs, pad_start, idx, pos, n_rows):
    n, d = hs.shape
    t = DISPATCH_TOKENS
    kspec = pl.BlockSpec((TOP_K, t), lambda i: (0, i), memory_space=pltpu.SMEM)
    return pl.pallas_call(
        _dispatch_kernel,
        grid=(n // t,),
        in_specs=[
            pl.BlockSpec(memory_space=pltpu.SMEM),
            kspec, kspec,
            pl.BlockSpec((t, d), lambda i: (i, 0)),
        ],
        out_specs=pl.BlockSpec(memory_space=pl.ANY),
        out_shape=jax.ShapeDtypeStruct((n_rows, d), F32),
        scratch_shapes=[pltpu.SemaphoreType.DMA],
        compiler_params=_cparams("arbitrary"),
        name="moe_dispatch",
    )(pad_start, idx, pos, hs)


def _expert_kernel(be_ref, bv_ref, xs_ref, wg_ref, wu_ref, wd_ref, y_ref):
    del be_ref
    valid = bv_ref[pl.program_id(0)]

    @pl.when(valid > 0)
    def _():
        row = lax.broadcasted_iota(jnp.int32, xs_ref.shape, 0)
        x = jnp.where(row < valid, xs_ref[...], 0.0)
        h = _silu(_bdot(x, wg_ref[0])) * _bdot(x, wu_ref[0])
        y_ref[...] = _bdot(h, wd_ref[0])

    @pl.when(valid <= 0)
    def _():
        y_ref[...] = jnp.zeros_like(y_ref)


def _experts(xs, blk_expert, blk_valid, w_gate, w_up, w_down):
    n_rows, d = xs.shape
    nb = n_rows // MOE_ROWS
    de = w_gate.shape[2]
    grid_spec = pltpu.PrefetchScalarGridSpec(
        num_scalar_prefetch=2,
        grid=(nb,),
        in_specs=[
            pl.BlockSpec((MOE_ROWS, d), lambda i, be, bv: (i, 0)),
            pl.BlockSpec((1, d, de), lambda i, be, bv: (be[i], 0, 0)),
            pl.BlockSpec((1, d, de), lambda i, be, bv: (be[i], 0, 0)),
            pl.BlockSpec((1, de, d), lambda i, be, bv: (be[i], 0, 0)),
        ],
        out_specs=pl.BlockSpec((MOE_ROWS, d), lambda i, be, bv: (i, 0)),
    )
    return pl.pallas_call(
        _expert_kernel,
        grid_spec=grid_spec,
        out_shape=jax.ShapeDtypeStruct((n_rows, d), F32),
        compiler_params=_cparams("arbitrary"),
        name="moe_experts",
    )(blk_expert, blk_valid, xs, w_gate.astype(BF16), w_up.astype(BF16), w_down.astype(BF16))


COMBINE_TOKENS = 128


def _combine_kernel(start_ref, idx_ref, pos_ref, y_ref, wtm_ref, hs_ref, x_ref, g2_ref, sg_ref, su_ref, sd_ref,
                    lng_ref, lnb_ref, o_ref, ybuf, sem):
    tokens = hs_ref.shape[0]

    def copy(k, t):
        return _row_copy(y_ref, _sorted_row(start_ref, idx_ref, pos_ref, k, t), ybuf.at[k], t, sem)

    def start(t, carry):
        for k in range(TOP_K):
            copy(k, t).start(priority=k % 2)
        return carry

    lax.fori_loop(0, tokens, start, 0)
    hs = hs_ref[...]
    shared = _bdot(_silu(_bdot(hs, sg_ref[...])) * _bdot(hs, su_ref[...]), sd_ref[...])

    def wait(t, carry):
        for k in range(TOP_K):
            copy(k, t).wait()
        return carry

    lax.fori_loop(0, tokens, wait, 0)
    wtm = wtm_ref[...]
    routed = ybuf[0] * wtm[:, 0:1]
    for k in range(1, TOP_K):
        routed = routed + ybuf[k] * wtm[:, k:k + 1]
    xr = ALPHA * x_ref[...] + g2_ref[0] * (routed + shared)
    o_ref[...] = _layer_norm_rows(xr, lng_ref[0:1, :], lnb_ref[0:1, :])


def _combine(y, pad_start, idx, pos, wtm, hs, x, mod, layer, n_prompt, dec_seq, s_gate, s_up, s_down, ln_g, ln_b):
    n, d = x.shape
    t = COMBINE_TOKENS
    ds = s_gate.shape[1]
    row = pl.BlockSpec((t, d), lambda i: (i, 0))
    kspec = pl.BlockSpec((TOP_K, t), lambda i: (0, i), memory_space=pltpu.SMEM)
    return pl.pallas_call(
        _combine_kernel,
        grid=(n // t,),
        in_specs=[
            pl.BlockSpec(memory_space=pltpu.SMEM),
            kspec, kspec,
            pl.BlockSpec(memory_space=pl.ANY),
            pl.BlockSpec((t, LANES), lambda i: (i, 0)),
            row, row,
            _mod_spec(layer, 5, n_prompt, dec_seq, t),
            pl.BlockSpec((d, ds), lambda i: (0, 0)),
            pl.BlockSpec((d, ds), lambda i: (0, 0)),
            pl.BlockSpec((ds, d), lambda i: (0, 0)),
            pl.BlockSpec((SUBLANES, d), lambda i: (0, 0)),
            pl.BlockSpec((SUBLANES, d), lambda i: (0, 0)),
        ],
        out_specs=row,
        out_shape=jax.ShapeDtypeStruct((n, d), F32),
        scratch_shapes=[pltpu.VMEM((TOP_K, t, d), F32), pltpu.SemaphoreType.DMA],
        compiler_params=_cparams("arbitrary"),
        name="moe_combine",
    )(pad_start, idx, pos, y, wtm, hs, x, mod, s_gate.astype(BF16), s_up.astype(BF16), s_down.astype(BF16),
      _pad_rows(ln_g), _pad_rows(ln_b))


def _moe(x, mod, layer, n_prompt, dec_seq, router, bias, w_gate, w_up, w_down, s_gate, s_up, s_down, ln_g, ln_b):
    n, d = x.shape
    hs, idx, _, pos, wtm, cnt = _router(x, mod, layer, n_prompt, dec_seq, router, bias)
    counts = cnt[:, 0].astype(jnp.int32)
    padded = (counts + MOE_ROWS - 1) // MOE_ROWS * MOE_ROWS
    pad_end = jnp.cumsum(padded)
    pad_start = (pad_end - padded).astype(jnp.int32)
    nb = n * TOP_K // MOE_ROWS + N_EXPERTS
    blk_row = jnp.arange(nb, dtype=jnp.int32) * MOE_ROWS
    blk_expert = jnp.minimum(jnp.sum(pad_end[None, :] <= blk_row[:, None], axis=1), N_EXPERTS - 1).astype(jnp.int32)
    blk_valid = jnp.clip(pad_start[blk_expert] + counts[blk_expert] - blk_row, 0, MOE_ROWS).astype(jnp.int32)
    xs = _dispatch(hs, pad_start, idx, pos, nb * MOE_ROWS)
    y = _experts(xs, blk_expert, blk_valid, w_gate, w_up, w_down)
    return _combine(y, pad_start, idx, pos, wtm, hs, x, mod, layer, n_prompt, dec_seq, s_gate, s_up, s_down,
                    ln_g, ln_b)


def kernel(x_prompt, x_sample, state_la, cache_k, cache_v, c, c_ctx, w_mod, b_mod, ln_g, ln_b,
           la_w_in, la_conv, la_a_log, la_dt_bias, la_norm_g, la_w_out,
           swa_w_qkv, swa_sink, swa_w_out,
           moe_router, moe_bias, moe_w_gate, moe_w_up, moe_w_down, sh_w_gate, sh_w_up, sh_w_down):
    bp, tp, d = x_prompt.shape
    bs, ts, _ = x_sample.shape
    n_prompt = bp * tp
    assert bs + 1 <= COND_ROWS and d == D_MODEL
    assert n_prompt % ROW_BLOCK == 0 and ts % ROW_BLOCK == 0 and tp % DELTA_ROWS == 0

    cond = jnp.zeros((COND_ROWS, d), F32).at[0].set(c_ctx).at[1:1 + bs].set(c)
    mod = _modulation(cond, w_mod, b_mod)
    x = jnp.concatenate([x_prompt.reshape(n_prompt, d), x_sample.reshape(bs * ts, d)], axis=0)

    def moe(x, i):
        return _moe(x, mod, i, n_prompt, ts, moe_router[i], moe_bias[i], moe_w_gate[i], moe_w_up[i], moe_w_down[i],
                    sh_w_gate[i], sh_w_up[i], sh_w_down[i], ln_g[i, 1], ln_b[i, 1])

    proj, gbc, gbr = _la_in_proj(x, mod, 0, n_prompt, ts, la_w_in[0], la_a_log[0], la_dt_bias[0])
    qkv_p = _conv_qkv(proj, la_conv[0], 0, bp, tp)
    qkv_s = _conv_qkv(proj, la_conv[0], n_prompt, bs, ts)
    opf, opb, s_fin = _delta_scan(_delta_prep(qkv_p, gbc, gbr, 0), None, bp, tp)
    s0 = state_la[:, 0].reshape(bs, N_DIRS * LA_HEADS, LA_DK, LA_DV)
    osf, osb, _ = _delta_scan(_delta_prep(qkv_s, gbc, gbr, n_prompt), s0, bs, ts)
    x = _la_out((opf, opb), (osf, osb), proj, x, mod, 0, n_prompt, ts, la_norm_g[0], la_w_out[0], ln_g[0, 0],
                ln_b[0, 0])
    new_la = s_fin.reshape(bp, 1, N_DIRS, LA_HEADS, LA_DK, LA_DV)
    x = moe(x, 0)

    qkv = _mod_proj(x, mod, 1, n_prompt, ts, swa_w_qkv[0], "swa_qkv_proj")
    qw = SWA_HEADS * SWA_DH
    new_k = qkv[:n_prompt, qw:qw + SWA_KVW].reshape(bp, 1, tp, SWA_KV_HEADS, SWA_DH)
    new_v = qkv[:n_prompt, qw + SWA_KVW:].reshape(bp, 1, tp, SWA_KV_HEADS, SWA_DH)
    a_p = _ctx_attention(qkv, swa_sink[0], bp, tp)
    qk_rope = _rope_qk(qkv, n_prompt, bs, ts)
    past = cache_k.shape[2]
    a_s = _lat_attention(qk_rope, qkv, cache_k[:, 0].reshape(bs * past, SWA_KVW),
                         cache_v[:, 0].reshape(bs * past, SWA_KVW), swa_sink[0], n_prompt, bs, ts)
    x = _swa_out(a_p, a_s, x, mod, 1, n_prompt, ts, swa_w_out[0], ln_g[1, 0], ln_b[1, 0])
    x = moe(x, 1)

    y_prompt = x[:n_prompt].reshape(bp, tp, d)
    y_sample = x[n_prompt:].reshape(bs, ts, d)
    return y_prompt, y_sample, new_la, new_k, new_v
```

```python
import functools
import math

import jax
import jax.numpy as jnp
from jax import lax
from jax.experimental import pallas as pl
from jax.experimental.pallas import tpu as pltpu

F32 = jnp.float32
BF16 = jnp.bfloat16
HIGHEST = lax.Precision.HIGHEST

LANES = 128
SUBLANES = 8
VMEM_LIMIT_BYTES = 56 * 1024 * 1024

D_MODEL = 1024
DEPTH = 2
GRID_W = 64
LA_DK = 128
LA_DV = 128
LA_HEADS = D_MODEL // LA_DK
LA_QKV = LA_HEADS * (2 * LA_DK + LA_DV)
CONV_K = 5
CHUNK = 64
SUB = 16
SWA_DH = 64
SWA_HEADS = D_MODEL // SWA_DH
SWA_KV_HEADS = SWA_HEADS // 4
SWA_GROUP = SWA_HEADS // SWA_KV_HEADS
SWA_KVW = SWA_KV_HEADS * SWA_DH
WINDOW = 128
Q_BLOCK = 128
ROT_FREQS = SWA_DH // 4
ROPE_BASE = 10000.0
NEG_INF = -1e30
N_EXPERTS = 64
TOP_K = 8
N_GROUPS = 8
TOPK_GROUPS = 4
D_EXPERT = 256
D_SHARED = 256
ROUTED_SCALE = 2.5
ALPHA = (2 * DEPTH) ** 0.25
LN_EPS = 1e-5
RMS_EPS = 1e-6

ROW_BLOCK = 256
MOE_ROWS = 256
COND_ROWS = 8


def _cparams(*sem):
    return pltpu.CompilerParams(dimension_semantics=sem, vmem_limit_bytes=VMEM_LIMIT_BYTES)


def _bdot(a, b):
    return jnp.dot(a.astype(BF16), b.astype(BF16), preferred_element_type=F32)


def _bdot_nt(a, b):
    return lax.dot_general(a.astype(BF16), b.astype(BF16), (((1,), (1,)), ((), ())), preferred_element_type=F32)


def _bdot_tn(a, b):
    return lax.dot_general(a.astype(BF16), b.astype(BF16), (((0,), (0,)), ((), ())), preferred_element_type=F32)


def _silu(x):
    return x * jax.nn.sigmoid(x)


def _layer_norm_rows(x, g, b):
    mu = jnp.mean(x, axis=-1, keepdims=True)
    xc = x - mu
    var = jnp.mean(xc * xc, axis=-1, keepdims=True)
    return xc * lax.rsqrt(var + LN_EPS) * g + b


MOD_COLS = 512


def _mod_kernel(c_ref, w_ref, b_ref, o_ref):
    s = _silu(c_ref[...])
    o_ref[0] = jnp.dot(s, w_ref[0], precision=HIGHEST, preferred_element_type=F32) + b_ref[0]


def _modulation(cond, w_mod, b_mod):
    depth, d, n6 = w_mod.shape
    out = pl.pallas_call(
        _mod_kernel,
        grid=(depth, n6 // MOD_COLS),
        in_specs=[
            pl.BlockSpec((COND_ROWS, d), lambda l, j: (0, 0)),
            pl.BlockSpec((1, d, MOD_COLS), lambda l, j: (l, 0, j)),
            pl.BlockSpec((1, 1, MOD_COLS), lambda l, j: (l, 0, j)),
        ],
        out_specs=pl.BlockSpec((1, COND_ROWS, MOD_COLS), lambda l, j: (l, 0, j)),
        out_shape=jax.ShapeDtypeStruct((depth, COND_ROWS, n6), F32),
        compiler_params=_cparams("parallel", "parallel"),
        name="adaln_modulation",
    )(cond, w_mod, b_mod.reshape(depth, 1, n6))
    return out.reshape(depth * COND_ROWS * 6, 1, d)


def _mod_spec(layer, chunk, n_prompt, dec_seq, rows):
    def index(i, *_):
        tok = i * rows
        row = jnp.where(tok < n_prompt, 0, 1 + (tok - n_prompt) // dec_seq)
        return ((layer * COND_ROWS + row) * 6 + chunk, 0, 0)

    return pl.BlockSpec((1, 1, D_MODEL), index)


def _la_in_kernel(x_ref, sc_ref, sh_ref, w_ref, wab_ref, wabt_ref, pc_ref, pr_ref, o_ref, gbc_ref, gbr_ref):
    xm = x_ref[...] * (1.0 + sc_ref[0]) + sh_ref[0]
    o_ref[...] = _bdot(xm, w_ref[...])
    ab = jnp.dot(xm, wab_ref[...], precision=HIGHEST, preferred_element_type=F32)
    lane = lax.broadcasted_iota(jnp.int32, ab.shape, 1)
    neg_a = pc_ref[0:1, :]
    dt_b = pc_ref[1:2, :]
    z = ab + dt_b
    softplus = jnp.maximum(z, 0.0) + jnp.log1p(jnp.exp(-jnp.abs(z)))
    gbc_ref[...] = jnp.where(lane < 2 * LA_HEADS, neg_a * softplus, jax.nn.sigmoid(ab))
    rows = x_ref.shape[0]
    for c in range(rows // CHUNK):
        xc = xm[c * CHUNK:(c + 1) * CHUNK, :]
        abt = lax.dot_general(wabt_ref[...], xc, (((1,), (1,)), ((), ())), precision=HIGHEST,
                              preferred_element_type=F32)
        zt = abt + pr_ref[:, 1:2]
        spt = jnp.maximum(zt, 0.0) + jnp.log1p(jnp.exp(-jnp.abs(zt)))
        gbr_ref[c] = pr_ref[:, 0:1] * spt


def _la_in_proj(x, mod, layer, n_prompt, dec_seq, w_in, a_log, dt_bias):
    n, d = x.shape
    wide = LA_QKV + LA_HEADS * LA_DV
    w_main = w_in[:, :wide].astype(BF16)
    w_ab = w_in[:, wide:]
    n_ab = w_ab.shape[1]
    w_ab_pad = jnp.zeros((d, LANES), F32).at[:, :n_ab].set(w_ab)
    neg_a = -jnp.exp(a_log.astype(F32)).reshape(-1)
    dtb = dt_bias.astype(F32).reshape(-1)
    pc = jnp.zeros((SUBLANES, LANES), F32).at[0, :2 * LA_HEADS].set(neg_a).at[1, :2 * LA_HEADS].set(dtb)
    pr = jnp.zeros((n_ab, LANES), F32).at[:2 * LA_HEADS, 0].set(neg_a).at[:2 * LA_HEADS, 1].set(dtb)
    r = ROW_BLOCK
    return pl.pallas_call(
        _la_in_kernel,
        grid=(n // r,),
        in_specs=[
            pl.BlockSpec((r, d), lambda i: (i, 0)),
            _mod_spec(layer, 1, n_prompt, dec_seq, r),
            _mod_spec(layer, 0, n_prompt, dec_seq, r),
            pl.BlockSpec((d, wide), lambda i: (0, 0)),
            pl.BlockSpec((d, LANES), lambda i: (0, 0)),
            pl.BlockSpec((n_ab, d), lambda i: (0, 0)),
            pl.BlockSpec((SUBLANES, LANES), lambda i: (0, 0)),
            pl.BlockSpec((n_ab, LANES), lambda i: (0, 0)),
        ],
        out_specs=[
            pl.BlockSpec((r, wide), lambda i: (i, 0)),
            pl.BlockSpec((r, LANES), lambda i: (i, 0)),
            pl.BlockSpec((r // CHUNK, n_ab, CHUNK), lambda i: (i, 0, 0)),
        ],
        out_shape=[
            jax.ShapeDtypeStruct((n, wide), F32),
            jax.ShapeDtypeStruct((n, LANES), F32),
            jax.ShapeDtypeStruct((n // CHUNK, n_ab, CHUNK), F32),
        ],
        compiler_params=_cparams("parallel"),
        name="deltanet_in_proj",
    )(x, mod, mod, w_main, w_ab_pad, w_ab.T, pc, pr)


def _proj_kernel(x_ref, sc_ref, sh_ref, w_ref, o_ref):
    xm = x_ref[...] * (1.0 + sc_ref[0]) + sh_ref[0]
    o_ref[...] = _bdot(xm, w_ref[...])


def _mod_proj(x, mod, layer, n_prompt, dec_seq, w, name):
    n, d = x.shape
    cols = w.shape[1]
    r = ROW_BLOCK
    return pl.pallas_call(
        _proj_kernel,
        grid=(n // r,),
        in_specs=[
            pl.BlockSpec((r, d), lambda i: (i, 0)),
            _mod_spec(layer, 1, n_prompt, dec_seq, r),
            _mod_spec(layer, 0, n_prompt, dec_seq, r),
            pl.BlockSpec((d, cols), lambda i: (0, 0)),
        ],
        out_specs=pl.BlockSpec((r, cols), lambda i: (i, 0)),
        out_shape=jax.ShapeDtypeStruct((n, cols), F32),
        compiler_params=_cparams("parallel"),
        name=name,
    )(x, mod, mod, w.astype(BF16))


CONV_PAD = SUBLANES
CONV_ROWS = 256


def _conv_kernel(x_ref, w_ref, o_ref, pad_ref, *, seq):
    c = pl.program_id(1)
    zeros = jnp.zeros((CONV_PAD, LANES), F32)
    pad_ref[0:CONV_PAD, :] = zeros
    pad_ref[CONV_PAD + seq:, :] = zeros
    pad_ref[CONV_PAD:CONV_PAD + seq, :] = x_ref[...]
    is_qk = c < 2 * LA_HEADS
    scale = jnp.where(c < LA_HEADS, LA_DK ** -0.5, 1.0).astype(F32)
    half = CONV_K // 2
    for r0 in range(0, seq, CONV_ROWS):
        acc = jnp.zeros((CONV_ROWS, LANES), F32)
        for j in range(CONV_K):
            start = r0 + CONV_PAD + j - half
            acc = acc + w_ref[j:j + 1, :] * pad_ref[start:start + CONV_ROWS, :]
        y = _silu(acc)
        nrm = lax.rsqrt(jnp.sum(y * y, axis=-1, keepdims=True) + 1e-6) * scale
        o_ref[r0:r0 + CONV_ROWS, :] = y * jnp.where(is_qk, nrm, 1.0)


def _conv_qkv(proj, conv_w, row0, batch, seq):
    w = jnp.zeros((SUBLANES, LA_QKV), F32).at[:CONV_K].set(conv_w.astype(F32))
    blk0 = row0 // seq
    return pl.pallas_call(
        functools.partial(_conv_kernel, seq=seq),
        grid=(batch, LA_QKV // LANES),
        in_specs=[
            pl.BlockSpec((seq, LANES), lambda b, c: (blk0 + b, c)),
            pl.BlockSpec((SUBLANES, LANES), lambda b, c: (0, c)),
        ],
        out_specs=pl.BlockSpec((seq, LANES), lambda b, c: (b, c)),
        out_shape=jax.ShapeDtypeStruct((batch * seq, LA_QKV), F32),
        scratch_shapes=[pltpu.VMEM((seq + 2 * CONV_PAD, LANES), F32)],
        compiler_params=_cparams("parallel", "parallel"),
        name="deltanet_conv",
    )(proj, w)


DELTA_ROWS = 256
PAIR = 2 * CHUNK
N_DIRS = 2


PREP_HEADS = 2


def _pair_masks(rev):
    ii = lax.broadcasted_iota(jnp.int32, (PAIR, PAIR), 0)
    jj = lax.broadcasted_iota(jnp.int32, (PAIR, PAIR), 1)
    same_chunk = (ii // CHUNK) == (jj // CHUNK)
    same_sub = (ii // SUB) == (jj // SUB)
    if rev:
        return same_chunk, same_sub, same_chunk & (jj >= ii), same_chunk & (jj > ii), same_chunk & (ii >= jj)
    return same_chunk, same_sub, same_chunk & (jj <= ii), same_chunk & (jj < ii), same_chunk & (ii <= jj)


def _delta_prep_kernel(q_ref, k_ref, v_ref, gbc_ref, gbr_ref, u_ref, wq_ref, kd_ref, qk_ref, gl_ref):
    hp = pl.program_id(1)
    lane = lax.broadcasted_iota(jnp.int32, (PAIR, LANES), 1)
    cpp = PAIR // CHUNK
    masks = [_pair_masks(rev) for rev in (False, True)]
    probs = []
    for hh in range(PREP_HEADS):
        cols = slice(hh * LA_DK, (hh + 1) * LA_DK)
        for pr in range(DELTA_ROWS // PAIR):
            rows = slice(pr * PAIR, (pr + 1) * PAIR)
            q = q_ref[rows, cols]
            k = k_ref[rows, cols]
            v = v_ref[rows, cols]
            gbc = gbc_ref[rows, :]
            gram = _bdot_nt(k, k)
            qk_raw = _bdot_nt(q, k)
            for d in range(N_DIRS):
                same_chunk, same_sub, before, strict, before_t = masks[d]
                h = hp * PREP_HEADS + hh
                g_lane = d * LA_HEADS + h
                b_lane = (N_DIRS + d) * LA_HEADS + h
                g_col = jnp.sum(jnp.where(lane == g_lane, gbc, 0.0), axis=1, keepdims=True)
                b_col = jnp.sum(jnp.where(lane == b_lane, gbc, 0.0), axis=1, keepdims=True)
                g_row = jnp.concatenate([gbr_ref[pr * cpp + c, pl.ds(g_lane, 1), :] for c in range(cpp)], axis=1)
                gam_col = jnp.sum(jnp.where(before, g_row, 0.0), axis=1, keepdims=True)
                gam_row = jnp.sum(jnp.where(before_t, g_col, 0.0), axis=0, keepdims=True)
                tot_col = jnp.sum(jnp.where(same_chunk, g_row, 0.0), axis=1, keepdims=True)
                tot_row = jnp.sum(jnp.where(same_chunk, g_col, 0.0), axis=0, keepdims=True)
                dec = jnp.where(before, jnp.exp(jnp.where(before, gam_col - gam_row, 0.0)), 0.0)
                e_col = jnp.exp(gam_col)
                nm = jnp.where(strict, -(gram * b_col * dec), 0.0)
                nd = jnp.where(same_sub, nm, 0.0)
                probs.append(dict(
                    hh=hh, pr=pr, d=d, rows=rows, nd=nd, ne=nm - nd,
                    rhs=jnp.concatenate([v * b_col, k * (b_col * e_col)], axis=1),
                    qd=q * e_col, kd=k * jnp.exp(tot_col - gam_col), qkm=qk_raw * dec, gl=jnp.exp(tot_row)))
    for p in probs:
        p["yd"] = p["nd"]
        p["p"] = p["nd"]
    for _ in range(3):
        for p in probs:
            p["p"] = _bdot(p["p"], p["p"])
        for p in probs:
            p["yd"] = p["yd"] + p["p"] + _bdot(p["yd"], p["p"])
    for p in probs:
        p["f"] = p["ne"] + _bdot(p["yd"], p["ne"])
    for p in probs:
        p["f2"] = _bdot(p["f"], p["f"])
    for p in probs:
        p["gm"] = p["f"] + p["f2"] + _bdot(p["f"], p["f2"])
    for p in probs:
        p["yt"] = p["gm"] + p["yd"] + _bdot(p["gm"], p["yd"])
    for p in probs:
        p["sol"] = p["rhs"] + _bdot(p["yt"], p["rhs"])
    for p in probs:
        d, hh, pr, rows = p["d"], p["hh"], p["pr"], p["rows"]
        u = p["sol"][:, :LA_DV]
        w = p["sol"][:, LA_DV:]
        u_ref[d, hh, rows, :] = u
        kd_ref[d, hh, rows, :] = p["kd"].T.astype(BF16)
        qk_ref[d, hh, rows, :] = p["qkm"].astype(BF16)
        for c in range(cpp):
            cg = pr * cpp + c
            cr = slice(c * CHUNK, (c + 1) * CHUNK)
            wq_ref[d, hh, cg * PAIR:cg * PAIR + CHUNK, :] = w[cr, :].astype(BF16)
            wq_ref[d, hh, cg * PAIR + CHUNK:(cg + 1) * PAIR, :] = p["qd"][cr, :].astype(BF16)
            gl_ref[d, hh, cg] = p["gl"][:, c * CHUNK:c * CHUNK + 1] + jnp.zeros((1, LANES), F32)


def _delta_prep(qkv, gbc, gbr, row0):
    n = qkv.shape[0]
    r = DELTA_ROWS
    g0 = row0 // r
    n_ab = gbr.shape[1]
    cpb = r // CHUNK
    hg = LA_HEADS // PREP_HEADS
    hw = PREP_HEADS * LA_DK
    hm = lambda rows, dt: jax.ShapeDtypeStruct((N_DIRS, LA_HEADS, rows, LANES), dt)
    hspec = lambda rows: pl.BlockSpec((N_DIRS, PREP_HEADS, rows, LANES), lambda i, h: (0, h, i, 0))
    return pl.pallas_call(
        _delta_prep_kernel,
        grid=(n // r, hg),
        in_specs=[
            pl.BlockSpec((r, hw), lambda i, h: (i, h)),
            pl.BlockSpec((r, hw), lambda i, h: (i, hg + h)),
            pl.BlockSpec((r, hw), lambda i, h: (i, 2 * hg + h)),
            pl.BlockSpec((r, LANES), lambda i, h: (g0 + i, 0)),
            pl.BlockSpec((cpb, n_ab, CHUNK), lambda i, h: (g0 + i, 0, 0)),
        ],
        out_specs=[
            hspec(r), hspec(2 * r), hspec(r), hspec(r),
            pl.BlockSpec((N_DIRS, PREP_HEADS, cpb, 1, LANES), lambda i, h: (0, h, i, 0, 0)),
        ],
        out_shape=[
            hm(n, F32), hm(2 * n, BF16), hm(n, BF16), hm(n, BF16),
            jax.ShapeDtypeStruct((N_DIRS, LA_HEADS, n // CHUNK, 1, LANES), F32),
        ],
        compiler_params=_cparams("parallel", "parallel"),
        name="deltanet_prep",
    )(qkv, qkv, qkv, gbc, gbr)


def _delta_scan_kernel(*refs, has_s0):
    u_refs, wq_refs, kd_refs, qk_refs, gl_refs = (refs[0:2], refs[2:4], refs[4:6], refs[6:8], refs[8:10])
    if has_s0:
        s0_ref, of_ref, ob_ref, sfin_ref, s_ref = refs[10:]
    else:
        of_ref, ob_ref, sfin_ref, s_ref = refs[10:]
    o_refs = (of_ref, ob_ref)
    t = pl.program_id(1)

    @pl.when(t == 0)
    def _():
        if has_s0:
            s_ref[...] = s0_ref[0]
        else:
            s_ref[...] = jnp.zeros_like(s_ref)

    n_pairs = DELTA_ROWS // PAIR
    cpp = PAIR // CHUNK
    chains = [(d, h) for d in range(N_DIRS) for h in range(LA_HEADS)]
    zeros = jnp.zeros((CHUNK, LA_DV), F32)
    for step in range(n_pairs):
        pair_of = [n_pairs - 1 - step if d == 1 else step for d in range(N_DIRS)]
        s = [s_ref[d * LA_HEADS + h] for d, h in chains]
        v_new = [[None] * cpp for _ in chains]
        o_inter = [[None] * cpp for _ in chains]
        for j in range(cpp):
            chunk_of = [cpp - 1 - j if d == 1 else j for d in range(N_DIRS)]
            r = []
            for i, (d, h) in enumerate(chains):
                cg = pair_of[d] * cpp + chunk_of[d]
                r.append(jnp.dot(wq_refs[d][0, h, cg * PAIR:(cg + 1) * PAIR, :], s[i].astype(BF16),
                                 preferred_element_type=F32))
            for i, (d, h) in enumerate(chains):
                c = chunk_of[d]
                cg = pair_of[d] * cpp + c
                v_new[i][c] = u_refs[d][0, h, cg * CHUNK:(cg + 1) * CHUNK, :] - r[i][:CHUNK]
                o_inter[i][c] = r[i][CHUNK:]
                vz = jnp.concatenate([v_new[i][c] if m == c else zeros for m in range(cpp)], axis=0)
                kd_t = kd_refs[d][0, h, pair_of[d] * PAIR:(pair_of[d] + 1) * PAIR, :]
                s[i] = s[i] * gl_refs[d][0, h, cg] + jnp.dot(kd_t, vz.astype(BF16), preferred_element_type=F32)
        for i, (d, h) in enumerate(chains):
            s_ref[d * LA_HEADS + h] = s[i]
            rows = slice(pair_of[d] * PAIR, (pair_of[d] + 1) * PAIR)
            o_intra = jnp.dot(qk_refs[d][0, h, rows, :], jnp.concatenate(v_new[i], axis=0).astype(BF16),
                              preferred_element_type=F32)
            o_refs[d][h, rows, :] = jnp.concatenate(o_inter[i], axis=0) + o_intra

    @pl.when(t == pl.num_programs(1) - 1)
    def _():
        sfin_ref[0] = s_ref[...]


def _delta_scan(prep, s0, batch, seq):
    u, wq, kd, qk, gl = prep
    nt = seq // DELTA_ROWS
    r = DELTA_ROWS
    cpb = r // CHUNK
    n_chain = N_DIRS * LA_HEADS

    def tt(d, t):
        return nt - 1 - t if d == 1 else t

    def dspecs(rows):
        return [pl.BlockSpec((1, LA_HEADS, rows, LANES), functools.partial(
            lambda b, t, d: (d, 0, b * nt + tt(d, t), 0), d=d)) for d in range(N_DIRS)]

    gl_specs = [pl.BlockSpec((1, LA_HEADS, cpb, 1, LANES), functools.partial(
        lambda b, t, d: (d, 0, b * nt + tt(d, t), 0, 0), d=d)) for d in range(N_DIRS)]
    in_specs = dspecs(r) + dspecs(2 * r) + dspecs(r) + dspecs(r) + gl_specs
    args = [u, u, wq, wq, kd, kd, qk, qk, gl, gl]
    has_s0 = s0 is not None
    if has_s0:
        in_specs.append(pl.BlockSpec((1, n_chain, LA_DK, LA_DV), lambda b, t: (b, 0, 0, 0)))
        args.append(s0)
    o_shape = jax.ShapeDtypeStruct((LA_HEADS, batch * seq, LA_DV), F32)
    o_specs = [pl.BlockSpec((LA_HEADS, r, LANES), functools.partial(
        lambda b, t, d: (0, b * nt + tt(d, t), 0), d=d)) for d in range(N_DIRS)]
    return pl.pallas_call(
        functools.partial(_delta_scan_kernel, has_s0=has_s0),
        grid=(batch, nt),
        in_specs=in_specs,
        out_specs=o_specs + [pl.BlockSpec((1, n_chain, LA_DK, LA_DV), lambda b, t: (b, 0, 0, 0))],
        out_shape=[o_shape, o_shape, jax.ShapeDtypeStruct((batch, n_chain, LA_DK, LA_DV), F32)],
        scratch_shapes=[pltpu.VMEM((n_chain, LA_DK, LA_DV), F32)],
        compiler_params=_cparams("parallel", "arbitrary"),
        name="deltanet_scan",
    )(*args)


def _stream_specs(n_prompt, rows, cols):
    npb = n_prompt // rows
    return [
        pl.BlockSpec((rows, cols), lambda i: (jnp.minimum(i, npb - 1), 0)),
        pl.BlockSpec((rows, cols), lambda i: (jnp.maximum(i - npb, 0), 0)),
    ]


def _la_out_kernel(ofp_ref, obp_ref, ofs_ref, obs_ref, z_ref, x_ref, g1_ref, ng_ref, w_ref, lng_ref, lnb_ref,
                   y_ref, *, npb):
    is_prompt = pl.program_id(0) < npb
    ng = ng_ref[0:1, :]
    parts = []
    for h in range(LA_HEADS):
        oh = jnp.where(is_prompt, ofp_ref[h] + obp_ref[h], ofs_ref[h] + obs_ref[h])
        r = lax.rsqrt(jnp.mean(oh * oh, axis=-1, keepdims=True) + RMS_EPS)
        parts.append(oh * r * ng)
    on = jnp.concatenate(parts, axis=1) * _silu(z_ref[...])
    out = _bdot(on, w_ref[...])
    xr = ALPHA * x_ref[...] + g1_ref[0] * out
    y_ref[...] = _layer_norm_rows(xr, lng_ref[0:1, :], lnb_ref[0:1, :])


def _pad_rows(v):
    return jnp.zeros((SUBLANES, v.shape[0]), F32).at[0].set(v.astype(F32))


def _la_out(o_p, o_s, proj, x, mod, layer, n_prompt, dec_seq, norm_g, w_out, ln_g, ln_b):
    n, d = x.shape
    r = ROW_BLOCK
    zblk = LA_QKV // d
    npb = n_prompt // r
    sp = pl.BlockSpec((LA_HEADS, r, LA_DV), lambda i: (0, jnp.minimum(i, npb - 1), 0))
    ss = pl.BlockSpec((LA_HEADS, r, LA_DV), lambda i: (0, jnp.maximum(i - npb, 0), 0))
    return pl.pallas_call(
        functools.partial(_la_out_kernel, npb=n_prompt // r),
        grid=(n // r,),
        in_specs=[
            sp, sp, ss, ss,
            pl.BlockSpec((r, d), lambda i: (i, zblk)),
            pl.BlockSpec((r, d), lambda i: (i, 0)),
            _mod_spec(layer, 2, n_prompt, dec_seq, r),
            pl.BlockSpec((SUBLANES, LA_DV), lambda i: (0, 0)),
            pl.BlockSpec((d, d), lambda i: (0, 0)),
            pl.BlockSpec((SUBLANES, d), lambda i: (0, 0)),
            pl.BlockSpec((SUBLANES, d), lambda i: (0, 0)),
        ],
        out_specs=pl.BlockSpec((r, d), lambda i: (i, 0)),
        out_shape=jax.ShapeDtypeStruct((n, d), F32),
        compiler_params=_cparams("parallel"),
        name="deltanet_out",
    )(o_p[0], o_p[1], o_s[0], o_s[1], proj, x, mod, _pad_rows(norm_g), w_out.astype(BF16), _pad_rows(ln_g),
      _pad_rows(ln_b))


def _out_kernel(op_ref, os_ref, x_ref, g1_ref, w_ref, lng_ref, lnb_ref, y_ref, *, npb):
    is_prompt = pl.program_id(0) < npb
    o = jnp.where(is_prompt, op_ref[...], os_ref[...])
    out = _bdot(o, w_ref[...])
    xr = ALPHA * x_ref[...] + g1_ref[0] * out
    y_ref[...] = _layer_norm_rows(xr, lng_ref[0:1, :], lnb_ref[0:1, :])


def _swa_out(o_p, o_s, x, mod, layer, n_prompt, dec_seq, w_out, ln_g, ln_b):
    n, d = x.shape
    r = ROW_BLOCK
    sp, ss = _stream_specs(n_prompt, r, d)
    return pl.pallas_call(
        functools.partial(_out_kernel, npb=n_prompt // r),
        grid=(n // r,),
        in_specs=[
            sp, ss,
            pl.BlockSpec((r, d), lambda i: (i, 0)),
            _mod_spec(layer, 2, n_prompt, dec_seq, r),
            pl.BlockSpec((d, d), lambda i: (0, 0)),
            pl.BlockSpec((SUBLANES, d), lambda i: (0, 0)),
            pl.BlockSpec((SUBLANES, d), lambda i: (0, 0)),
        ],
        out_specs=pl.BlockSpec((r, d), lambda i: (i, 0)),
        out_shape=jax.ShapeDtypeStruct((n, d), F32),
        compiler_params=_cparams("parallel"),
        name="swa_out",
    )(o_p, o_s, x, mod, w_out.astype(BF16), _pad_rows(ln_g), _pad_rows(ln_b))


SWA_SCALE = SWA_DH ** -0.5


def _stack_group(q, kv):
    base = kv * SWA_GROUP * SWA_DH
    return jnp.concatenate([q[:, base + g * SWA_DH: base + (g + 1) * SWA_DH] for g in range(SWA_GROUP)], axis=0)


def _sink_col(sink_ref, kv, rows):
    return jnp.concatenate(
        [jnp.full((rows, 1), sink_ref[kv * SWA_GROUP + g], F32) for g in range(SWA_GROUP)], axis=0)


def _ctx_attn_kernel(sink_ref, q_ref, k_ref, v_ref, o_ref):
    q = q_ref[...]
    k = k_ref[...]
    v = v_ref[...]
    t = q.shape[0]
    outs = []
    for kv in range(SWA_KV_HEADS):
        kh = k[:, kv * SWA_DH:(kv + 1) * SWA_DH]
        vh = v[:, kv * SWA_DH:(kv + 1) * SWA_DH]
        s = _bdot_nt(_stack_group(q, kv), kh) * SWA_SCALE
        sink = _sink_col(sink_ref, kv, t)
        m = jnp.maximum(jnp.max(s, axis=1, keepdims=True), sink)
        p = jnp.exp(s - m)
        den = jnp.sum(p, axis=1, keepdims=True) + jnp.exp(sink - m)
        o = _bdot(p, vh) / den
        outs.extend(o[g * t:(g + 1) * t, :] for g in range(SWA_GROUP))
    o_ref[...] = jnp.concatenate(outs, axis=1)


def _ctx_attention(qkv, sink, batch, seq):
    kcol = SWA_HEADS * SWA_DH // SWA_KVW
    return pl.pallas_call(
        _ctx_attn_kernel,
        grid=(batch,),
        in_specs=[
            pl.BlockSpec(memory_space=pltpu.SMEM),
            pl.BlockSpec((seq, SWA_HEADS * SWA_DH), lambda b: (b, 0)),
            pl.BlockSpec((seq, SWA_KVW), lambda b: (b, kcol)),
            pl.BlockSpec((seq, SWA_KVW), lambda b: (b, kcol + 1)),
        ],
        out_specs=pl.BlockSpec((seq, SWA_HEADS * SWA_DH), lambda b: (b, 0)),
        out_shape=jax.ShapeDtypeStruct((batch * seq, SWA_HEADS * SWA_DH), F32),
        compiler_params=_cparams("parallel"),
        name="swa_context_attention",
    )(sink.astype(F32), qkv, qkv, qkv)


ROPE_COLS = SWA_HEADS * SWA_DH + SWA_KVW
ROPE_SHIFT = SWA_DH // 4


def _rope_tables(t_len):
    t = jnp.arange(t_len)
    r = (t // GRID_W).astype(F32)
    col = (t % GRID_W).astype(F32)
    inv = ROPE_BASE ** (-jnp.arange(ROT_FREQS, dtype=F32) / ROT_FREQS)
    ang_r = r[:, None] * inv
    ang_c = col[:, None] * inv
    ang = jnp.concatenate([ang_r, ang_r, ang_c, ang_c], -1)
    cos = jnp.cos(ang)
    sin = jnp.sin(ang)
    first = (jnp.arange(SWA_DH) % (2 * ROPE_SHIFT)) < ROPE_SHIFT
    sin_up = jnp.where(first, -sin, 0.0)
    sin_dn = jnp.where(first, 0.0, sin)
    reps = LANES // SWA_DH
    return tuple(jnp.tile(a, (1, reps)) for a in (cos, sin_up, sin_dn))


def _rope_kernel(x_ref, cos_ref, su_ref, sd_ref, o_ref):
    cos = cos_ref[...]
    su = su_ref[...]
    sd = sd_ref[...]
    for j in range(ROPE_COLS // LANES):
        x = x_ref[:, j * LANES:(j + 1) * LANES]
        up = pltpu.roll(x, LANES - ROPE_SHIFT, 1)
        dn = pltpu.roll(x, ROPE_SHIFT, 1)
        o_ref[:, j * LANES:(j + 1) * LANES] = x * cos + up * su + dn * sd


def _rope_qk(qkv, row0, batch, seq):
    r = ROW_BLOCK
    nb = seq // r
    blk0 = row0 // r
    tables = _rope_tables(seq)
    tspec = pl.BlockSpec((r, LANES), lambda b, i: (i, 0))
    return pl.pallas_call(
        _rope_kernel,
        grid=(batch, nb),
        in_specs=[pl.BlockSpec((r, ROPE_COLS), lambda b, i: (blk0 + b * nb + i, 0)), tspec, tspec, tspec],
        out_specs=pl.BlockSpec((r, ROPE_COLS), lambda b, i: (b * nb + i, 0)),
        out_shape=jax.ShapeDtypeStruct((batch * seq, ROPE_COLS), F32),
        compiler_params=_cparams("parallel", "parallel"),
        name="swa_rope",
    )(qkv, *tables)


def _lat_attn_kernel(sink_ref, q_ref, kp_ref, kc_ref, kn_ref, vp_ref, vc_ref, vn_ref, ck_ref, cv_ref, o_ref, *, seq):
    i = pl.program_id(1)
    q = q_ref[...]
    kw = jnp.concatenate([kp_ref[...], kc_ref[...], kn_ref[...]], axis=0)
    vw = jnp.concatenate([vp_ref[...], vc_ref[...], vn_ref[...]], axis=0)
    ck = ck_ref[...]
    cv = cv_ref[...]
    span = 3 * Q_BLOCK
    rows = SWA_GROUP * Q_BLOCK
    q0 = i * Q_BLOCK
    qpos = q0 + lax.broadcasted_iota(jnp.int32, (rows, span), 0) % Q_BLOCK
    kpos = q0 - WINDOW + lax.broadcasted_iota(jnp.int32, (rows, span), 1)
    valid = (kpos >= 0) & (kpos < seq) & (jnp.abs(qpos - kpos) <= WINDOW)
    heads = range(SWA_KV_HEADS)
    cols = [slice(kv * SWA_DH, (kv + 1) * SWA_DH) for kv in heads]
    qg = [_stack_group(q, kv) for kv in heads]
    s_win = [_bdot_nt(qg[kv], kw[:, cols[kv]]) for kv in heads]
    s_ctx = [_bdot_nt(qg[kv], ck[:, cols[kv]]) for kv in heads]
    p_win, p_ctx, den = [], [], []
    for kv in heads:
        sw = jnp.where(valid, s_win[kv] * SWA_SCALE, NEG_INF)
        sc = s_ctx[kv] * SWA_SCALE
        sink = _sink_col(sink_ref, kv, Q_BLOCK)
        m = jnp.maximum(jnp.maximum(jnp.max(sw, axis=1, keepdims=True), jnp.max(sc, axis=1, keepdims=True)), sink)
        p_win.append(jnp.exp(sw - m))
        p_ctx.append(jnp.exp(sc - m))
        den.append(jnp.sum(p_win[kv], axis=1, keepdims=True) + jnp.sum(p_ctx[kv], axis=1, keepdims=True)
                   + jnp.exp(sink - m))
    o_win = [_bdot(p_win[kv], vw[:, cols[kv]]) for kv in heads]
    o_ctx = [_bdot(p_ctx[kv], cv[:, cols[kv]]) for kv in heads]
    outs = []
    for kv in heads:
        o = (o_win[kv] + o_ctx[kv]) / den[kv]
        outs.extend(o[g * Q_BLOCK:(g + 1) * Q_BLOCK, :] for g in range(SWA_GROUP))
    o_ref[...] = jnp.concatenate(outs, axis=1)


def _lat_attention(qk_rope, qkv, cache_k, cache_v, sink, row0, batch, seq):
    nq = seq // Q_BLOCK
    blk0 = row0 // Q_BLOCK
    qw = SWA_HEADS * SWA_DH
    kcol = qw // SWA_KVW
    past = cache_k.shape[0] // batch

    def kspec(off):
        return pl.BlockSpec((Q_BLOCK, SWA_KVW), lambda b, i: (b * nq + jnp.clip(i + off, 0, nq - 1), kcol))

    def vspec(off):
        return pl.BlockSpec((Q_BLOCK, SWA_KVW),
                            lambda b, i: (blk0 + b * nq + jnp.clip(i + off, 0, nq - 1), kcol + 1))

    cspec = pl.BlockSpec((past, SWA_KVW), lambda b, i: (b, 0))
    return pl.pallas_call(
        functools.partial(_lat_attn_kernel, seq=seq),
        grid=(batch, nq),
        in_specs=[
            pl.BlockSpec(memory_space=pltpu.SMEM),
            pl.BlockSpec((Q_BLOCK, qw), lambda b, i: (b * nq + i, 0)),
            kspec(-1), kspec(0), kspec(1),
            vspec(-1), vspec(0), vspec(1),
            cspec, cspec,
        ],
        out_specs=pl.BlockSpec((Q_BLOCK, qw), lambda b, i: (b * nq + i, 0)),
        out_shape=jax.ShapeDtypeStruct((batch * seq, qw), F32),
        compiler_params=_cparams("parallel", "parallel"),
        name="swa_latent_attention",
    )(sink.astype(F32), qk_rope, qk_rope, qk_rope, qk_rope, qkv, qkv, qkv, cache_k, cache_v)


GROUP_SIZE = N_EXPERTS // N_GROUPS
TILE_ROWS = D_MODEL // LANES


def _to_tiles(ref, x):
    rows = x.shape[0]
    for s in range(TILE_ROWS):
        ref[pl.ds(s, rows, stride=TILE_ROWS), :] = x[:, s * LANES:(s + 1) * LANES]


def _from_tiles(ref, rows):
    return jnp.concatenate([ref[pl.ds(s, rows, stride=TILE_ROWS), :] for s in range(TILE_ROWS)], axis=1)


def _first_argmax(v, iota, axis, size):
    m = jnp.max(v, axis=axis, keepdims=True)
    idx = jnp.min(jnp.where(v == m, iota, size), axis=axis, keepdims=True)
    return m, idx


def _router_kernel(x_ref, sc_ref, sh_ref, rt_ref, bias_ref, hs_ref, idx_ref, w_ref, pos_ref, wtm_ref, cnt_ref):
    i = pl.program_id(0)
    rows = x_ref.shape[0]

    @pl.when(i == 0)
    def _():
        cnt_ref[...] = jnp.zeros_like(cnt_ref)

    hs = x_ref[...] * (1.0 + sc_ref[0]) + sh_ref[0]
    _to_tiles(hs_ref, hs)
    logits = lax.dot_general(rt_ref[...], hs, (((1,), (1,)), ((), ())), precision=HIGHEST,
                             preferred_element_type=F32)
    scores = jax.nn.sigmoid(logits)
    biased = scores + bias_ref[:, 0:1]
    b3 = biased.reshape(N_GROUPS, GROUP_SIZE, rows)
    mem = lax.broadcasted_iota(jnp.int32, b3.shape, 1)
    m1, i1 = _first_argmax(b3, mem, 1, GROUP_SIZE)
    m2 = jnp.max(jnp.where(mem == i1, -jnp.inf, b3), axis=1, keepdims=True)
    grp = (m1 + m2).reshape(N_GROUPS, rows)
    giota = lax.broadcasted_iota(jnp.int32, grp.shape, 0)
    gsel = jnp.zeros(grp.shape, jnp.bool_)
    for _ in range(TOPK_GROUPS):
        _, gi = _first_argmax(grp, giota, 0, N_GROUPS)
        hit = giota == gi
        gsel = gsel | hit
        grp = jnp.where(hit, -jnp.inf, grp)
    gmask = jnp.broadcast_to(gsel.reshape(N_GROUPS, 1, rows), b3.shape).reshape(N_EXPERTS, rows)
    sel = jnp.where(gmask, biased, -jnp.inf)
    eiota = lax.broadcasted_iota(jnp.int32, sel.shape, 0)
    chosen_any = jnp.zeros(sel.shape, jnp.bool_)
    idx_rows, w_rows, hits = [], [], []
    for _ in range(TOP_K):
        _, ei = _first_argmax(sel, eiota, 0, N_EXPERTS)
        hit = eiota == ei
        chosen_any = chosen_any | hit
        idx_rows.append(ei)
        w_rows.append(jnp.sum(jnp.where(hit, scores, 0.0), axis=0, keepdims=True))
        hits.append(hit)
        sel = jnp.where(hit, -jnp.inf, sel)
    wsum = w_rows[0]
    for w in w_rows[1:]:
        wsum = wsum + w
    w_rows = [w / wsum * ROUTED_SCALE for w in w_rows]
    onehot = chosen_any.astype(F32)
    s_i = lax.broadcasted_iota(jnp.int32, (rows, rows), 0)
    t_i = lax.broadcasted_iota(jnp.int32, (rows, rows), 1)
    earlier = (s_i < t_i).astype(F32)
    rank = cnt_ref[:, 0:1] + _bdot(onehot, earlier)
    cnt_ref[...] = cnt_ref[...] + jnp.sum(onehot, axis=1, keepdims=True)
    pos_rows = [jnp.sum(jnp.where(hit, rank, 0.0), axis=0, keepdims=True) for hit in hits]
    idx_ref[...] = jnp.concatenate(idx_rows, axis=0)
    w8 = jnp.concatenate(w_rows, axis=0)
    w_ref[...] = w8
    pos_ref[...] = jnp.concatenate(pos_rows, axis=0).astype(jnp.int32)
    wtm_ref[...] = jnp.concatenate([w8, jnp.zeros((LANES - TOP_K, rows), F32)], axis=0).T


def _router(x, mod, layer, n_prompt, dec_seq, router, bias):
    n, d = x.shape
    r = ROW_BLOCK
    bias_pad = jnp.zeros((N_EXPERTS, LANES), F32).at[:, 0].set(bias.astype(F32))
    kspec = pl.BlockSpec((TOP_K, r), lambda i: (0, i))
    return pl.pallas_call(
        _router_kernel,
        grid=(n // r,),
        in_specs=[
            pl.BlockSpec((r, d), lambda i: (i, 0)),
            _mod_spec(layer, 4, n_prompt, dec_seq, r),
            _mod_spec(layer, 3, n_prompt, dec_seq, r),
            pl.BlockSpec((N_EXPERTS, d), lambda i: (0, 0)),
            pl.BlockSpec((N_EXPERTS, LANES), lambda i: (0, 0)),
        ],
        out_specs=[
            pl.BlockSpec((r * TILE_ROWS, LANES), lambda i: (i, 0)),
            kspec, kspec, kspec,
            pl.BlockSpec((r, LANES), lambda i: (i, 0)),
            pl.BlockSpec((N_EXPERTS, LANES), lambda i: (0, 0)),
        ],
        out_shape=[
            jax.ShapeDtypeStruct((n * TILE_ROWS, LANES), F32),
            jax.ShapeDtypeStruct((TOP_K, n), jnp.int32),
            jax.ShapeDtypeStruct((TOP_K, n), F32),
            jax.ShapeDtypeStruct((TOP_K, n), jnp.int32),
            jax.ShapeDtypeStruct((n, LANES), F32),
            jax.ShapeDtypeStruct((N_EXPERTS, LANES), F32),
        ],
        compiler_params=_cparams("arbitrary"),
        name="moe_router",
    )(x, mod, mod, router.T.astype(F32), bias_pad)


DISPATCH_TOKENS = 256


def _dest_kernel(start_ref, idx_ref, pos_ref, o_ref):
    idx = idx_ref[...]
    start = jnp.zeros(idx.shape, jnp.int32)
    for e in range(N_EXPERTS):
        start = jnp.where(idx == e, start_ref[e], start)
    o_ref[...] = (start + pos_ref[...]) * TILE_ROWS


def _dest_rows(pad_start, idx, pos):
    k, n = idx.shape
    t = min(2048, n)
    spec = pl.BlockSpec((k, t), lambda i: (0, i))
    return pl.pallas_call(
        _dest_kernel,
        grid=(n // t,),
        in_specs=[pl.BlockSpec(memory_space=pltpu.SMEM), spec, spec],
        out_specs=spec,
        out_shape=jax.ShapeDtypeStruct((k, n), jnp.int32),
        compiler_params=_cparams("parallel"),
        name="moe_dest_rows",
    )(pad_start, idx, pos)


def _tile_copy(src_ref, src_row, dst_ref, dst_row, sem):
    return pltpu.make_async_copy(src_ref.at[pl.ds(pl.multiple_of(src_row, TILE_ROWS), TILE_ROWS), :],
                                 dst_ref.at[pl.ds(pl.multiple_of(dst_row, TILE_ROWS), TILE_ROWS), :], sem)


def _dispatch_kernel(dest_ref, hs_ref, xs_ref, sem):
    tokens = hs_ref.shape[0] // TILE_ROWS

    def copy(k, t):
        return _tile_copy(hs_ref, t * TILE_ROWS, xs_ref, dest_ref[k, t], sem)

    def start(t, carry):
        for k in range(TOP_K):
            copy(k, t).start(priority=k % 2)
        return carry

    lax.fori_loop(0, tokens, start, 0)

    def wait(t, carry):
        for k in range(TOP_K):
            copy(k, t).wait()
        return carry

    lax.fori_loop(0, tokens, wait, 0)


def _dispatch(hs, dest, n_rows):
    t = DISPATCH_TOKENS
    n = hs.shape[0] // TILE_ROWS
    return pl.pallas_call(
        _dispatch_kernel,
        grid=(n // t,),
        in_specs=[
            pl.BlockSpec((TOP_K, t), lambda i: (0, i), memory_space=pltpu.SMEM),
            pl.BlockSpec((t * TILE_ROWS, LANES), lambda i: (i, 0)),
        ],
        out_specs=pl.BlockSpec(memory_space=pl.ANY),
        out_shape=jax.ShapeDtypeStruct((n_rows * TILE_ROWS, LANES), F32),
        scratch_shapes=[pltpu.SemaphoreType.DMA],
        compiler_params=_cparams("arbitrary"),
        name="moe_dispatch",
    )(dest, hs)


def _expert_kernel(be_ref, bv_ref, xs_ref, wg_ref, wu_ref, wd_ref, y_ref):
    del be_ref
    valid = bv_ref[pl.program_id(0)]

    @pl.when(valid > 0)
    def _():
        x = _from_tiles(xs_ref, MOE_ROWS)
        row = lax.broadcasted_iota(jnp.int32, x.shape, 0)
        x = jnp.where(row < valid, x, 0.0)
        h = _silu(_bdot(x, wg_ref[0])) * _bdot(x, wu_ref[0])
        _to_tiles(y_ref, _bdot(h, wd_ref[0]))

    @pl.when(valid <= 0)
    def _():
        y_ref[...] = jnp.zeros_like(y_ref)


def _experts(xs, blk_expert, blk_valid, w_gate, w_up, w_down):
    nb = xs.shape[0] // (MOE_ROWS * TILE_ROWS)
    _, d, de = w_gate.shape
    blk = pl.BlockSpec((MOE_ROWS * TILE_ROWS, LANES), lambda i, be, bv: (i, 0))
    grid_spec = pltpu.PrefetchScalarGridSpec(
        num_scalar_prefetch=2,
        grid=(nb,),
        in_specs=[
            blk,
            pl.BlockSpec((1, d, de), lambda i, be, bv: (be[i], 0, 0)),
            pl.BlockSpec((1, d, de), lambda i, be, bv: (be[i], 0, 0)),
            pl.BlockSpec((1, de, d), lambda i, be, bv: (be[i], 0, 0)),
        ],
        out_specs=blk,
    )
    return pl.pallas_call(
        _expert_kernel,
        grid_spec=grid_spec,
        out_shape=jax.ShapeDtypeStruct(xs.shape, F32),
        compiler_params=_cparams("arbitrary"),
        name="moe_experts",
    )(blk_expert, blk_valid, xs, w_gate.astype(BF16), w_up.astype(BF16), w_down.astype(BF16))


COMBINE_TOKENS = 128


def _combine_kernel(dest_ref, y_ref, wtm_ref, hs_ref, x_ref, g2_ref, sg_ref, su_ref, sd_ref,
                    lng_ref, lnb_ref, o_ref, ybuf, sem):
    tokens = x_ref.shape[0]

    def copy(k, t):
        return _tile_copy(y_ref, dest_ref[k, t], ybuf.at[k], t * TILE_ROWS, sem)

    def start(t, carry):
        for k in range(TOP_K):
            copy(k, t).start(priority=k % 2)
        return carry

    lax.fori_loop(0, tokens, start, 0)
    hs = _from_tiles(hs_ref, tokens)
    shared = _bdot(_silu(_bdot(hs, sg_ref[...])) * _bdot(hs, su_ref[...]), sd_ref[...])

    def wait(t, carry):
        for k in range(TOP_K):
            copy(k, t).wait()
        return carry

    lax.fori_loop(0, tokens, wait, 0)
    wtm = wtm_ref[...]
    routed = _from_tiles(ybuf.at[0], tokens) * wtm[:, 0:1]
    for k in range(1, TOP_K):
        routed = routed + _from_tiles(ybuf.at[k], tokens) * wtm[:, k:k + 1]
    xr = ALPHA * x_ref[...] + g2_ref[0] * (routed + shared)
    o_ref[...] = _layer_norm_rows(xr, lng_ref[0:1, :], lnb_ref[0:1, :])


def _combine(y, dest, wtm, hs, x, mod, layer, n_prompt, dec_seq, s_gate, s_up, s_down, ln_g, ln_b):
    n, d = x.shape
    t = COMBINE_TOKENS
    ds = s_gate.shape[1]
    row = pl.BlockSpec((t, d), lambda i: (i, 0))
    return pl.pallas_call(
        _combine_kernel,
        grid=(n // t,),
        in_specs=[
            pl.BlockSpec((TOP_K, t), lambda i: (0, i), memory_space=pltpu.SMEM),
            pl.BlockSpec(memory_space=pl.ANY),
            pl.BlockSpec((t, LANES), lambda i: (i, 0)),
            pl.BlockSpec((t * TILE_ROWS, LANES), lambda i: (i, 0)),
            row,
            _mod_spec(layer, 5, n_prompt, dec_seq, t),
            pl.BlockSpec((d, ds), lambda i: (0, 0)),
            pl.BlockSpec((d, ds), lambda i: (0, 0)),
            pl.BlockSpec((ds, d), lambda i: (0, 0)),
            pl.BlockSpec((SUBLANES, d), lambda i: (0, 0)),
            pl.BlockSpec((SUBLANES, d), lambda i: (0, 0)),
        ],
        out_specs=row,
        out_shape=jax.ShapeDtypeStruct((n, d), F32),
        scratch_shapes=[pltpu.VMEM((TOP_K, t * TILE_ROWS, LANES), F32), pltpu.SemaphoreType.DMA],
        compiler_params=_cparams("arbitrary"),
        name="moe_combine",
    )(dest, y, wtm, hs, x, mod, s_gate.astype(BF16), s_up.astype(BF16), s_down.astype(BF16),
      _pad_rows(ln_g), _pad_rows(ln_b))


def _moe(x, mod, layer, n_prompt, dec_seq, router, bias, w_gate, w_up, w_down, s_gate, s_up, s_down, ln_g, ln_b):
    n, d = x.shape
    hs, idx, _, pos, wtm, cnt = _router(x, mod, layer, n_prompt, dec_seq, router, bias)
    counts = cnt[:, 0].astype(jnp.int32)
    padded = (counts + MOE_ROWS - 1) // MOE_ROWS * MOE_ROWS
    pad_end = jnp.cumsum(padded)
    pad_start = (pad_end - padded).astype(jnp.int32)
    nb = n * TOP_K // MOE_ROWS + N_EXPERTS
    blk_row = jnp.arange(nb, dtype=jnp.int32) * MOE_ROWS
    blk_expert = jnp.minimum(jnp.sum(pad_end[None, :] <= blk_row[:, None], axis=1), N_EXPERTS - 1).astype(jnp.int32)
    blk_valid = jnp.clip(pad_start[blk_expert] + counts[blk_expert] - blk_row, 0, MOE_ROWS).astype(jnp.int32)
    dest = _dest_rows(pad_start, idx, pos)
    xs = _dispatch(hs, dest, nb * MOE_ROWS)
    y = _experts(xs, blk_expert, blk_valid, w_gate, w_up, w_down)
    return _combine(y, dest, wtm, hs, x, mod, layer, n_prompt, dec_seq, s_gate, s_up, s_down, ln_g, ln_b)


def kernel(x_prompt, x_sample, state_la, cache_k, cache_v, c, c_ctx, w_mod, b_mod, ln_g, ln_b,
           la_w_in, la_conv, la_a_log, la_dt_bias, la_norm_g, la_w_out,
           swa_w_qkv, swa_sink, swa_w_out,
           moe_router, moe_bias, moe_w_gate, moe_w_up, moe_w_down, sh_w_gate, sh_w_up, sh_w_down):
    bp, tp, d = x_prompt.shape
    bs, ts, _ = x_sample.shape
    n_prompt = bp * tp
    assert bs + 1 <= COND_ROWS and d == D_MODEL
    assert n_prompt % ROW_BLOCK == 0 and ts % ROW_BLOCK == 0 and tp % DELTA_ROWS == 0

    cond = jnp.zeros((COND_ROWS, d), F32).at[0].set(c_ctx).at[1:1 + bs].set(c)
    mod = _modulation(cond, w_mod, b_mod)
    x = jnp.concatenate([x_prompt.reshape(n_prompt, d), x_sample.reshape(bs * ts, d)], axis=0)

    def moe(x, i):
        return _moe(x, mod, i, n_prompt, ts, moe_router[i], moe_bias[i], moe_w_gate[i], moe_w_up[i], moe_w_down[i],
                    sh_w_gate[i], sh_w_up[i], sh_w_down[i], ln_g[i, 1], ln_b[i, 1])

    proj, gbc, gbr = _la_in_proj(x, mod, 0, n_prompt, ts, la_w_in[0], la_a_log[0], la_dt_bias[0])
    qkv_p = _conv_qkv(proj, la_conv[0], 0, bp, tp)
    qkv_s = _conv_qkv(proj, la_conv[0], n_prompt, bs, ts)
    opf, opb, s_fin = _delta_scan(_delta_prep(qkv_p, gbc, gbr, 0), None, bp, tp)
    s0 = state_la[:, 0].reshape(bs, N_DIRS * LA_HEADS, LA_DK, LA_DV)
    osf, osb, _ = _delta_scan(_delta_prep(qkv_s, gbc, gbr, n_prompt), s0, bs, ts)
    x = _la_out((opf, opb), (osf, osb), proj, x, mod, 0, n_prompt, ts, la_norm_g[0], la_w_out[0], ln_g[0, 0],
                ln_b[0, 0])
    new_la = s_fin.reshape(bp, 1, N_DIRS, LA_HEADS, LA_DK, LA_DV)
    x = moe(x, 0)

    qkv = _mod_proj(x, mod, 1, n_prompt, ts, swa_w_qkv[0], "swa_qkv_proj")
    qw = SWA_HEADS * SWA_DH
    new_k = qkv[:n_prompt, qw:qw + SWA_KVW].reshape(bp, 1, tp, SWA_KV_HEADS, SWA_DH)
    new_v = qkv[:n_prompt, qw + SWA_KVW:].reshape(bp, 1, tp, SWA_KV_HEADS, SWA_DH)
    a_p = _ctx_attention(qkv, swa_sink[0], bp, tp)
    qk_rope = _rope_qk(qkv, n_prompt, bs, ts)
    past = cache_k.shape[2]
    a_s = _lat_attention(qk_rope, qkv, cache_k[:, 0].reshape(bs * past, SWA_KVW),
                         cache_v[:, 0].reshape(bs * past, SWA_KVW), swa_sink[0], n_prompt, bs, ts)
    x = _swa_out(a_p, a_s, x, mod, 1, n_prompt, ts, swa_w_out[0], ln_g[1, 0], ln_b[1, 0])
    x = moe(x, 1)

    y_prompt = x[:n_prompt].reshape(bp, tp, d)
    y_sample = x[n_prompt:].reshape(bs, ts, d)
    return y_prompt, y_sample, new_la, new_k, new_v
```

```python
import functools
import math

import jax
import jax.numpy as jnp
from jax import lax
from jax.experimental import pallas as pl
from jax.experimental.pallas import tpu as pltpu

F32 = jnp.float32
BF16 = jnp.bfloat16
HIGHEST = lax.Precision.HIGHEST

LANES = 128
SUBLANES = 8
VMEM_LIMIT_BYTES = 56 * 1024 * 1024

D_MODEL = 1024
DEPTH = 2
GRID_W = 64
LA_DK = 128
LA_DV = 128
LA_HEADS = D_MODEL // LA_DK
LA_QKV = LA_HEADS * (2 * LA_DK + LA_DV)
CONV_K = 5
CHUNK = 64
SUB = 16
SWA_DH = 64
SWA_HEADS = D_MODEL // SWA_DH
SWA_KV_HEADS = SWA_HEADS // 4
SWA_GROUP = SWA_HEADS // SWA_KV_HEADS
SWA_KVW = SWA_KV_HEADS * SWA_DH
WINDOW = 128
Q_BLOCK = 128
ROT_FREQS = SWA_DH // 4
ROPE_BASE = 10000.0
NEG_INF = -1e30
N_EXPERTS = 64
TOP_K = 8
N_GROUPS = 8
TOPK_GROUPS = 4
D_EXPERT = 256
D_SHARED = 256
ROUTED_SCALE = 2.5
ALPHA = (2 * DEPTH) ** 0.25
LN_EPS = 1e-5
RMS_EPS = 1e-6

ROW_BLOCK = 256
MOE_ROWS = 256
COND_ROWS = 8


def _cparams(*sem):
    return pltpu.CompilerParams(dimension_semantics=sem, vmem_limit_bytes=VMEM_LIMIT_BYTES)


def _bdot(a, b):
    return jnp.dot(a.astype(BF16), b.astype(BF16), preferred_element_type=F32)


def _bdot_nt(a, b):
    return lax.dot_general(a.astype(BF16), b.astype(BF16), (((1,), (1,)), ((), ())), preferred_element_type=F32)


def _bdot_tn(a, b):
    return lax.dot_general(a.astype(BF16), b.astype(BF16), (((0,), (0,)), ((), ())), preferred_element_type=F32)


def _silu(x):
    return x * jax.nn.sigmoid(x)


def _layer_norm_rows(x, g, b):
    mu = jnp.mean(x, axis=-1, keepdims=True)
    xc = x - mu
    var = jnp.mean(xc * xc, axis=-1, keepdims=True)
    return xc * lax.rsqrt(var + LN_EPS) * g + b


MOD_COLS = 512


def _mod_kernel(c_ref, w_ref, b_ref, o_ref):
    s = _silu(c_ref[...])
    o_ref[0] = jnp.dot(s, w_ref[0], precision=HIGHEST, preferred_element_type=F32) + b_ref[0]


def _modulation(cond, w_mod, b_mod):
    depth, d, n6 = w_mod.shape
    out = pl.pallas_call(
        _mod_kernel,
        grid=(depth, n6 // MOD_COLS),
        in_specs=[
            pl.BlockSpec((COND_ROWS, d), lambda l, j: (0, 0)),
            pl.BlockSpec((1, d, MOD_COLS), lambda l, j: (l, 0, j)),
            pl.BlockSpec((1, 1, MOD_COLS), lambda l, j: (l, 0, j)),
        ],
        out_specs=pl.BlockSpec((1, COND_ROWS, MOD_COLS), lambda l, j: (l, 0, j)),
        out_shape=jax.ShapeDtypeStruct((depth, COND_ROWS, n6), F32),
        compiler_params=_cparams("parallel", "parallel"),
        name="adaln_modulation",
    )(cond, w_mod, b_mod.reshape(depth, 1, n6))
    return out.reshape(depth * COND_ROWS * 6, 1, d)


def _mod_spec(layer, chunk, n_prompt, dec_seq, rows, blk0=0):
    def index(i, *_):
        tok = (blk0 + i) * rows
        row = jnp.where(tok < n_prompt, 0, 1 + (tok - n_prompt) // dec_seq)
        return ((layer * COND_ROWS + row) * 6 + chunk, 0, 0)

    return pl.BlockSpec((1, 1, D_MODEL), index)


def _la_in_kernel(xp_ref, xs_ref, sc_ref, sh_ref, w_ref, wab_ref, wabt_ref, pc_ref, pr_ref, o_ref, gbc_ref, gbr_ref,
                  *, npb):
    x = jnp.where(pl.program_id(0) < npb, xp_ref[...], xs_ref[...])
    xm = x * (1.0 + sc_ref[0]) + sh_ref[0]
    o_ref[...] = _bdot(xm, w_ref[...])
    ab = jnp.dot(xm, wab_ref[...], precision=HIGHEST, preferred_element_type=F32)
    lane = lax.broadcasted_iota(jnp.int32, ab.shape, 1)
    neg_a = pc_ref[0:1, :]
    dt_b = pc_ref[1:2, :]
    z = ab + dt_b
    softplus = jnp.maximum(z, 0.0) + jnp.log1p(jnp.exp(-jnp.abs(z)))
    gbc_ref[...] = jnp.where(lane < 2 * LA_HEADS, neg_a * softplus, jax.nn.sigmoid(ab))
    rows = xm.shape[0]
    for c in range(rows // CHUNK):
        xc = xm[c * CHUNK:(c + 1) * CHUNK, :]
        abt = lax.dot_general(wabt_ref[...], xc, (((1,), (1,)), ((), ())), precision=HIGHEST,
                              preferred_element_type=F32)
        zt = abt + pr_ref[:, 1:2]
        spt = jnp.maximum(zt, 0.0) + jnp.log1p(jnp.exp(-jnp.abs(zt)))
        gbr_ref[c] = pr_ref[:, 0:1] * spt


def _la_in_proj(x_p, x_s, mod, layer, dec_seq, w_in, a_log, dt_bias):
    n_prompt, d = x_p.shape
    n = n_prompt + x_s.shape[0]
    wide = LA_QKV + LA_HEADS * LA_DV
    w_main = w_in[:, :wide].astype(BF16)
    w_ab = w_in[:, wide:]
    n_ab = w_ab.shape[1]
    w_ab_pad = jnp.zeros((d, LANES), F32).at[:, :n_ab].set(w_ab)
    neg_a = -jnp.exp(a_log.astype(F32)).reshape(-1)
    dtb = dt_bias.astype(F32).reshape(-1)
    pc = jnp.zeros((SUBLANES, LANES), F32).at[0, :2 * LA_HEADS].set(neg_a).at[1, :2 * LA_HEADS].set(dtb)
    pr = jnp.zeros((n_ab, LANES), F32).at[:2 * LA_HEADS, 0].set(neg_a).at[:2 * LA_HEADS, 1].set(dtb)
    r = ROW_BLOCK
    return pl.pallas_call(
        functools.partial(_la_in_kernel, npb=n_prompt // r),
        grid=(n // r,),
        in_specs=_stream_specs(n_prompt, r, d) + [
            _mod_spec(layer, 1, n_prompt, dec_seq, r),
            _mod_spec(layer, 0, n_prompt, dec_seq, r),
            pl.BlockSpec((d, wide), lambda i: (0, 0)),
            pl.BlockSpec((d, LANES), lambda i: (0, 0)),
            pl.BlockSpec((n_ab, d), lambda i: (0, 0)),
            pl.BlockSpec((SUBLANES, LANES), lambda i: (0, 0)),
            pl.BlockSpec((n_ab, LANES), lambda i: (0, 0)),
        ],
        out_specs=[
            pl.BlockSpec((r, wide), lambda i: (i, 0)),
            pl.BlockSpec((r, LANES), lambda i: (i, 0)),
            pl.BlockSpec((r // CHUNK, n_ab, CHUNK), lambda i: (i, 0, 0)),
        ],
        out_shape=[
            jax.ShapeDtypeStruct((n, wide), F32),
            jax.ShapeDtypeStruct((n, LANES), F32),
            jax.ShapeDtypeStruct((n // CHUNK, n_ab, CHUNK), F32),
        ],
        compiler_params=_cparams("parallel"),
        name="deltanet_in_proj",
    )(x_p, x_s, mod, mod, w_main, w_ab_pad, w_ab.T, pc, pr)


def _proj_kernel(x_ref, sc_ref, sh_ref, w_ref, o_ref):
    xm = x_ref[...] * (1.0 + sc_ref[0]) + sh_ref[0]
    o_ref[...] = _bdot(xm, w_ref[...])


def _mod_proj(x, mod, layer, n_prompt, dec_seq, w, name):
    n, d = x.shape
    cols = w.shape[1]
    r = ROW_BLOCK
    return pl.pallas_call(
        _proj_kernel,
        grid=(n // r,),
        in_specs=[
            pl.BlockSpec((r, d), lambda i: (i, 0)),
            _mod_spec(layer, 1, n_prompt, dec_seq, r),
            _mod_spec(layer, 0, n_prompt, dec_seq, r),
            pl.BlockSpec((d, cols), lambda i: (0, 0)),
        ],
        out_specs=pl.BlockSpec((r, cols), lambda i: (i, 0)),
        out_shape=jax.ShapeDtypeStruct((n, cols), F32),
        compiler_params=_cparams("parallel"),
        name=name,
    )(x, mod, mod, w.astype(BF16))


CONV_PAD = SUBLANES
CONV_ROWS = 256


def _conv_kernel(x_ref, w_ref, o_ref, pad_ref, *, seq):
    c = pl.program_id(1)
    zeros = jnp.zeros((CONV_PAD, LANES), F32)
    pad_ref[0:CONV_PAD, :] = zeros
    pad_ref[CONV_PAD + seq:, :] = zeros
    pad_ref[CONV_PAD:CONV_PAD + seq, :] = x_ref[...]
    is_qk = c < 2 * LA_HEADS
    scale = jnp.where(c < LA_HEADS, LA_DK ** -0.5, 1.0).astype(F32)
    half = CONV_K // 2
    for r0 in range(0, seq, CONV_ROWS):
        acc = jnp.zeros((CONV_ROWS, LANES), F32)
        for j in range(CONV_K):
            start = r0 + CONV_PAD + j - half
            acc = acc + w_ref[j:j + 1, :] * pad_ref[start:start + CONV_ROWS, :]
        y = _silu(acc)
        nrm = lax.rsqrt(jnp.sum(y * y, axis=-1, keepdims=True) + 1e-6) * scale
        o_ref[r0:r0 + CONV_ROWS, :] = y * jnp.where(is_qk, nrm, 1.0)


def _conv_qkv(proj, conv_w, row0, batch, seq):
    w = jnp.zeros((SUBLANES, LA_QKV), F32).at[:CONV_K].set(conv_w.astype(F32))
    blk0 = row0 // seq
    return pl.pallas_call(
        functools.partial(_conv_kernel, seq=seq),
        grid=(batch, LA_QKV // LANES),
        in_specs=[
            pl.BlockSpec((seq, LANES), lambda b, c: (blk0 + b, c)),
            pl.BlockSpec((SUBLANES, LANES), lambda b, c: (0, c)),
        ],
        out_specs=pl.BlockSpec((seq, LANES), lambda b, c: (b, c)),
        out_shape=jax.ShapeDtypeStruct((batch * seq, LA_QKV), F32),
        scratch_shapes=[pltpu.VMEM((seq + 2 * CONV_PAD, LANES), F32)],
        compiler_params=_cparams("parallel", "parallel"),
        name="deltanet_conv",
    )(proj, w)


DELTA_ROWS = 256
PAIR = 2 * CHUNK
N_DIRS = 2


PREP_HEADS = 2


def _pair_masks(rev):
    ii = lax.broadcasted_iota(jnp.int32, (PAIR, PAIR), 0)
    jj = lax.broadcasted_iota(jnp.int32, (PAIR, PAIR), 1)
    same_chunk = (ii // CHUNK) == (jj // CHUNK)
    same_sub = (ii // SUB) == (jj // SUB)
    if rev:
        return same_chunk, same_sub, same_chunk & (jj >= ii), same_chunk & (jj > ii), same_chunk & (ii >= jj)
    return same_chunk, same_sub, same_chunk & (jj <= ii), same_chunk & (jj < ii), same_chunk & (ii <= jj)


def _delta_prep_kernel(q_ref, k_ref, v_ref, gbc_ref, gbr_ref, u_ref, wq_ref, kd_ref, qk_ref, gl_ref):
    hp = pl.program_id(1)
    lane = lax.broadcasted_iota(jnp.int32, (PAIR, LANES), 1)
    cpp = PAIR // CHUNK
    masks = [_pair_masks(rev) for rev in (False, True)]
    probs = []
    for hh in range(PREP_HEADS):
        cols = slice(hh * LA_DK, (hh + 1) * LA_DK)
        for pr in range(DELTA_ROWS // PAIR):
            rows = slice(pr * PAIR, (pr + 1) * PAIR)
            q = q_ref[rows, cols]
            k = k_ref[rows, cols]
            v = v_ref[rows, cols]
            gbc = gbc_ref[rows, :]
            gram = _bdot_nt(k, k)
            qk_raw = _bdot_nt(q, k)
            for d in range(N_DIRS):
                same_chunk, same_sub, before, strict, before_t = masks[d]
                h = hp * PREP_HEADS + hh
                g_lane = d * LA_HEADS + h
                b_lane = (N_DIRS + d) * LA_HEADS + h
                g_col = jnp.sum(jnp.where(lane == g_lane, gbc, 0.0), axis=1, keepdims=True)
                b_col = jnp.sum(jnp.where(lane == b_lane, gbc, 0.0), axis=1, keepdims=True)
                g_row = jnp.concatenate([gbr_ref[pr * cpp + c, pl.ds(g_lane, 1), :] for c in range(cpp)], axis=1)
                gam_col = jnp.sum(jnp.where(before, g_row, 0.0), axis=1, keepdims=True)
                gam_row = jnp.sum(jnp.where(before_t, g_col, 0.0), axis=0, keepdims=True)
                tot_col = jnp.sum(jnp.where(same_chunk, g_row, 0.0), axis=1, keepdims=True)
                tot_row = jnp.sum(jnp.where(same_chunk, g_col, 0.0), axis=0, keepdims=True)
                dec = jnp.where(before, jnp.exp(jnp.where(before, gam_col - gam_row, 0.0)), 0.0)
                e_col = jnp.exp(gam_col)
                nm = jnp.where(strict, -(gram * b_col * dec), 0.0)
                nd = jnp.where(same_sub, nm, 0.0)
                probs.append(dict(
                    hh=hh, pr=pr, d=d, rows=rows, nd=nd, ne=nm - nd,
                    rhs=jnp.concatenate([v * b_col, k * (b_col * e_col)], axis=1),
                    qd=q * e_col, kd=k * jnp.exp(tot_col - gam_col), qkm=qk_raw * dec, gl=jnp.exp(tot_row)))
    for p in probs:
        p["yd"] = p["nd"]
        p["p"] = p["nd"]
    for _ in range(3):
        for p in probs:
            p["p"] = _bdot(p["p"], p["p"])
        for p in probs:
            p["yd"] = p["yd"] + p["p"] + _bdot(p["yd"], p["p"])
    for p in probs:
        p["f"] = p["ne"] + _bdot(p["yd"], p["ne"])
    for p in probs:
        p["f2"] = _bdot(p["f"], p["f"])
    for p in probs:
        p["gm"] = p["f"] + p["f2"] + _bdot(p["f"], p["f2"])
    for p in probs:
        p["yt"] = p["gm"] + p["yd"] + _bdot(p["gm"], p["yd"])
    for p in probs:
        p["sol"] = p["rhs"] + _bdot(p["yt"], p["rhs"])
    for p in probs:
        d, hh, pr, rows = p["d"], p["hh"], p["pr"], p["rows"]
        u = p["sol"][:, :LA_DV]
        w = p["sol"][:, LA_DV:]
        u_ref[d, hh, rows, :] = u
        kd_ref[d, hh, rows, :] = p["kd"].T.astype(BF16)
        qk_ref[d, hh, rows, :] = p["qkm"].astype(BF16)
        for c in range(cpp):
            cg = pr * cpp + c
            cr = slice(c * CHUNK, (c + 1) * CHUNK)
            wq_ref[d, hh, cg * PAIR:cg * PAIR + CHUNK, :] = w[cr, :].astype(BF16)
            wq_ref[d, hh, cg * PAIR + CHUNK:(cg + 1) * PAIR, :] = p["qd"][cr, :].astype(BF16)
            gl_ref[d, hh, cg] = p["gl"][:, c * CHUNK:c * CHUNK + 1] + jnp.zeros((1, LANES), F32)


def _delta_prep(qkv, gbc, gbr, row0):
    n = qkv.shape[0]
    r = DELTA_ROWS
    g0 = row0 // r
    n_ab = gbr.shape[1]
    cpb = r // CHUNK
    hg = LA_HEADS // PREP_HEADS
    hw = PREP_HEADS * LA_DK
    hm = lambda rows, dt: jax.ShapeDtypeStruct((N_DIRS, LA_HEADS, rows, LANES), dt)
    hspec = lambda rows: pl.BlockSpec((N_DIRS, PREP_HEADS, rows, LANES), lambda i, h: (0, h, i, 0))
    return pl.pallas_call(
        _delta_prep_kernel,
        grid=(n // r, hg),
        in_specs=[
            pl.BlockSpec((r, hw), lambda i, h: (i, h)),
            pl.BlockSpec((r, hw), lambda i, h: (i, hg + h)),
            pl.BlockSpec((r, hw), lambda i, h: (i, 2 * hg + h)),
            pl.BlockSpec((r, LANES), lambda i, h: (g0 + i, 0)),
            pl.BlockSpec((cpb, n_ab, CHUNK), lambda i, h: (g0 + i, 0, 0)),
        ],
        out_specs=[
            hspec(r), hspec(2 * r), hspec(r), hspec(r),
            pl.BlockSpec((N_DIRS, PREP_HEADS, cpb, 1, LANES), lambda i, h: (0, h, i, 0, 0)),
        ],
        out_shape=[
            hm(n, F32), hm(2 * n, BF16), hm(n, BF16), hm(n, BF16),
            jax.ShapeDtypeStruct((N_DIRS, LA_HEADS, n // CHUNK, 1, LANES), F32),
        ],
        compiler_params=_cparams("parallel", "parallel"),
        name="deltanet_prep",
    )(qkv, qkv, qkv, gbc, gbr)


def _delta_scan_kernel(*refs, has_s0):
    u_refs, wq_refs, kd_refs, qk_refs, gl_refs = (refs[0:2], refs[2:4], refs[4:6], refs[6:8], refs[8:10])
    if has_s0:
        s0_ref, of_ref, ob_ref, sfin_ref, s_ref = refs[10:]
    else:
        of_ref, ob_ref, sfin_ref, s_ref = refs[10:]
    o_refs = (of_ref, ob_ref)
    t = pl.program_id(1)

    @pl.when(t == 0)
    def _():
        if has_s0:
            s_ref[...] = s0_ref[0]
        else:
            s_ref[...] = jnp.zeros_like(s_ref)

    n_pairs = DELTA_ROWS // PAIR
    cpp = PAIR // CHUNK
    chains = [(d, h) for d in range(N_DIRS) for h in range(LA_HEADS)]
    zeros = jnp.zeros((CHUNK, LA_DV), F32)
    for step in range(n_pairs):
        pair_of = [n_pairs - 1 - step if d == 1 else step for d in range(N_DIRS)]
        s = [s_ref[d * LA_HEADS + h] for d, h in chains]
        v_new = [[None] * cpp for _ in chains]
        o_inter = [[None] * cpp for _ in chains]
        for j in range(cpp):
            chunk_of = [cpp - 1 - j if d == 1 else j for d in range(N_DIRS)]
            r = []
            for i, (d, h) in enumerate(chains):
                cg = pair_of[d] * cpp + chunk_of[d]
                r.append(jnp.dot(wq_refs[d][0, h, cg * PAIR:(cg + 1) * PAIR, :], s[i].astype(BF16),
                                 preferred_element_type=F32))
            for i, (d, h) in enumerate(chains):
                c = chunk_of[d]
                cg = pair_of[d] * cpp + c
                v_new[i][c] = u_refs[d][0, h, cg * CHUNK:(cg + 1) * CHUNK, :] - r[i][:CHUNK]
                o_inter[i][c] = r[i][CHUNK:]
                vz = jnp.concatenate([v_new[i][c] if m == c else zeros for m in range(cpp)], axis=0)
                kd_t = kd_refs[d][0, h, pair_of[d] * PAIR:(pair_of[d] + 1) * PAIR, :]
                s[i] = s[i] * gl_refs[d][0, h, cg] + jnp.dot(kd_t, vz.astype(BF16), preferred_element_type=F32)
        for i, (d, h) in enumerate(chains):
            s_ref[d * LA_HEADS + h] = s[i]
            rows = slice(pair_of[d] * PAIR, (pair_of[d] + 1) * PAIR)
            o_intra = jnp.dot(qk_refs[d][0, h, rows, :], jnp.concatenate(v_new[i], axis=0).astype(BF16),
                              preferred_element_type=F32)
            o_refs[d][h, rows, :] = jnp.concatenate(o_inter[i], axis=0) + o_intra

    @pl.when(t == pl.num_programs(1) - 1)
    def _():
        sfin_ref[0] = s_ref[...]


def _delta_scan(prep, s0, batch, seq):
    u, wq, kd, qk, gl = prep
    nt = seq // DELTA_ROWS
    r = DELTA_ROWS
    cpb = r // CHUNK
    n_chain = N_DIRS * LA_HEADS

    def tt(d, t):
        return nt - 1 - t if d == 1 else t

    def dspecs(rows):
        return [pl.BlockSpec((1, LA_HEADS, rows, LANES), functools.partial(
            lambda b, t, d: (d, 0, b * nt + tt(d, t), 0), d=d)) for d in range(N_DIRS)]

    gl_specs = [pl.BlockSpec((1, LA_HEADS, cpb, 1, LANES), functools.partial(
        lambda b, t, d: (d, 0, b * nt + tt(d, t), 0, 0), d=d)) for d in range(N_DIRS)]
    in_specs = dspecs(r) + dspecs(2 * r) + dspecs(r) + dspecs(r) + gl_specs
    args = [u, u, wq, wq, kd, kd, qk, qk, gl, gl]
    has_s0 = s0 is not None
    if has_s0:
        in_specs.append(pl.BlockSpec((1, n_chain, LA_DK, LA_DV), lambda b, t: (b, 0, 0, 0)))
        args.append(s0)
    o_shape = jax.ShapeDtypeStruct((LA_HEADS, batch * seq, LA_DV), F32)
    o_specs = [pl.BlockSpec((LA_HEADS, r, LANES), functools.partial(
        lambda b, t, d: (0, b * nt + tt(d, t), 0), d=d)) for d in range(N_DIRS)]
    return pl.pallas_call(
        functools.partial(_delta_scan_kernel, has_s0=has_s0),
        grid=(batch, nt),
        in_specs=in_specs,
        out_specs=o_specs + [pl.BlockSpec((1, n_chain, LA_DK, LA_DV), lambda b, t: (b, 0, 0, 0))],
        out_shape=[o_shape, o_shape, jax.ShapeDtypeStruct((batch, n_chain, LA_DK, LA_DV), F32)],
        scratch_shapes=[pltpu.VMEM((n_chain, LA_DK, LA_DV), F32)],
        compiler_params=_cparams("parallel", "arbitrary"),
        name="deltanet_scan",
    )(*args)


def _stream_specs(n_prompt, rows, cols):
    npb = n_prompt // rows
    return [
        pl.BlockSpec((rows, cols), lambda i: (jnp.minimum(i, npb - 1), 0)),
        pl.BlockSpec((rows, cols), lambda i: (jnp.maximum(i - npb, 0), 0)),
    ]


def _la_out_kernel(ofp_ref, obp_ref, ofs_ref, obs_ref, z_ref, xp_ref, xs_ref, g1_ref, ng_ref, w_ref, lng_ref,
                   lnb_ref, y_ref, *, npb):
    is_prompt = pl.program_id(0) < npb
    x = jnp.where(is_prompt, xp_ref[...], xs_ref[...])
    ng = ng_ref[0:1, :]
    parts = []
    for h in range(LA_HEADS):
        oh = jnp.where(is_prompt, ofp_ref[h] + obp_ref[h], ofs_ref[h] + obs_ref[h])
        r = lax.rsqrt(jnp.mean(oh * oh, axis=-1, keepdims=True) + RMS_EPS)
        parts.append(oh * r * ng)
    on = jnp.concatenate(parts, axis=1) * _silu(z_ref[...])
    out = _bdot(on, w_ref[...])
    xr = ALPHA * x + g1_ref[0] * out
    y_ref[...] = _layer_norm_rows(xr, lng_ref[0:1, :], lnb_ref[0:1, :])


def _pad_rows(v):
    return jnp.zeros((SUBLANES, v.shape[0]), F32).at[0].set(v.astype(F32))


def _la_out(o_p, o_s, proj, x_p, x_s, mod, layer, dec_seq, norm_g, w_out, ln_g, ln_b):
    n_prompt, d = x_p.shape
    n = n_prompt + x_s.shape[0]
    r = ROW_BLOCK
    zblk = LA_QKV // d
    npb = n_prompt // r
    sp = pl.BlockSpec((LA_HEADS, r, LA_DV), lambda i: (0, jnp.minimum(i, npb - 1), 0))
    ss = pl.BlockSpec((LA_HEADS, r, LA_DV), lambda i: (0, jnp.maximum(i - npb, 0), 0))
    return pl.pallas_call(
        functools.partial(_la_out_kernel, npb=n_prompt // r),
        grid=(n // r,),
        in_specs=[
            sp, sp, ss, ss,
            pl.BlockSpec((r, d), lambda i: (i, zblk)),
            *_stream_specs(n_prompt, r, d),
            _mod_spec(layer, 2, n_prompt, dec_seq, r),
            pl.BlockSpec((SUBLANES, LA_DV), lambda i: (0, 0)),
            pl.BlockSpec((d, d), lambda i: (0, 0)),
            pl.BlockSpec((SUBLANES, d), lambda i: (0, 0)),
            pl.BlockSpec((SUBLANES, d), lambda i: (0, 0)),
        ],
        out_specs=pl.BlockSpec((r, d), lambda i: (i, 0)),
        out_shape=jax.ShapeDtypeStruct((n, d), F32),
        compiler_params=_cparams("parallel"),
        name="deltanet_out",
    )(o_p[0], o_p[1], o_s[0], o_s[1], proj, x_p, x_s, mod, _pad_rows(norm_g), w_out.astype(BF16), _pad_rows(ln_g),
      _pad_rows(ln_b))


def _out_kernel(op_ref, os_ref, x_ref, g1_ref, w_ref, lng_ref, lnb_ref, y_ref, *, npb):
    is_prompt = pl.program_id(0) < npb
    o = jnp.where(is_prompt, op_ref[...], os_ref[...])
    out = _bdot(o, w_ref[...])
    xr = ALPHA * x_ref[...] + g1_ref[0] * out
    y_ref[...] = _layer_norm_rows(xr, lng_ref[0:1, :], lnb_ref[0:1, :])


def _swa_out(o_p, o_s, x, mod, layer, n_prompt, dec_seq, w_out, ln_g, ln_b):
    n, d = x.shape
    r = ROW_BLOCK
    sp, ss = _stream_specs(n_prompt, r, d)
    return pl.pallas_call(
        functools.partial(_out_kernel, npb=n_prompt // r),
        grid=(n // r,),
        in_specs=[
            sp, ss,
            pl.BlockSpec((r, d), lambda i: (i, 0)),
            _mod_spec(layer, 2, n_prompt, dec_seq, r),
            pl.BlockSpec((d, d), lambda i: (0, 0)),
            pl.BlockSpec((SUBLANES, d), lambda i: (0, 0)),
            pl.BlockSpec((SUBLANES, d), lambda i: (0, 0)),
        ],
        out_specs=pl.BlockSpec((r, d), lambda i: (i, 0)),
        out_shape=jax.ShapeDtypeStruct((n, d), F32),
        compiler_params=_cparams("parallel"),
        name="swa_out",
    )(o_p, o_s, x, mod, w_out.astype(BF16), _pad_rows(ln_g), _pad_rows(ln_b))


SWA_SCALE = SWA_DH ** -0.5


def _stack_group(q, kv):
    base = kv * SWA_GROUP * SWA_DH
    return jnp.concatenate([q[:, base + g * SWA_DH: base + (g + 1) * SWA_DH] for g in range(SWA_GROUP)], axis=0)


def _sink_col(sink_ref, kv, rows):
    return jnp.concatenate(
        [jnp.full((rows, 1), sink_ref[kv * SWA_GROUP + g], F32) for g in range(SWA_GROUP)], axis=0)


def _ctx_attn_kernel(sink_ref, q_ref, k_ref, v_ref, o_ref):
    q = q_ref[...]
    k = k_ref[...]
    v = v_ref[...]
    t = q.shape[0]
    outs = []
    for kv in range(SWA_KV_HEADS):
        kh = k[:, kv * SWA_DH:(kv + 1) * SWA_DH]
        vh = v[:, kv * SWA_DH:(kv + 1) * SWA_DH]
        s = _bdot_nt(_stack_group(q, kv), kh) * SWA_SCALE
        sink = _sink_col(sink_ref, kv, t)
        m = jnp.maximum(jnp.max(s, axis=1, keepdims=True), sink)
        p = jnp.exp(s - m)
        den = jnp.sum(p, axis=1, keepdims=True) + jnp.exp(sink - m)
        o = _bdot(p, vh) / den
        outs.extend(o[g * t:(g + 1) * t, :] for g in range(SWA_GROUP))
    o_ref[...] = jnp.concatenate(outs, axis=1)


def _ctx_attention(qkv, sink, batch, seq):
    kcol = SWA_HEADS * SWA_DH // SWA_KVW
    return pl.pallas_call(
        _ctx_attn_kernel,
        grid=(batch,),
        in_specs=[
            pl.BlockSpec(memory_space=pltpu.SMEM),
            pl.BlockSpec((seq, SWA_HEADS * SWA_DH), lambda b: (b, 0)),
            pl.BlockSpec((seq, SWA_KVW), lambda b: (b, kcol)),
            pl.BlockSpec((seq, SWA_KVW), lambda b: (b, kcol + 1)),
        ],
        out_specs=pl.BlockSpec((seq, SWA_HEADS * SWA_DH), lambda b: (b, 0)),
        out_shape=jax.ShapeDtypeStruct((batch * seq, SWA_HEADS * SWA_DH), F32),
        compiler_params=_cparams("parallel"),
        name="swa_context_attention",
    )(sink.astype(F32), qkv, qkv, qkv)


ROPE_COLS = SWA_HEADS * SWA_DH + SWA_KVW
ROPE_SHIFT = SWA_DH // 4


def _rope_tables(t_len):
    t = jnp.arange(t_len)
    r = (t // GRID_W).astype(F32)
    col = (t % GRID_W).astype(F32)
    inv = ROPE_BASE ** (-jnp.arange(ROT_FREQS, dtype=F32) / ROT_FREQS)
    ang_r = r[:, None] * inv
    ang_c = col[:, None] * inv
    ang = jnp.concatenate([ang_r, ang_r, ang_c, ang_c], -1)
    cos = jnp.cos(ang)
    sin = jnp.sin(ang)
    first = (jnp.arange(SWA_DH) % (2 * ROPE_SHIFT)) < ROPE_SHIFT
    sin_up = jnp.where(first, -sin, 0.0)
    sin_dn = jnp.where(first, 0.0, sin)
    reps = LANES // SWA_DH
    return tuple(jnp.tile(a, (1, reps)) for a in (cos, sin_up, sin_dn))


def _rope_kernel(x_ref, cos_ref, su_ref, sd_ref, o_ref):
    cos = cos_ref[...]
    su = su_ref[...]
    sd = sd_ref[...]
    for j in range(ROPE_COLS // LANES):
        x = x_ref[:, j * LANES:(j + 1) * LANES]
        up = pltpu.roll(x, LANES - ROPE_SHIFT, 1)
        dn = pltpu.roll(x, ROPE_SHIFT, 1)
        o_ref[:, j * LANES:(j + 1) * LANES] = x * cos + up * su + dn * sd


def _rope_qk(qkv, row0, batch, seq):
    r = ROW_BLOCK
    nb = seq // r
    blk0 = row0 // r
    tables = _rope_tables(seq)
    tspec = pl.BlockSpec((r, LANES), lambda b, i: (i, 0))
    return pl.pallas_call(
        _rope_kernel,
        grid=(batch, nb),
        in_specs=[pl.BlockSpec((r, ROPE_COLS), lambda b, i: (blk0 + b * nb + i, 0)), tspec, tspec, tspec],
        out_specs=pl.BlockSpec((r, ROPE_COLS), lambda b, i: (b * nb + i, 0)),
        out_shape=jax.ShapeDtypeStruct((batch * seq, ROPE_COLS), F32),
        compiler_params=_cparams("parallel", "parallel"),
        name="swa_rope",
    )(qkv, *tables)


def _lat_attn_kernel(sink_ref, q_ref, kp_ref, kc_ref, kn_ref, vp_ref, vc_ref, vn_ref, ck_ref, cv_ref, o_ref, *, seq):
    i = pl.program_id(1)
    q = q_ref[...] * SWA_SCALE
    kw = jnp.concatenate([kp_ref[...], kc_ref[...], kn_ref[...]], axis=0)
    vw = jnp.concatenate([vp_ref[...], vc_ref[...], vn_ref[...]], axis=0)
    ck = ck_ref[...]
    cv = cv_ref[...]
    span = 3 * Q_BLOCK
    rows = SWA_GROUP * Q_BLOCK
    q0 = i * Q_BLOCK
    qpos = q0 + lax.broadcasted_iota(jnp.int32, (rows, span), 0) % Q_BLOCK
    kpos = q0 - WINDOW + lax.broadcasted_iota(jnp.int32, (rows, span), 1)
    valid = (kpos >= 0) & (kpos < seq) & (jnp.abs(qpos - kpos) <= WINDOW)
    heads = range(SWA_KV_HEADS)
    cols = [slice(kv * SWA_DH, (kv + 1) * SWA_DH) for kv in heads]
    qg = [_stack_group(q, kv) for kv in heads]
    s_win = [_bdot_nt(qg[kv], kw[:, cols[kv]]) for kv in heads]
    s_ctx = [_bdot_nt(qg[kv], ck[:, cols[kv]]) for kv in heads]
    p_win, p_ctx, den = [], [], []
    for kv in heads:
        sw = jnp.where(valid, s_win[kv], NEG_INF)
        sc = s_ctx[kv]
        sink = _sink_col(sink_ref, kv, Q_BLOCK)
        m = jnp.maximum(jnp.maximum(jnp.max(sw, axis=1, keepdims=True), jnp.max(sc, axis=1, keepdims=True)), sink)
        p_win.append(jnp.exp(sw - m))
        p_ctx.append(jnp.exp(sc - m))
        den.append(jnp.sum(p_win[kv], axis=1, keepdims=True) + jnp.sum(p_ctx[kv], axis=1, keepdims=True)
                   + jnp.exp(sink - m))
    o_win = [_bdot(p_win[kv], vw[:, cols[kv]]) for kv in heads]
    o_ctx = [_bdot(p_ctx[kv], cv[:, cols[kv]]) for kv in heads]
    outs = []
    for kv in heads:
        o = (o_win[kv] + o_ctx[kv]) / den[kv]
        outs.extend(o[g * Q_BLOCK:(g + 1) * Q_BLOCK, :] for g in range(SWA_GROUP))
    o_ref[...] = jnp.concatenate(outs, axis=1)


def _lat_attention(qk_rope, qkv, cache_k, cache_v, sink, row0, batch, seq):
    nq = seq // Q_BLOCK
    blk0 = row0 // Q_BLOCK
    qw = SWA_HEADS * SWA_DH
    kcol = qw // SWA_KVW
    past = cache_k.shape[0] // batch

    def kspec(off):
        return pl.BlockSpec((Q_BLOCK, SWA_KVW), lambda b, i: (b * nq + jnp.clip(i + off, 0, nq - 1), kcol))

    def vspec(off):
        return pl.BlockSpec((Q_BLOCK, SWA_KVW),
                            lambda b, i: (blk0 + b * nq + jnp.clip(i + off, 0, nq - 1), kcol + 1))

    cspec = pl.BlockSpec((past, SWA_KVW), lambda b, i: (b, 0))
    return pl.pallas_call(
        functools.partial(_lat_attn_kernel, seq=seq),
        grid=(batch, nq),
        in_specs=[
            pl.BlockSpec(memory_space=pltpu.SMEM),
            pl.BlockSpec((Q_BLOCK, qw), lambda b, i: (b * nq + i, 0)),
            kspec(-1), kspec(0), kspec(1),
            vspec(-1), vspec(0), vspec(1),
            cspec, cspec,
        ],
        out_specs=pl.BlockSpec((Q_BLOCK, qw), lambda b, i: (b * nq + i, 0)),
        out_shape=jax.ShapeDtypeStruct((batch * seq, qw), F32),
        compiler_params=_cparams("parallel", "parallel"),
        name="swa_latent_attention",
    )(sink.astype(F32), qk_rope, qk_rope, qk_rope, qk_rope, qkv, qkv, qkv, cache_k, cache_v)


GROUP_SIZE = N_EXPERTS // N_GROUPS
U32 = jnp.uint32
HALF_D = D_MODEL // 2
TILE_ROWS = HALF_D // LANES


def _bf16_bits_high(x):
    return pltpu.bitcast(x.astype(BF16).astype(F32), U32)


def _to_tiles(ref, x):
    rows = x.shape[0]
    for s in range(TILE_ROWS):
        lo = _bf16_bits_high(x[:, s * LANES:(s + 1) * LANES]) >> 16
        hi = _bf16_bits_high(x[:, HALF_D + s * LANES:HALF_D + (s + 1) * LANES])
        ref[pl.ds(s, rows, stride=TILE_ROWS), :] = lo | hi


def _from_tiles(ref, rows):
    words = [ref[pl.ds(s, rows, stride=TILE_ROWS), :] for s in range(TILE_ROWS)]
    lo = [pltpu.bitcast(w << 16, F32) for w in words]
    hi = [pltpu.bitcast(w & jnp.uint32(0xFFFF0000), F32) for w in words]
    return jnp.concatenate(lo + hi, axis=1)


def _first_argmax(v, iota, axis, size):
    m = jnp.max(v, axis=axis, keepdims=True)
    idx = jnp.min(jnp.where(v == m, iota, size), axis=axis, keepdims=True)
    return m, idx


def _router_kernel(x_ref, sc_ref, sh_ref, rt_ref, bias_ref, hs_ref, idx_ref, w_ref, pos_ref, wtm_ref, cnt_ref):
    i = pl.program_id(0)
    rows = x_ref.shape[0]

    @pl.when(i == 0)
    def _():
        cnt_ref[...] = jnp.zeros_like(cnt_ref)

    hs = x_ref[...] * (1.0 + sc_ref[0]) + sh_ref[0]
    _to_tiles(hs_ref, hs)
    logits = lax.dot_general(rt_ref[...], hs, (((1,), (1,)), ((), ())), precision=HIGHEST,
                             preferred_element_type=F32)
    scores = jax.nn.sigmoid(logits)
    biased = scores + bias_ref[:, 0:1]
    b3 = biased.reshape(N_GROUPS, GROUP_SIZE, rows)
    mem = lax.broadcasted_iota(jnp.int32, b3.shape, 1)
    m1, i1 = _first_argmax(b3, mem, 1, GROUP_SIZE)
    m2 = jnp.max(jnp.where(mem == i1, -jnp.inf, b3), axis=1, keepdims=True)
    grp = (m1 + m2).reshape(N_GROUPS, rows)
    giota = lax.broadcasted_iota(jnp.int32, grp.shape, 0)
    gsel = jnp.zeros(grp.shape, jnp.bool_)
    for _ in range(TOPK_GROUPS):
        _, gi = _first_argmax(grp, giota, 0, N_GROUPS)
        hit = giota == gi
        gsel = gsel | hit
        grp = jnp.where(hit, -jnp.inf, grp)
    gmask = jnp.broadcast_to(gsel.reshape(N_GROUPS, 1, rows), b3.shape).reshape(N_EXPERTS, rows)
    sel = jnp.where(gmask, biased, -jnp.inf)
    eiota = lax.broadcasted_iota(jnp.int32, sel.shape, 0)
    chosen_any = jnp.zeros(sel.shape, jnp.bool_)
    idx_rows, w_rows, hits = [], [], []
    for _ in range(TOP_K):
        _, ei = _first_argmax(sel, eiota, 0, N_EXPERTS)
        hit = eiota == ei
        chosen_any = chosen_any | hit
        idx_rows.append(ei)
        w_rows.append(jnp.sum(jnp.where(hit, scores, 0.0), axis=0, keepdims=True))
        hits.append(hit)
        sel = jnp.where(hit, -jnp.inf, sel)
    wsum = w_rows[0]
    for w in w_rows[1:]:
        wsum = wsum + w
    w_rows = [w / wsum * ROUTED_SCALE for w in w_rows]
    onehot = chosen_any.astype(F32)
    s_i = lax.broadcasted_iota(jnp.int32, (rows, rows), 0)
    t_i = lax.broadcasted_iota(jnp.int32, (rows, rows), 1)
    earlier = (s_i < t_i).astype(F32)
    rank = cnt_ref[:, 0:1] + _bdot(onehot, earlier)
    cnt_ref[...] = cnt_ref[...] + jnp.sum(onehot, axis=1, keepdims=True)
    pos_rows = [jnp.sum(jnp.where(hit, rank, 0.0), axis=0, keepdims=True) for hit in hits]
    idx_ref[...] = jnp.concatenate(idx_rows, axis=0)
    w8 = jnp.concatenate(w_rows, axis=0)
    w_ref[...] = w8
    pos_ref[...] = jnp.concatenate(pos_rows, axis=0).astype(jnp.int32)
    wtm_ref[...] = jnp.concatenate([w8, jnp.zeros((LANES - TOP_K, rows), F32)], axis=0).T


def _router(x, mod, layer, n_prompt, dec_seq, router, bias):
    n, d = x.shape
    r = ROW_BLOCK
    bias_pad = jnp.zeros((N_EXPERTS, LANES), F32).at[:, 0].set(bias.astype(F32))
    kspec = pl.BlockSpec((TOP_K, r), lambda i: (0, i))
    return pl.pallas_call(
        _router_kernel,
        grid=(n // r,),
        in_specs=[
            pl.BlockSpec((r, d), lambda i: (i, 0)),
            _mod_spec(layer, 4, n_prompt, dec_seq, r),
            _mod_spec(layer, 3, n_prompt, dec_seq, r),
            pl.BlockSpec((N_EXPERTS, d), lambda i: (0, 0)),
            pl.BlockSpec((N_EXPERTS, LANES), lambda i: (0, 0)),
        ],
        out_specs=[
            pl.BlockSpec((r * TILE_ROWS, LANES), lambda i: (i, 0)),
            kspec, kspec, kspec,
            pl.BlockSpec((r, LANES), lambda i: (i, 0)),
            pl.BlockSpec((N_EXPERTS, LANES), lambda i: (0, 0)),
        ],
        out_shape=[
            jax.ShapeDtypeStruct((n * TILE_ROWS, LANES), U32),
            jax.ShapeDtypeStruct((TOP_K, n), jnp.int32),
            jax.ShapeDtypeStruct((TOP_K, n), F32),
            jax.ShapeDtypeStruct((TOP_K, n), jnp.int32),
            jax.ShapeDtypeStruct((n, LANES), F32),
            jax.ShapeDtypeStruct((N_EXPERTS, LANES), F32),
        ],
        compiler_params=_cparams("arbitrary"),
        name="moe_router",
    )(x, mod, mod, router.T.astype(F32), bias_pad)


DISPATCH_TOKENS = 256


def _dest_kernel(start_ref, idx_ref, pos_ref, o_ref):
    idx = idx_ref[...]
    start = jnp.zeros(idx.shape, jnp.int32)
    for e in range(N_EXPERTS):
        start = jnp.where(idx == e, start_ref[e], start)
    o_ref[...] = (start + pos_ref[...]) * TILE_ROWS


def _dest_rows(pad_start, idx, pos):
    k, n = idx.shape
    t = min(2048, n)
    spec = pl.BlockSpec((k, t), lambda i: (0, i))
    return pl.pallas_call(
        _dest_kernel,
        grid=(n // t,),
        in_specs=[pl.BlockSpec(memory_space=pltpu.SMEM), spec, spec],
        out_specs=spec,
        out_shape=jax.ShapeDtypeStruct((k, n), jnp.int32),
        compiler_params=_cparams("parallel"),
        name="moe_dest_rows",
    )(pad_start, idx, pos)


def _tile_copy(src_ref, src_row, dst_ref, dst_row, sem):
    return pltpu.make_async_copy(src_ref.at[pl.ds(pl.multiple_of(src_row, TILE_ROWS), TILE_ROWS), :],
                                 dst_ref.at[pl.ds(pl.multiple_of(dst_row, TILE_ROWS), TILE_ROWS), :], sem)


def _dispatch_kernel(dest_ref, hs_ref, xs_ref, sem):
    tokens = hs_ref.shape[0] // TILE_ROWS

    def copy(k, t):
        return _tile_copy(hs_ref, t * TILE_ROWS, xs_ref, dest_ref[k, t], sem)

    def start(t, carry):
        for k in range(TOP_K):
            copy(k, t).start(priority=k % 2)
        return carry

    lax.fori_loop(0, tokens, start, 0)

    def wait(t, carry):
        for k in range(TOP_K):
            copy(k, t).wait()
        return carry

    lax.fori_loop(0, tokens, wait, 0)


def _dispatch(hs, dest, n_rows):
    t = DISPATCH_TOKENS
    n = hs.shape[0] // TILE_ROWS
    return pl.pallas_call(
        _dispatch_kernel,
        grid=(n // t,),
        in_specs=[
            pl.BlockSpec((TOP_K, t), lambda i: (0, i), memory_space=pltpu.SMEM),
            pl.BlockSpec((t * TILE_ROWS, LANES), lambda i: (i, 0)),
        ],
        out_specs=pl.BlockSpec(memory_space=pl.ANY),
        out_shape=jax.ShapeDtypeStruct((n_rows * TILE_ROWS, LANES), U32),
        scratch_shapes=[pltpu.SemaphoreType.DMA],
        compiler_params=_cparams("arbitrary"),
        name="moe_dispatch",
    )(dest, hs)


def _expert_kernel(be_ref, bv_ref, xs_ref, wg_ref, wu_ref, wd_ref, y_ref):
    del be_ref
    valid = bv_ref[pl.program_id(0)]

    @pl.when(valid > 0)
    def _():
        x = _from_tiles(xs_ref, MOE_ROWS)
        row = lax.broadcasted_iota(jnp.int32, x.shape, 0)
        x = jnp.where(row < valid, x, 0.0)
        h = _silu(_bdot(x, wg_ref[0])) * _bdot(x, wu_ref[0])
        _to_tiles(y_ref, _bdot(h, wd_ref[0]))

    @pl.when(valid <= 0)
    def _():
        y_ref[...] = jnp.zeros_like(y_ref)


def _experts(xs, blk_expert, blk_valid, w_gate, w_up, w_down):
    nb = xs.shape[0] // (MOE_ROWS * TILE_ROWS)
    _, d, de = w_gate.shape
    blk = pl.BlockSpec((MOE_ROWS * TILE_ROWS, LANES), lambda i, be, bv: (i, 0))
    grid_spec = pltpu.PrefetchScalarGridSpec(
        num_scalar_prefetch=2,
        grid=(nb,),
        in_specs=[
            blk,
            pl.BlockSpec((1, d, de), lambda i, be, bv: (be[i], 0, 0)),
            pl.BlockSpec((1, d, de), lambda i, be, bv: (be[i], 0, 0)),
            pl.BlockSpec((1, de, d), lambda i, be, bv: (be[i], 0, 0)),
        ],
        out_specs=blk,
    )
    return pl.pallas_call(
        _expert_kernel,
        grid_spec=grid_spec,
        out_shape=jax.ShapeDtypeStruct(xs.shape, U32),
        compiler_params=_cparams("arbitrary"),
        name="moe_experts",
    )(blk_expert, blk_valid, xs, w_gate.astype(BF16), w_up.astype(BF16), w_down.astype(BF16))


COMBINE_TOKENS = 128


def _combine_kernel(dest_ref, y_ref, wtm_ref, hs_ref, x_ref, g2_ref, sg_ref, su_ref, sd_ref,
                    lng_ref, lnb_ref, o_ref, ybuf, sem):
    tokens = x_ref.shape[0]

    def copy(k, t):
        return _tile_copy(y_ref, dest_ref[k, t], ybuf.at[k], t * TILE_ROWS, sem)

    def start(t, carry):
        for k in range(TOP_K):
            copy(k, t).start(priority=k % 2)
        return carry

    lax.fori_loop(0, tokens, start, 0)
    hs = _from_tiles(hs_ref, tokens)
    shared = _bdot(_silu(_bdot(hs, sg_ref[...])) * _bdot(hs, su_ref[...]), sd_ref[...])

    def wait(t, carry):
        for k in range(TOP_K):
            copy(k, t).wait()
        return carry

    lax.fori_loop(0, tokens, wait, 0)
    wtm = wtm_ref[...]
    routed = _from_tiles(ybuf.at[0], tokens) * wtm[:, 0:1]
    for k in range(1, TOP_K):
        routed = routed + _from_tiles(ybuf.at[k], tokens) * wtm[:, k:k + 1]
    xr = ALPHA * x_ref[...] + g2_ref[0] * (routed + shared)
    o_ref[...] = _layer_norm_rows(xr, lng_ref[0:1, :], lnb_ref[0:1, :])


def _combine(y, dest, wtm, hs, x, mod, layer, n_prompt, dec_seq, s_gate, s_up, s_down, ln_g, ln_b, row0, n_rows):
    d = x.shape[1]
    t = COMBINE_TOKENS
    ds = s_gate.shape[1]
    b0 = row0 // t
    row = pl.BlockSpec((t, d), lambda i: (i, 0))
    return pl.pallas_call(
        _combine_kernel,
        grid=(n_rows // t,),
        in_specs=[
            pl.BlockSpec((TOP_K, t), lambda i: (0, b0 + i), memory_space=pltpu.SMEM),
            pl.BlockSpec(memory_space=pl.ANY),
            pl.BlockSpec((t, LANES), lambda i: (b0 + i, 0)),
            pl.BlockSpec((t * TILE_ROWS, LANES), lambda i: (b0 + i, 0)),
            pl.BlockSpec((t, d), lambda i: (b0 + i, 0)),
            _mod_spec(layer, 5, n_prompt, dec_seq, t, b0),
            pl.BlockSpec((d, ds), lambda i: (0, 0)),
            pl.BlockSpec((d, ds), lambda i: (0, 0)),
            pl.BlockSpec((ds, d), lambda i: (0, 0)),
            pl.BlockSpec((SUBLANES, d), lambda i: (0, 0)),
            pl.BlockSpec((SUBLANES, d), lambda i: (0, 0)),
        ],
        out_specs=row,
        out_shape=jax.ShapeDtypeStruct((n_rows, d), F32),
        scratch_shapes=[pltpu.VMEM((TOP_K, t * TILE_ROWS, LANES), U32), pltpu.SemaphoreType.DMA],
        compiler_params=_cparams("arbitrary"),
        name="moe_combine",
    )(dest, y, wtm, hs, x, mod, s_gate.astype(BF16), s_up.astype(BF16), s_down.astype(BF16),
      _pad_rows(ln_g), _pad_rows(ln_b))


def _moe(x, mod, layer, n_prompt, dec_seq, router, bias, w_gate, w_up, w_down, s_gate, s_up, s_down, ln_g, ln_b,
         split_streams=False):
    n, d = x.shape
    hs, idx, _, pos, wtm, cnt = _router(x, mod, layer, n_prompt, dec_seq, router, bias)
    counts = cnt[:, 0].astype(jnp.int32)
    padded = (counts + MOE_ROWS - 1) // MOE_ROWS * MOE_ROWS
    pad_end = jnp.cumsum(padded)
    pad_start = (pad_end - padded).astype(jnp.int32)
    nb = n * TOP_K // MOE_ROWS + N_EXPERTS
    blk_row = jnp.arange(nb, dtype=jnp.int32) * MOE_ROWS
    blk_expert = jnp.minimum(jnp.sum(pad_end[None, :] <= blk_row[:, None], axis=1), N_EXPERTS - 1).astype(jnp.int32)
    blk_valid = jnp.clip(pad_start[blk_expert] + counts[blk_expert] - blk_row, 0, MOE_ROWS).astype(jnp.int32)
    dest = _dest_rows(pad_start, idx, pos)
    xs = _dispatch(hs, dest, nb * MOE_ROWS)
    y = _experts(xs, blk_expert, blk_valid, w_gate, w_up, w_down)
    parts = ((0, n_prompt), (n_prompt, n - n_prompt)) if split_streams else ((0, n),)
    outs = [_combine(y, dest, wtm, hs, x, mod, layer, n_prompt, dec_seq, s_gate, s_up, s_down, ln_g, ln_b, r0, nr)
            for r0, nr in parts]
    return tuple(outs) if split_streams else outs[0]


def kernel(x_prompt, x_sample, state_la, cache_k, cache_v, c, c_ctx, w_mod, b_mod, ln_g, ln_b,
           la_w_in, la_conv, la_a_log, la_dt_bias, la_norm_g, la_w_out,
           swa_w_qkv, swa_sink, swa_w_out,
           moe_router, moe_bias, moe_w_gate, moe_w_up, moe_w_down, sh_w_gate, sh_w_up, sh_w_down):
    bp, tp, d = x_prompt.shape
    bs, ts, _ = x_sample.shape
    n_prompt = bp * tp
    assert bs + 1 <= COND_ROWS and d == D_MODEL
    assert n_prompt % ROW_BLOCK == 0 and ts % ROW_BLOCK == 0 and tp % DELTA_ROWS == 0

    cond = jnp.zeros((COND_ROWS, d), F32).at[0].set(c_ctx).at[1:1 + bs].set(c)
    mod = _modulation(cond, w_mod, b_mod)
    x_p = x_prompt.reshape(n_prompt, d)
    x_s = x_sample.reshape(bs * ts, d)

    def moe(x, i, split_streams=False):
        return _moe(x, mod, i, n_prompt, ts, moe_router[i], moe_bias[i], moe_w_gate[i], moe_w_up[i], moe_w_down[i],
                    sh_w_gate[i], sh_w_up[i], sh_w_down[i], ln_g[i, 1], ln_b[i, 1], split_streams)

    proj, gbc, gbr = _la_in_proj(x_p, x_s, mod, 0, ts, la_w_in[0], la_a_log[0], la_dt_bias[0])
    qkv_p = _conv_qkv(proj, la_conv[0], 0, bp, tp)
    qkv_s = _conv_qkv(proj, la_conv[0], n_prompt, bs, ts)
    opf, opb, s_fin = _delta_scan(_delta_prep(qkv_p, gbc, gbr, 0), None, bp, tp)
    s0 = state_la[:, 0].reshape(bs, N_DIRS * LA_HEADS, LA_DK, LA_DV)
    osf, osb, _ = _delta_scan(_delta_prep(qkv_s, gbc, gbr, n_prompt), s0, bs, ts)
    x = _la_out((opf, opb), (osf, osb), proj, x_p, x_s, mod, 0, ts, la_norm_g[0], la_w_out[0], ln_g[0, 0],
                ln_b[0, 0])
    new_la = s_fin.reshape(bp, 1, N_DIRS, LA_HEADS, LA_DK, LA_DV)
    x = moe(x, 0)

    qkv = _mod_proj(x, mod, 1, n_prompt, ts, swa_w_qkv[0], "swa_qkv_proj")
    qw = SWA_HEADS * SWA_DH
    new_k = qkv[:n_prompt, qw:qw + SWA_KVW].reshape(bp, 1, tp, SWA_KV_HEADS, SWA_DH)
    new_v = qkv[:n_prompt, qw + SWA_KVW:].reshape(bp, 1, tp, SWA_KV_HEADS, SWA_DH)
    a_p = _ctx_attention(qkv, swa_sink[0], bp, tp)
    qk_rope = _rope_qk(qkv, n_prompt, bs, ts)
    past = cache_k.shape[2]
    a_s = _lat_attention(qk_rope, qkv, cache_k[:, 0].reshape(bs * past, SWA_KVW),
                         cache_v[:, 0].reshape(bs * past, SWA_KVW), swa_sink[0], n_prompt, bs, ts)
    x = _swa_out(a_p, a_s, x, mod, 1, n_prompt, ts, swa_w_out[0], ln_g[1, 0], ln_b[1, 0])
    y_p, y_s = moe(x, 1, split_streams=True)
    return y_p.reshape(bp, tp, d), y_s.reshape(bs, ts, d), new_la, new_k, new_v
```

```python
import functools
import math

import jax
import jax.numpy as jnp
from jax import lax
from jax.experimental import pallas as pl
from jax.experimental.pallas import tpu as pltpu

F32 = jnp.float32
BF16 = jnp.bfloat16
HIGHEST = lax.Precision.HIGHEST

LANES = 128
SUBLANES = 8
VMEM_LIMIT_BYTES = 56 * 1024 * 1024

D_MODEL = 1024
DEPTH = 2
GRID_W = 64
LA_DK = 128
LA_DV = 128
LA_HEADS = D_MODEL // LA_DK
LA_QKV = LA_HEADS * (2 * LA_DK + LA_DV)
CONV_K = 5
CHUNK = 64
SUB = 16
SWA_DH = 64
SWA_HEADS = D_MODEL // SWA_DH
SWA_KV_HEADS = SWA_HEADS // 4
SWA_GROUP = SWA_HEADS // SWA_KV_HEADS
SWA_KVW = SWA_KV_HEADS * SWA_DH
WINDOW = 128
Q_BLOCK = 128
ROT_FREQS = SWA_DH // 4
ROPE_BASE = 10000.0
NEG_INF = -1e30
N_EXPERTS = 64
TOP_K = 8
N_GROUPS = 8
TOPK_GROUPS = 4
D_EXPERT = 256
D_SHARED = 256
ROUTED_SCALE = 2.5
ALPHA = (2 * DEPTH) ** 0.25
LN_EPS = 1e-5
RMS_EPS = 1e-6

ROW_BLOCK = 256
MOE_ROWS = 512
COND_ROWS = 8


def _cparams(*sem):
    return pltpu.CompilerParams(dimension_semantics=sem, vmem_limit_bytes=VMEM_LIMIT_BYTES)


def _bdot(a, b):
    return jnp.dot(a.astype(BF16), b.astype(BF16), preferred_element_type=F32)


def _bdot_nt(a, b):
    return lax.dot_general(a.astype(BF16), b.astype(BF16), (((1,), (1,)), ((), ())), preferred_element_type=F32)


def _bdot_tn(a, b):
    return lax.dot_general(a.astype(BF16), b.astype(BF16), (((0,), (0,)), ((), ())), preferred_element_type=F32)


def _silu(x):
    return x * jax.nn.sigmoid(x)


def _layer_norm_rows(x, g, b):
    mu = jnp.mean(x, axis=-1, keepdims=True)
    xc = x - mu
    var = jnp.mean(xc * xc, axis=-1, keepdims=True)
    return xc * lax.rsqrt(var + LN_EPS) * g + b


MOD_COLS = 512


def _mod_kernel(c_ref, w_ref, b_ref, o_ref):
    s = _silu(c_ref[...])
    o_ref[0] = jnp.dot(s, w_ref[0], precision=HIGHEST, preferred_element_type=F32) + b_ref[0]


def _modulation(cond, w_mod, b_mod):
    depth, d, n6 = w_mod.shape
    out = pl.pallas_call(
        _mod_kernel,
        grid=(depth, n6 // MOD_COLS),
        in_specs=[
            pl.BlockSpec((COND_ROWS, d), lambda l, j: (0, 0)),
            pl.BlockSpec((1, d, MOD_COLS), lambda l, j: (l, 0, j)),
            pl.BlockSpec((1, 1, MOD_COLS), lambda l, j: (l, 0, j)),
        ],
        out_specs=pl.BlockSpec((1, COND_ROWS, MOD_COLS), lambda l, j: (l, 0, j)),
        out_shape=jax.ShapeDtypeStruct((depth, COND_ROWS, n6), F32),
        compiler_params=_cparams("parallel", "parallel"),
        name="adaln_modulation",
    )(cond, w_mod, b_mod.reshape(depth, 1, n6))
    return out.reshape(depth * COND_ROWS * 6, 1, d)


def _mod_spec(layer, chunk, n_prompt, dec_seq, rows, blk0=0):
    def index(i, *_):
        tok = (blk0 + i) * rows
        row = jnp.where(tok < n_prompt, 0, 1 + (tok - n_prompt) // dec_seq)
        return ((layer * COND_ROWS + row) * 6 + chunk, 0, 0)

    return pl.BlockSpec((1, 1, D_MODEL), index)


def _la_in_kernel(xp_ref, xs_ref, sc_ref, sh_ref, w_ref, wab_ref, wabt_ref, pc_ref, pr_ref, o_ref, gbc_ref, gbr_ref,
                  *, npb):
    x = jnp.where(pl.program_id(0) < npb, xp_ref[...], xs_ref[...])
    xm = x * (1.0 + sc_ref[0]) + sh_ref[0]
    o_ref[...] = _bdot(xm, w_ref[...])
    ab = jnp.dot(xm, wab_ref[...], precision=HIGHEST, preferred_element_type=F32)
    lane = lax.broadcasted_iota(jnp.int32, ab.shape, 1)
    neg_a = pc_ref[0:1, :]
    dt_b = pc_ref[1:2, :]
    z = ab + dt_b
    softplus = jnp.maximum(z, 0.0) + jnp.log1p(jnp.exp(-jnp.abs(z)))
    gbc_ref[...] = jnp.where(lane < 2 * LA_HEADS, neg_a * softplus, jax.nn.sigmoid(ab))
    rows = xm.shape[0]
    for c in range(rows // CHUNK):
        xc = xm[c * CHUNK:(c + 1) * CHUNK, :]
        abt = lax.dot_general(wabt_ref[...], xc, (((1,), (1,)), ((), ())), precision=HIGHEST,
                              preferred_element_type=F32)
        zt = abt + pr_ref[:, 1:2]
        spt = jnp.maximum(zt, 0.0) + jnp.log1p(jnp.exp(-jnp.abs(zt)))
        gbr_ref[c] = pr_ref[:, 0:1] * spt


def _la_in_proj(x_p, x_s, mod, layer, dec_seq, w_in, a_log, dt_bias):
    n_prompt, d = x_p.shape
    n = n_prompt + x_s.shape[0]
    wide = LA_QKV + LA_HEADS * LA_DV
    w_main = w_in[:, :wide].astype(BF16)
    w_ab = w_in[:, wide:]
    n_ab = w_ab.shape[1]
    w_ab_pad = jnp.zeros((d, LANES), F32).at[:, :n_ab].set(w_ab)
    neg_a = -jnp.exp(a_log.astype(F32)).reshape(-1)
    dtb = dt_bias.astype(F32).reshape(-1)
    pc = jnp.zeros((SUBLANES, LANES), F32).at[0, :2 * LA_HEADS].set(neg_a).at[1, :2 * LA_HEADS].set(dtb)
    pr = jnp.zeros((n_ab, LANES), F32).at[:2 * LA_HEADS, 0].set(neg_a).at[:2 * LA_HEADS, 1].set(dtb)
    r = ROW_BLOCK
    return pl.pallas_call(
        functools.partial(_la_in_kernel, npb=n_prompt // r),
        grid=(n // r,),
        in_specs=_stream_specs(n_prompt, r, d) + [
            _mod_spec(layer, 1, n_prompt, dec_seq, r),
            _mod_spec(layer, 0, n_prompt, dec_seq, r),
            pl.BlockSpec((d, wide), lambda i: (0, 0)),
            pl.BlockSpec((d, LANES), lambda i: (0, 0)),
            pl.BlockSpec((n_ab, d), lambda i: (0, 0)),
            pl.BlockSpec((SUBLANES, LANES), lambda i: (0, 0)),
            pl.BlockSpec((n_ab, LANES), lambda i: (0, 0)),
        ],
        out_specs=[
            pl.BlockSpec((r, wide), lambda i: (i, 0)),
            pl.BlockSpec((r, LANES), lambda i: (i, 0)),
            pl.BlockSpec((r // CHUNK, n_ab, CHUNK), lambda i: (i, 0, 0)),
        ],
        out_shape=[
            jax.ShapeDtypeStruct((n, wide), F32),
            jax.ShapeDtypeStruct((n, LANES), F32),
            jax.ShapeDtypeStruct((n // CHUNK, n_ab, CHUNK), F32),
        ],
        compiler_params=_cparams("parallel"),
        name="deltanet_in_proj",
    )(x_p, x_s, mod, mod, w_main, w_ab_pad, w_ab.T, pc, pr)


def _proj_kernel(x_ref, sc_ref, sh_ref, w_ref, o_ref):
    xm = x_ref[...] * (1.0 + sc_ref[0]) + sh_ref[0]
    o_ref[...] = _bdot(xm, w_ref[...])


def _mod_proj(x, mod, layer, n_prompt, dec_seq, w, name):
    n, d = x.shape
    cols = w.shape[1]
    r = ROW_BLOCK
    return pl.pallas_call(
        _proj_kernel,
        grid=(n // r,),
        in_specs=[
            pl.BlockSpec((r, d), lambda i: (i, 0)),
            _mod_spec(layer, 1, n_prompt, dec_seq, r),
            _mod_spec(layer, 0, n_prompt, dec_seq, r),
            pl.BlockSpec((d, cols), lambda i: (0, 0)),
        ],
        out_specs=pl.BlockSpec((r, cols), lambda i: (i, 0)),
        out_shape=jax.ShapeDtypeStruct((n, cols), F32),
        compiler_params=_cparams("parallel"),
        name=name,
    )(x, mod, mod, w.astype(BF16))


CONV_PAD = SUBLANES
CONV_ROWS = 256


CONV_BLOCK_ELEMS = 4096 * LANES


def _conv_kernel(x_ref, w_ref, o_ref, pad_ref, *, seq, heads):
    zeros = jnp.zeros((CONV_PAD, heads * LANES), F32)
    pad_ref[0:CONV_PAD, :] = zeros
    pad_ref[CONV_PAD + seq:, :] = zeros
    pad_ref[CONV_PAD:CONV_PAD + seq, :] = x_ref[...]
    half = CONV_K // 2
    rows = min(CONV_ROWS, seq)
    for j in range(heads):
        c = pl.program_id(1) * heads + j
        lanes = slice(j * LANES, (j + 1) * LANES)
        is_qk = c < 2 * LA_HEADS
        scale = jnp.where(c < LA_HEADS, LA_DK ** -0.5, 1.0).astype(F32)
        for r0 in range(0, seq, rows):
            acc = jnp.zeros((rows, LANES), F32)
            for tap in range(CONV_K):
                start = r0 + CONV_PAD + tap - half
                acc = acc + w_ref[tap:tap + 1, lanes] * pad_ref[start:start + rows, lanes]
            y = _silu(acc)
            nrm = lax.rsqrt(jnp.sum(y * y, axis=-1, keepdims=True) + 1e-6) * scale
            o_ref[r0:r0 + rows, lanes] = y * jnp.where(is_qk, nrm, 1.0)


def _conv_qkv(proj, conv_w, row0, batch, seq):
    w = jnp.zeros((SUBLANES, LA_QKV), F32).at[:CONV_K].set(conv_w.astype(F32))
    blk0 = row0 // seq
    heads = max(1, min(LA_HEADS, CONV_BLOCK_ELEMS // (seq * LANES)))
    cw = heads * LANES
    return pl.pallas_call(
        functools.partial(_conv_kernel, seq=seq, heads=heads),
        grid=(batch, LA_QKV // cw),
        in_specs=[
            pl.BlockSpec((seq, cw), lambda b, c: (blk0 + b, c)),
            pl.BlockSpec((SUBLANES, cw), lambda b, c: (0, c)),
        ],
        out_specs=pl.BlockSpec((seq, cw), lambda b, c: (b, c)),
        out_shape=jax.ShapeDtypeStruct((batch * seq, LA_QKV), F32),
        scratch_shapes=[pltpu.VMEM((seq + 2 * CONV_PAD, cw), F32)],
        compiler_params=_cparams("parallel", "parallel"),
        name="deltanet_conv",
    )(proj, w)


DELTA_ROWS = 256
PAIR = 2 * CHUNK
N_DIRS = 2


PREP_HEADS = 2


def _pair_masks(rev):
    ii = lax.broadcasted_iota(jnp.int32, (PAIR, PAIR), 0)
    jj = lax.broadcasted_iota(jnp.int32, (PAIR, PAIR), 1)
    same_chunk = (ii // CHUNK) == (jj // CHUNK)
    same_sub = (ii // SUB) == (jj // SUB)
    if rev:
        return same_chunk, same_sub, same_chunk & (jj >= ii), same_chunk & (jj > ii), same_chunk & (ii >= jj)
    return same_chunk, same_sub, same_chunk & (jj <= ii), same_chunk & (jj < ii), same_chunk & (ii <= jj)


def _delta_prep_kernel(q_ref, k_ref, v_ref, gbc_ref, gbr_ref, u_ref, wq_ref, kd_ref, qk_ref, gl_ref):
    hp = pl.program_id(1)
    lane = lax.broadcasted_iota(jnp.int32, (PAIR, LANES), 1)
    cpp = PAIR // CHUNK
    masks = [_pair_masks(rev) for rev in (False, True)]
    probs = []
    for hh in range(PREP_HEADS):
        cols = slice(hh * LA_DK, (hh + 1) * LA_DK)
        for pr in range(DELTA_ROWS // PAIR):
            rows = slice(pr * PAIR, (pr + 1) * PAIR)
            q = q_ref[rows, cols]
            k = k_ref[rows, cols]
            v = v_ref[rows, cols]
            gbc = gbc_ref[rows, :]
            gram = _bdot_nt(k, k)
            qk_raw = _bdot_nt(q, k)
            for d in range(N_DIRS):
                same_chunk, same_sub, before, strict, before_t = masks[d]
                h = hp * PREP_HEADS + hh
                g_lane = d * LA_HEADS + h
                b_lane = (N_DIRS + d) * LA_HEADS + h
                g_col = jnp.sum(jnp.where(lane == g_lane, gbc, 0.0), axis=1, keepdims=True)
                b_col = jnp.sum(jnp.where(lane == b_lane, gbc, 0.0), axis=1, keepdims=True)
                g_row = jnp.concatenate([gbr_ref[pr * cpp + c, pl.ds(g_lane, 1), :] for c in range(cpp)], axis=1)
                gam_col = jnp.sum(jnp.where(before, g_row, 0.0), axis=1, keepdims=True)
                gam_row = jnp.sum(jnp.where(before_t, g_col, 0.0), axis=0, keepdims=True)
                tot_col = jnp.sum(jnp.where(same_chunk, g_row, 0.0), axis=1, keepdims=True)
                tot_row = jnp.sum(jnp.where(same_chunk, g_col, 0.0), axis=0, keepdims=True)
                dec = jnp.where(before, jnp.exp(jnp.where(before, gam_col - gam_row, 0.0)), 0.0)
                e_col = jnp.exp(gam_col)
                nm = jnp.where(strict, -(gram * b_col * dec), 0.0)
                nd = jnp.where(same_sub, nm, 0.0)
                probs.append(dict(
                    hh=hh, pr=pr, d=d, rows=rows, nd=nd, ne=nm - nd,
                    rhs=jnp.concatenate([v * b_col, k * (b_col * e_col)], axis=1),
                    qd=q * e_col, kd=k * jnp.exp(tot_col - gam_col), qkm=qk_raw * dec, gl=jnp.exp(tot_row)))
    for p in probs:
        p["yd"] = p["nd"]
        p["p"] = p["nd"]
    for _ in range(3):
        for p in probs:
            p["p"] = _bdot(p["p"], p["p"])
        for p in probs:
            p["yd"] = p["yd"] + p["p"] + _bdot(p["yd"], p["p"])
    for p in probs:
        p["f"] = p["ne"] + _bdot(p["yd"], p["ne"])
    for p in probs:
        p["f2"] = _bdot(p["f"], p["f"])
    for p in probs:
        p["gm"] = p["f"] + p["f2"] + _bdot(p["f"], p["f2"])
    for p in probs:
        p["yt"] = p["gm"] + p["yd"] + _bdot(p["gm"], p["yd"])
    for p in probs:
        p["sol"] = p["rhs"] + _bdot(p["yt"], p["rhs"])
    for p in probs:
        d, hh, pr, rows = p["d"], p["hh"], p["pr"], p["rows"]
        u = p["sol"][:, :LA_DV]
        w = p["sol"][:, LA_DV:]
        u_ref[d, hh, rows, :] = u
        kd_ref[d, hh, rows, :] = p["kd"].T.astype(BF16)
        qk_ref[d, hh, rows, :] = p["qkm"].astype(BF16)
        for c in range(cpp):
            cg = pr * cpp + c
            cr = slice(c * CHUNK, (c + 1) * CHUNK)
            wq_ref[d, hh, cg * PAIR:cg * PAIR + CHUNK, :] = w[cr, :].astype(BF16)
            wq_ref[d, hh, cg * PAIR + CHUNK:(cg + 1) * PAIR, :] = p["qd"][cr, :].astype(BF16)
            gl_ref[d, hh, cg] = p["gl"][:, c * CHUNK:c * CHUNK + 1] + jnp.zeros((1, LANES), F32)


def _delta_prep(qkv, gbc, gbr, row0):
    n = qkv.shape[0]
    r = DELTA_ROWS
    g0 = row0 // r
    n_ab = gbr.shape[1]
    cpb = r // CHUNK
    hg = LA_HEADS // PREP_HEADS
    hw = PREP_HEADS * LA_DK
    hm = lambda rows, dt: jax.ShapeDtypeStruct((N_DIRS, LA_HEADS, rows, LANES), dt)
    hspec = lambda rows: pl.BlockSpec((N_DIRS, PREP_HEADS, rows, LANES), lambda i, h: (0, h, i, 0))
    return pl.pallas_call(
        _delta_prep_kernel,
        grid=(n // r, hg),
        in_specs=[
            pl.BlockSpec((r, hw), lambda i, h: (i, h)),
            pl.BlockSpec((r, hw), lambda i, h: (i, hg + h)),
            pl.BlockSpec((r, hw), lambda i, h: (i, 2 * hg + h)),
            pl.BlockSpec((r, LANES), lambda i, h: (g0 + i, 0)),
            pl.BlockSpec((cpb, n_ab, CHUNK), lambda i, h: (g0 + i, 0, 0)),
        ],
        out_specs=[
            hspec(r), hspec(2 * r), hspec(r), hspec(r),
            pl.BlockSpec((N_DIRS, PREP_HEADS, cpb, 1, LANES), lambda i, h: (0, h, i, 0, 0)),
        ],
        out_shape=[
            hm(n, F32), hm(2 * n, BF16), hm(n, BF16), hm(n, BF16),
            jax.ShapeDtypeStruct((N_DIRS, LA_HEADS, n // CHUNK, 1, LANES), F32),
        ],
        compiler_params=_cparams("parallel", "parallel"),
        name="deltanet_prep",
    )(qkv, qkv, qkv, gbc, gbr)


def _delta_scan_kernel(*refs, has_s0):
    u_refs, wq_refs, kd_refs, qk_refs, gl_refs = (refs[0:2], refs[2:4], refs[4:6], refs[6:8], refs[8:10])
    if has_s0:
        s0_ref, of_ref, ob_ref, sfin_ref, s_ref = refs[10:]
    else:
        of_ref, ob_ref, sfin_ref, s_ref = refs[10:]
    o_refs = (of_ref, ob_ref)
    t = pl.program_id(1)

    @pl.when(t == 0)
    def _():
        if has_s0:
            s_ref[...] = s0_ref[0]
        else:
            s_ref[...] = jnp.zeros_like(s_ref)

    n_pairs = DELTA_ROWS // PAIR
    cpp = PAIR // CHUNK
    chains = [(d, h) for d in range(N_DIRS) for h in range(LA_HEADS)]
    zeros = jnp.zeros((CHUNK, LA_DV), F32)
    for step in range(n_pairs):
        pair_of = [n_pairs - 1 - step if d == 1 else step for d in range(N_DIRS)]
        s = [s_ref[d * LA_HEADS + h] for d, h in chains]
        v_new = [[None] * cpp for _ in chains]
        o_inter = [[None] * cpp for _ in chains]
        for j in range(cpp):
            chunk_of = [cpp - 1 - j if d == 1 else j for d in range(N_DIRS)]
            r = []
            for i, (d, h) in enumerate(chains):
                cg = pair_of[d] * cpp + chunk_of[d]
                r.append(jnp.dot(wq_refs[d][0, h, cg * PAIR:(cg + 1) * PAIR, :], s[i].astype(BF16),
                                 preferred_element_type=F32))
            for i, (d, h) in enumerate(chains):
                c = chunk_of[d]
                cg = pair_of[d] * cpp + c
                v_new[i][c] = u_refs[d][0, h, cg * CHUNK:(cg + 1) * CHUNK, :] - r[i][:CHUNK]
                o_inter[i][c] = r[i][CHUNK:]
                vz = jnp.concatenate([v_new[i][c] if m == c else zeros for m in range(cpp)], axis=0)
                kd_t = kd_refs[d][0, h, pair_of[d] * PAIR:(pair_of[d] + 1) * PAIR, :]
                s[i] = s[i] * gl_refs[d][0, h, cg] + jnp.dot(kd_t, vz.astype(BF16), preferred_element_type=F32)
        for i, (d, h) in enumerate(chains):
            s_ref[d * LA_HEADS + h] = s[i]
            rows = slice(pair_of[d] * PAIR, (pair_of[d] + 1) * PAIR)
            o_intra = jnp.dot(qk_refs[d][0, h, rows, :], jnp.concatenate(v_new[i], axis=0).astype(BF16),
                              preferred_element_type=F32)
            o_refs[d][h, rows, :] = jnp.concatenate(o_inter[i], axis=0) + o_intra

    @pl.when(t == pl.num_programs(1) - 1)
    def _():
        sfin_ref[0] = s_ref[...]


def _delta_scan(prep, s0, batch, seq):
    u, wq, kd, qk, gl = prep
    nt = seq // DELTA_ROWS
    r = DELTA_ROWS
    cpb = r // CHUNK
    n_chain = N_DIRS * LA_HEADS

    def tt(d, t):
        return nt - 1 - t if d == 1 else t

    def dspecs(rows):
        return [pl.BlockSpec((1, LA_HEADS, rows, LANES), functools.partial(
            lambda b, t, d: (d, 0, b * nt + tt(d, t), 0), d=d)) for d in range(N_DIRS)]

    gl_specs = [pl.BlockSpec((1, LA_HEADS, cpb, 1, LANES), functools.partial(
        lambda b, t, d: (d, 0, b * nt + tt(d, t), 0, 0), d=d)) for d in range(N_DIRS)]
    in_specs = dspecs(r) + dspecs(2 * r) + dspecs(r) + dspecs(r) + gl_specs
    args = [u, u, wq, wq, kd, kd, qk, qk, gl, gl]
    has_s0 = s0 is not None
    if has_s0:
        in_specs.append(pl.BlockSpec((1, n_chain, LA_DK, LA_DV), lambda b, t: (b, 0, 0, 0)))
        args.append(s0)
    o_shape = jax.ShapeDtypeStruct((LA_HEADS, batch * seq, LA_DV), F32)
    o_specs = [pl.BlockSpec((LA_HEADS, r, LANES), functools.partial(
        lambda b, t, d: (0, b * nt + tt(d, t), 0), d=d)) for d in range(N_DIRS)]
    return pl.pallas_call(
        functools.partial(_delta_scan_kernel, has_s0=has_s0),
        grid=(batch, nt),
        in_specs=in_specs,
        out_specs=o_specs + [pl.BlockSpec((1, n_chain, LA_DK, LA_DV), lambda b, t: (b, 0, 0, 0))],
        out_shape=[o_shape, o_shape, jax.ShapeDtypeStruct((batch, n_chain, LA_DK, LA_DV), F32)],
        scratch_shapes=[pltpu.VMEM((n_chain, LA_DK, LA_DV), F32)],
        compiler_params=_cparams("parallel", "arbitrary"),
        name="deltanet_scan",
    )(*args)


def _stream_specs(n_prompt, rows, cols):
    npb = n_prompt // rows
    return [
        pl.BlockSpec((rows, cols), lambda i: (jnp.minimum(i, npb - 1), 0)),
        pl.BlockSpec((rows, cols), lambda i: (jnp.maximum(i - npb, 0), 0)),
    ]


def _la_out_kernel(ofp_ref, obp_ref, ofs_ref, obs_ref, z_ref, xp_ref, xs_ref, g1_ref, ng_ref, w_ref, lng_ref,
                   lnb_ref, y_ref, *, npb):
    is_prompt = pl.program_id(0) < npb
    x = jnp.where(is_prompt, xp_ref[...], xs_ref[...])
    ng = ng_ref[0:1, :]
    parts = []
    for h in range(LA_HEADS):
        oh = jnp.where(is_prompt, ofp_ref[h] + obp_ref[h], ofs_ref[h] + obs_ref[h])
        r = lax.rsqrt(jnp.mean(oh * oh, axis=-1, keepdims=True) + RMS_EPS)
        parts.append(oh * r * ng)
    on = jnp.concatenate(parts, axis=1) * _silu(z_ref[...])
    out = _bdot(on, w_ref[...])
    xr = ALPHA * x + g1_ref[0] * out
    y_ref[...] = _layer_norm_rows(xr, lng_ref[0:1, :], lnb_ref[0:1, :])


def _pad_rows(v):
    return jnp.zeros((SUBLANES, v.shape[0]), F32).at[0].set(v.astype(F32))


def _la_out(o_p, o_s, proj, x_p, x_s, mod, layer, dec_seq, norm_g, w_out, ln_g, ln_b):
    n_prompt, d = x_p.shape
    n = n_prompt + x_s.shape[0]
    r = ROW_BLOCK
    zblk = LA_QKV // d
    npb = n_prompt // r
    sp = pl.BlockSpec((LA_HEADS, r, LA_DV), lambda i: (0, jnp.minimum(i, npb - 1), 0))
    ss = pl.BlockSpec((LA_HEADS, r, LA_DV), lambda i: (0, jnp.maximum(i - npb, 0), 0))
    return pl.pallas_call(
        functools.partial(_la_out_kernel, npb=n_prompt // r),
        grid=(n // r,),
        in_specs=[
            sp, sp, ss, ss,
            pl.BlockSpec((r, d), lambda i: (i, zblk)),
            *_stream_specs(n_prompt, r, d),
            _mod_spec(layer, 2, n_prompt, dec_seq, r),
            pl.BlockSpec((SUBLANES, LA_DV), lambda i: (0, 0)),
            pl.BlockSpec((d, d), lambda i: (0, 0)),
            pl.BlockSpec((SUBLANES, d), lambda i: (0, 0)),
            pl.BlockSpec((SUBLANES, d), lambda i: (0, 0)),
        ],
        out_specs=pl.BlockSpec((r, d), lambda i: (i, 0)),
        out_shape=jax.ShapeDtypeStruct((n, d), F32),
        compiler_params=_cparams("parallel"),
        name="deltanet_out",
    )(o_p[0], o_p[1], o_s[0], o_s[1], proj, x_p, x_s, mod, _pad_rows(norm_g), w_out.astype(BF16), _pad_rows(ln_g),
      _pad_rows(ln_b))


def _out_kernel(op_ref, os_ref, x_ref, g1_ref, w_ref, lng_ref, lnb_ref, y_ref, *, npb):
    is_prompt = pl.program_id(0) < npb
    o = jnp.where(is_prompt, op_ref[...], os_ref[...])
    out = _bdot(o, w_ref[...])
    xr = ALPHA * x_ref[...] + g1_ref[0] * out
    y_ref[...] = _layer_norm_rows(xr, lng_ref[0:1, :], lnb_ref[0:1, :])


def _swa_out(o_p, o_s, x, mod, layer, n_prompt, dec_seq, w_out, ln_g, ln_b):
    n, d = x.shape
    r = ROW_BLOCK
    sp, ss = _stream_specs(n_prompt, r, d)
    return pl.pallas_call(
        functools.partial(_out_kernel, npb=n_prompt // r),
        grid=(n // r,),
        in_specs=[
            sp, ss,
            pl.BlockSpec((r, d), lambda i: (i, 0)),
            _mod_spec(layer, 2, n_prompt, dec_seq, r),
            pl.BlockSpec((d, d), lambda i: (0, 0)),
            pl.BlockSpec((SUBLANES, d), lambda i: (0, 0)),
            pl.BlockSpec((SUBLANES, d), lambda i: (0, 0)),
        ],
        out_specs=pl.BlockSpec((r, d), lambda i: (i, 0)),
        out_shape=jax.ShapeDtypeStruct((n, d), F32),
        compiler_params=_cparams("parallel"),
        name="swa_out",
    )(o_p, o_s, x, mod, w_out.astype(BF16), _pad_rows(ln_g), _pad_rows(ln_b))


SWA_SCALE = SWA_DH ** -0.5


def _stack_group(q, kv):
    base = kv * SWA_GROUP * SWA_DH
    return jnp.concatenate([q[:, base + g * SWA_DH: base + (g + 1) * SWA_DH] for g in range(SWA_GROUP)], axis=0)


def _sink_col(sink_ref, kv, rows):
    return jnp.concatenate(
        [jnp.full((rows, 1), sink_ref[kv * SWA_GROUP + g], F32) for g in range(SWA_GROUP)], axis=0)


def _ctx_attn_kernel(sink_ref, q_ref, k_ref, v_ref, o_ref):
    q = q_ref[...]
    k = k_ref[...]
    v = v_ref[...]
    t = q.shape[0]
    outs = []
    for kv in range(SWA_KV_HEADS):
        kh = k[:, kv * SWA_DH:(kv + 1) * SWA_DH]
        vh = v[:, kv * SWA_DH:(kv + 1) * SWA_DH]
        s = _bdot_nt(_stack_group(q, kv), kh) * SWA_SCALE
        sink = _sink_col(sink_ref, kv, t)
        m = jnp.maximum(jnp.max(s, axis=1, keepdims=True), sink)
        p = jnp.exp(s - m)
        den = jnp.sum(p, axis=1, keepdims=True) + jnp.exp(sink - m)
        o = _bdot(p, vh) / den
        outs.extend(o[g * t:(g + 1) * t, :] for g in range(SWA_GROUP))
    o_ref[...] = jnp.concatenate(outs, axis=1)


def _ctx_attention(qkv, sink, batch, seq):
    kcol = SWA_HEADS * SWA_DH // SWA_KVW
    return pl.pallas_call(
        _ctx_attn_kernel,
        grid=(batch,),
        in_specs=[
            pl.BlockSpec(memory_space=pltpu.SMEM),
            pl.BlockSpec((seq, SWA_HEADS * SWA_DH), lambda b: (b, 0)),
            pl.BlockSpec((seq, SWA_KVW), lambda b: (b, kcol)),
            pl.BlockSpec((seq, SWA_KVW), lambda b: (b, kcol + 1)),
        ],
        out_specs=pl.BlockSpec((seq, SWA_HEADS * SWA_DH), lambda b: (b, 0)),
        out_shape=jax.ShapeDtypeStruct((batch * seq, SWA_HEADS * SWA_DH), F32),
        compiler_params=_cparams("parallel"),
        name="swa_context_attention",
    )(sink.astype(F32), qkv, qkv, qkv)


ROPE_COLS = SWA_HEADS * SWA_DH + SWA_KVW
ROPE_SHIFT = SWA_DH // 4


def _rope_tables(t_len):
    t = jnp.arange(t_len)
    r = (t // GRID_W).astype(F32)
    col = (t % GRID_W).astype(F32)
    inv = ROPE_BASE ** (-jnp.arange(ROT_FREQS, dtype=F32) / ROT_FREQS)
    ang_r = r[:, None] * inv
    ang_c = col[:, None] * inv
    ang = jnp.concatenate([ang_r, ang_r, ang_c, ang_c], -1)
    cos = jnp.cos(ang)
    sin = jnp.sin(ang)
    first = (jnp.arange(SWA_DH) % (2 * ROPE_SHIFT)) < ROPE_SHIFT
    sin_up = jnp.where(first, -sin, 0.0)
    sin_dn = jnp.where(first, 0.0, sin)
    reps = LANES // SWA_DH
    return tuple(jnp.tile(a, (1, reps)) for a in (cos, sin_up, sin_dn))


def _rope_kernel(x_ref, cos_ref, su_ref, sd_ref, o_ref):
    cos = cos_ref[...]
    su = su_ref[...]
    sd = sd_ref[...]
    for j in range(ROPE_COLS // LANES):
        x = x_ref[:, j * LANES:(j + 1) * LANES]
        up = pltpu.roll(x, LANES - ROPE_SHIFT, 1)
        dn = pltpu.roll(x, ROPE_SHIFT, 1)
        o_ref[:, j * LANES:(j + 1) * LANES] = x * cos + up * su + dn * sd


def _rope_qk(qkv, row0, batch, seq):
    r = ROW_BLOCK
    nb = seq // r
    blk0 = row0 // r
    tables = _rope_tables(seq)
    tspec = pl.BlockSpec((r, LANES), lambda b, i: (i, 0))
    return pl.pallas_call(
        _rope_kernel,
        grid=(batch, nb),
        in_specs=[pl.BlockSpec((r, ROPE_COLS), lambda b, i: (blk0 + b * nb + i, 0)), tspec, tspec, tspec],
        out_specs=pl.BlockSpec((r, ROPE_COLS), lambda b, i: (b * nb + i, 0)),
        out_shape=jax.ShapeDtypeStruct((batch * seq, ROPE_COLS), F32),
        compiler_params=_cparams("parallel", "parallel"),
        name="swa_rope",
    )(qkv, *tables)


def _lat_attn_kernel(sink_ref, q_ref, kp_ref, kc_ref, kn_ref, vp_ref, vc_ref, vn_ref, ck_ref, cv_ref, o_ref, *, seq):
    i = pl.program_id(1)
    q = q_ref[...] * SWA_SCALE
    kw = jnp.concatenate([kp_ref[...], kc_ref[...], kn_ref[...]], axis=0)
    vw = jnp.concatenate([vp_ref[...], vc_ref[...], vn_ref[...]], axis=0)
    ck = ck_ref[...]
    cv = cv_ref[...]
    span = 3 * Q_BLOCK
    rows = SWA_GROUP * Q_BLOCK
    q0 = i * Q_BLOCK
    qpos = q0 + lax.broadcasted_iota(jnp.int32, (rows, span), 0) % Q_BLOCK
    kpos = q0 - WINDOW + lax.broadcasted_iota(jnp.int32, (rows, span), 1)
    valid = (kpos >= 0) & (kpos < seq) & (jnp.abs(qpos - kpos) <= WINDOW)
    heads = range(SWA_KV_HEADS)
    cols = [slice(kv * SWA_DH, (kv + 1) * SWA_DH) for kv in heads]
    qg = [_stack_group(q, kv) for kv in heads]
    s_win = [_bdot_nt(qg[kv], kw[:, cols[kv]]) for kv in heads]
    s_ctx = [_bdot_nt(qg[kv], ck[:, cols[kv]]) for kv in heads]
    p_win, p_ctx, den = [], [], []
    for kv in heads:
        sw = jnp.where(valid, s_win[kv], NEG_INF)
        sc = s_ctx[kv]
        sink = _sink_col(sink_ref, kv, Q_BLOCK)
        m = jnp.maximum(jnp.maximum(jnp.max(sw, axis=1, keepdims=True), jnp.max(sc, axis=1, keepdims=True)), sink)
        p_win.append(jnp.exp(sw - m))
        p_ctx.append(jnp.exp(sc - m))
        den.append(jnp.sum(p_win[kv], axis=1, keepdims=True) + jnp.sum(p_ctx[kv], axis=1, keepdims=True)
                   + jnp.exp(sink - m))
    o_win = [_bdot(p_win[kv], vw[:, cols[kv]]) for kv in heads]
    o_ctx = [_bdot(p_ctx[kv], cv[:, cols[kv]]) for kv in heads]
    outs = []
    for kv in heads:
        o = (o_win[kv] + o_ctx[kv]) / den[kv]
        outs.extend(o[g * Q_BLOCK:(g + 1) * Q_BLOCK, :] for g in range(SWA_GROUP))
    o_ref[...] = jnp.concatenate(outs, axis=1)


def _lat_attention(qk_rope, qkv, cache_k, cache_v, sink, row0, batch, seq):
    nq = seq // Q_BLOCK
    blk0 = row0 // Q_BLOCK
    qw = SWA_HEADS * SWA_DH
    kcol = qw // SWA_KVW
    past = cache_k.shape[0] // batch

    def kspec(off):
        return pl.BlockSpec((Q_BLOCK, SWA_KVW), lambda b, i: (b * nq + jnp.clip(i + off, 0, nq - 1), kcol))

    def vspec(off):
        return pl.BlockSpec((Q_BLOCK, SWA_KVW),
                            lambda b, i: (blk0 + b * nq + jnp.clip(i + off, 0, nq - 1), kcol + 1))

    cspec = pl.BlockSpec((past, SWA_KVW), lambda b, i: (b, 0))
    return pl.pallas_call(
        functools.partial(_lat_attn_kernel, seq=seq),
        grid=(batch, nq),
        in_specs=[
            pl.BlockSpec(memory_space=pltpu.SMEM),
            pl.BlockSpec((Q_BLOCK, qw), lambda b, i: (b * nq + i, 0)),
            kspec(-1), kspec(0), kspec(1),
            vspec(-1), vspec(0), vspec(1),
            cspec, cspec,
        ],
        out_specs=pl.BlockSpec((Q_BLOCK, qw), lambda b, i: (b * nq + i, 0)),
        out_shape=jax.ShapeDtypeStruct((batch * seq, qw), F32),
        compiler_params=_cparams("parallel", "parallel"),
        name="swa_latent_attention",
    )(sink.astype(F32), qk_rope, qk_rope, qk_rope, qk_rope, qkv, qkv, qkv, cache_k, cache_v)


GROUP_SIZE = N_EXPERTS // N_GROUPS
U32 = jnp.uint32
HALF_D = D_MODEL // 2
TILE_ROWS = HALF_D // LANES


def _bf16_bits_high(x):
    return pltpu.bitcast(x.astype(BF16).astype(F32), U32)


def _to_tiles(ref, x):
    rows = x.shape[0]
    for s in range(TILE_ROWS):
        lo = _bf16_bits_high(x[:, s * LANES:(s + 1) * LANES]) >> 16
        hi = _bf16_bits_high(x[:, HALF_D + s * LANES:HALF_D + (s + 1) * LANES])
        ref[pl.ds(s, rows, stride=TILE_ROWS), :] = lo | hi


def _from_tiles(ref, rows):
    words = [ref[pl.ds(s, rows, stride=TILE_ROWS), :] for s in range(TILE_ROWS)]
    lo = [pltpu.bitcast(w << 16, F32) for w in words]
    hi = [pltpu.bitcast(w & jnp.uint32(0xFFFF0000), F32) for w in words]
    return jnp.concatenate(lo + hi, axis=1)


def _first_argmax(v, iota, axis, size):
    m = jnp.max(v, axis=axis, keepdims=True)
    idx = jnp.min(jnp.where(v == m, iota, size), axis=axis, keepdims=True)
    return m, idx


def _router_kernel(x_ref, sc_ref, sh_ref, rt_ref, bias_ref, hs_ref, idx_ref, w_ref, pos_ref, wtm_ref, cnt_ref):
    i = pl.program_id(0)
    rows = x_ref.shape[0]

    @pl.when(i == 0)
    def _():
        cnt_ref[...] = jnp.zeros_like(cnt_ref)

    hs = x_ref[...] * (1.0 + sc_ref[0]) + sh_ref[0]
    _to_tiles(hs_ref, hs)
    logits = lax.dot_general(rt_ref[...], hs, (((1,), (1,)), ((), ())), precision=HIGHEST,
                             preferred_element_type=F32)
    scores = jax.nn.sigmoid(logits)
    biased = scores + bias_ref[:, 0:1]
    b3 = biased.reshape(N_GROUPS, GROUP_SIZE, rows)
    mem = lax.broadcasted_iota(jnp.int32, b3.shape, 1)
    m1, i1 = _first_argmax(b3, mem, 1, GROUP_SIZE)
    m2 = jnp.max(jnp.where(mem == i1, -jnp.inf, b3), axis=1, keepdims=True)
    grp = (m1 + m2).reshape(N_GROUPS, rows)
    giota = lax.broadcasted_iota(jnp.int32, grp.shape, 0)
    gsel = jnp.zeros(grp.shape, jnp.bool_)
    for _ in range(TOPK_GROUPS):
        _, gi = _first_argmax(grp, giota, 0, N_GROUPS)
        hit = giota == gi
        gsel = gsel | hit
        grp = jnp.where(hit, -jnp.inf, grp)
    gmask = jnp.broadcast_to(gsel.reshape(N_GROUPS, 1, rows), b3.shape).reshape(N_EXPERTS, rows)
    sel = jnp.where(gmask, biased, -jnp.inf)
    eiota = lax.broadcasted_iota(jnp.int32, sel.shape, 0)
    chosen_any = jnp.zeros(sel.shape, jnp.bool_)
    idx_rows, w_rows, hits = [], [], []
    for _ in range(TOP_K):
        _, ei = _first_argmax(sel, eiota, 0, N_EXPERTS)
        hit = eiota == ei
        chosen_any = chosen_any | hit
        idx_rows.append(ei)
        w_rows.append(jnp.sum(jnp.where(hit, scores, 0.0), axis=0, keepdims=True))
        hits.append(hit)
        sel = jnp.where(hit, -jnp.inf, sel)
    wsum = w_rows[0]
    for w in w_rows[1:]:
        wsum = wsum + w
    w_rows = [w / wsum * ROUTED_SCALE for w in w_rows]
    onehot = chosen_any.astype(F32)
    s_i = lax.broadcasted_iota(jnp.int32, (rows, rows), 0)
    t_i = lax.broadcasted_iota(jnp.int32, (rows, rows), 1)
    earlier = (s_i < t_i).astype(F32)
    rank = cnt_ref[:, 0:1] + _bdot(onehot, earlier)
    cnt_ref[...] = cnt_ref[...] + jnp.sum(onehot, axis=1, keepdims=True)
    pos_rows = [jnp.sum(jnp.where(hit, rank, 0.0), axis=0, keepdims=True) for hit in hits]
    idx_ref[...] = jnp.concatenate(idx_rows, axis=0)
    w8 = jnp.concatenate(w_rows, axis=0)
    w_ref[...] = w8
    pos_ref[...] = jnp.concatenate(pos_rows, axis=0).astype(jnp.int32)
    wtm_ref[...] = jnp.concatenate([w8, jnp.zeros((LANES - TOP_K, rows), F32)], axis=0).T


def _router(x, mod, layer, n_prompt, dec_seq, router, bias):
    n, d = x.shape
    r = ROW_BLOCK
    bias_pad = jnp.zeros((N_EXPERTS, LANES), F32).at[:, 0].set(bias.astype(F32))
    kspec = pl.BlockSpec((TOP_K, r), lambda i: (0, i))
    return pl.pallas_call(
        _router_kernel,
        grid=(n // r,),
        in_specs=[
            pl.BlockSpec((r, d), lambda i: (i, 0)),
            _mod_spec(layer, 4, n_prompt, dec_seq, r),
            _mod_spec(layer, 3, n_prompt, dec_seq, r),
            pl.BlockSpec((N_EXPERTS, d), lambda i: (0, 0)),
            pl.BlockSpec((N_EXPERTS, LANES), lambda i: (0, 0)),
        ],
        out_specs=[
            pl.BlockSpec((r * TILE_ROWS, LANES), lambda i: (i, 0)),
            kspec, kspec, kspec,
            pl.BlockSpec((r, LANES), lambda i: (i, 0)),
            pl.BlockSpec((N_EXPERTS, LANES), lambda i: (0, 0)),
        ],
        out_shape=[
            jax.ShapeDtypeStruct((n * TILE_ROWS, LANES), U32),
            jax.ShapeDtypeStruct((TOP_K, n), jnp.int32),
            jax.ShapeDtypeStruct((TOP_K, n), F32),
            jax.ShapeDtypeStruct((TOP_K, n), jnp.int32),
            jax.ShapeDtypeStruct((n, LANES), F32),
            jax.ShapeDtypeStruct((N_EXPERTS, LANES), F32),
        ],
        compiler_params=_cparams("arbitrary"),
        name="moe_router",
    )(x, mod, mod, router.T.astype(F32), bias_pad)


DISPATCH_TOKENS = 256


def _dest_kernel(start_ref, idx_ref, pos_ref, o_ref):
    idx = idx_ref[...]
    start = jnp.zeros(idx.shape, jnp.int32)
    for e in range(N_EXPERTS):
        start = jnp.where(idx == e, start_ref[e], start)
    o_ref[...] = (start + pos_ref[...]) * TILE_ROWS


def _dest_rows(pad_start, idx, pos):
    k, n = idx.shape
    t = min(2048, n)
    spec = pl.BlockSpec((k, t), lambda i: (0, i))
    return pl.pallas_call(
        _dest_kernel,
        grid=(n // t,),
        in_specs=[pl.BlockSpec(memory_space=pltpu.SMEM), spec, spec],
        out_specs=spec,
        out_shape=jax.ShapeDtypeStruct((k, n), jnp.int32),
        compiler_params=_cparams("parallel"),
        name="moe_dest_rows",
    )(pad_start, idx, pos)


def _tile_copy(src_ref, src_row, dst_ref, dst_row, sem):
    return pltpu.make_async_copy(src_ref.at[pl.ds(pl.multiple_of(src_row, TILE_ROWS), TILE_ROWS), :],
                                 dst_ref.at[pl.ds(pl.multiple_of(dst_row, TILE_ROWS), TILE_ROWS), :], sem)


def _dispatch_kernel(dest_ref, hs_ref, xs_ref, sem):
    tokens = hs_ref.shape[0] // TILE_ROWS

    def copy(k, t):
        return _tile_copy(hs_ref, t * TILE_ROWS, xs_ref, dest_ref[k, t], sem)

    def start(t, carry):
        for k in range(TOP_K):
            copy(k, t).start(priority=k % 2)
        return carry

    lax.fori_loop(0, tokens, start, 0)

    def wait(t, carry):
        for k in range(TOP_K):
            copy(k, t).wait()
        return carry

    lax.fori_loop(0, tokens, wait, 0)


def _dispatch(hs, dest, n_rows):
    t = DISPATCH_TOKENS
    n = hs.shape[0] // TILE_ROWS
    return pl.pallas_call(
        _dispatch_kernel,
        grid=(n // t,),
        in_specs=[
            pl.BlockSpec((TOP_K, t), lambda i: (0, i), memory_space=pltpu.SMEM),
            pl.BlockSpec((t * TILE_ROWS, LANES), lambda i: (i, 0)),
        ],
        out_specs=pl.BlockSpec(memory_space=pl.ANY),
        out_shape=jax.ShapeDtypeStruct((n_rows * TILE_ROWS, LANES), U32),
        scratch_shapes=[pltpu.SemaphoreType.DMA],
        compiler_params=_cparams("arbitrary"),
        name="moe_dispatch",
    )(dest, hs)


def _expert_kernel(be_ref, bv_ref, xs_ref, wg_ref, wu_ref, wd_ref, y_ref):
    del be_ref
    valid = bv_ref[pl.program_id(0)]

    @pl.when(valid > 0)
    def _():
        x = _from_tiles(xs_ref, MOE_ROWS)
        row = lax.broadcasted_iota(jnp.int32, x.shape, 0)
        x = jnp.where(row < valid, x, 0.0)
        h = _silu(_bdot(x, wg_ref[0])) * _bdot(x, wu_ref[0])
        _to_tiles(y_ref, _bdot(h, wd_ref[0]))

    @pl.when(valid <= 0)
    def _():
        y_ref[...] = jnp.zeros_like(y_ref)


def _experts(xs, blk_expert, blk_valid, w_gate, w_up, w_down):
    nb = xs.shape[0] // (MOE_ROWS * TILE_ROWS)
    _, d, de = w_gate.shape
    blk = pl.BlockSpec((MOE_ROWS * TILE_ROWS, LANES), lambda i, be, bv: (i, 0))
    grid_spec = pltpu.PrefetchScalarGridSpec(
        num_scalar_prefetch=2,
        grid=(nb,),
        in_specs=[
            blk,
            pl.BlockSpec((1, d, de), lambda i, be, bv: (be[i], 0, 0)),
            pl.BlockSpec((1, d, de), lambda i, be, bv: (be[i], 0, 0)),
            pl.BlockSpec((1, de, d), lambda i, be, bv: (be[i], 0, 0)),
        ],
        out_specs=blk,
    )
    return pl.pallas_call(
        _expert_kernel,
        grid_spec=grid_spec,
        out_shape=jax.ShapeDtypeStruct(xs.shape, U32),
        compiler_params=_cparams("arbitrary"),
        name="moe_experts",
    )(blk_expert, blk_valid, xs, w_gate, w_up, w_down)


COMBINE_TOKENS = 256


def _combine_kernel(dest_ref, y_ref, wtm_ref, hs_ref, x_ref, g2_ref, sg_ref, su_ref, sd_ref,
                    lng_ref, lnb_ref, o_ref, ybuf, sem):
    tokens = x_ref.shape[0]

    def copy(k, t):
        return _tile_copy(y_ref, dest_ref[k, t], ybuf.at[k], t * TILE_ROWS, sem)

    def start(t, carry):
        for k in range(TOP_K):
            copy(k, t).start(priority=k % 2)
        return carry

    lax.fori_loop(0, tokens, start, 0)
    hs = _from_tiles(hs_ref, tokens)
    shared = _bdot(_silu(_bdot(hs, sg_ref[...])) * _bdot(hs, su_ref[...]), sd_ref[...])

    def wait(t, carry):
        for k in range(TOP_K):
            copy(k, t).wait()
        return carry

    lax.fori_loop(0, tokens, wait, 0)
    wtm = wtm_ref[...]
    routed = _from_tiles(ybuf.at[0], tokens) * wtm[:, 0:1]
    for k in range(1, TOP_K):
        routed = routed + _from_tiles(ybuf.at[k], tokens) * wtm[:, k:k + 1]
    xr = ALPHA * x_ref[...] + g2_ref[0] * (routed + shared)
    o_ref[...] = _layer_norm_rows(xr, lng_ref[0:1, :], lnb_ref[0:1, :])


def _combine(y, dest, wtm, hs, x, mod, layer, n_prompt, dec_seq, s_gate, s_up, s_down, ln_g, ln_b, row0, n_rows):
    d = x.shape[1]
    t = COMBINE_TOKENS
    ds = s_gate.shape[1]
    b0 = row0 // t
    row = pl.BlockSpec((t, d), lambda i: (i, 0))
    return pl.pallas_call(
        _combine_kernel,
        grid=(n_rows // t,),
        in_specs=[
            pl.BlockSpec((TOP_K, t), lambda i: (0, b0 + i), memory_space=pltpu.SMEM),
            pl.BlockSpec(memory_space=pl.ANY),
            pl.BlockSpec((t, LANES), lambda i: (b0 + i, 0)),
            pl.BlockSpec((t * TILE_ROWS, LANES), lambda i: (b0 + i, 0)),
            pl.BlockSpec((t, d), lambda i: (b0 + i, 0)),
            _mod_spec(layer, 5, n_prompt, dec_seq, t, b0),
            pl.BlockSpec((d, ds), lambda i: (0, 0)),
            pl.BlockSpec((d, ds), lambda i: (0, 0)),
            pl.BlockSpec((ds, d), lambda i: (0, 0)),
            pl.BlockSpec((SUBLANES, d), lambda i: (0, 0)),
            pl.BlockSpec((SUBLANES, d), lambda i: (0, 0)),
        ],
        out_specs=row,
        out_shape=jax.ShapeDtypeStruct((n_rows, d), F32),
        scratch_shapes=[pltpu.VMEM((TOP_K, t * TILE_ROWS, LANES), U32), pltpu.SemaphoreType.DMA],
        compiler_params=_cparams("arbitrary"),
        name="moe_combine",
    )(dest, y, wtm, hs, x, mod, s_gate.astype(BF16), s_up.astype(BF16), s_down.astype(BF16),
      _pad_rows(ln_g), _pad_rows(ln_b))


def _moe(x, mod, layer, n_prompt, dec_seq, router, bias, w_gate, w_up, w_down, s_gate, s_up, s_down, ln_g, ln_b,
         split_streams=False):
    n, d = x.shape
    hs, idx, _, pos, wtm, cnt = _router(x, mod, layer, n_prompt, dec_seq, router, bias)
    counts = cnt[:, 0].astype(jnp.int32)
    padded = (counts + MOE_ROWS - 1) // MOE_ROWS * MOE_ROWS
    pad_end = jnp.cumsum(padded)
    pad_start = (pad_end - padded).astype(jnp.int32)
    nb = n * TOP_K // MOE_ROWS + N_EXPERTS
    blk_row = jnp.arange(nb, dtype=jnp.int32) * MOE_ROWS
    blk_expert = jnp.minimum(jnp.sum(pad_end[None, :] <= blk_row[:, None], axis=1), N_EXPERTS - 1).astype(jnp.int32)
    blk_valid = jnp.clip(pad_start[blk_expert] + counts[blk_expert] - blk_row, 0, MOE_ROWS).astype(jnp.int32)
    dest = _dest_rows(pad_start, idx, pos)
    xs = _dispatch(hs, dest, nb * MOE_ROWS)
    y = _experts(xs, blk_expert, blk_valid, w_gate, w_up, w_down)
    parts = ((0, n_prompt), (n_prompt, n - n_prompt)) if split_streams else ((0, n),)
    outs = [_combine(y, dest, wtm, hs, x, mod, layer, n_prompt, dec_seq, s_gate, s_up, s_down, ln_g, ln_b, r0, nr)
            for r0, nr in parts]
    return tuple(outs) if split_streams else outs[0]


def kernel(x_prompt, x_sample, state_la, cache_k, cache_v, c, c_ctx, w_mod, b_mod, ln_g, ln_b,
           la_w_in, la_conv, la_a_log, la_dt_bias, la_norm_g, la_w_out,
           swa_w_qkv, swa_sink, swa_w_out,
           moe_router, moe_bias, moe_w_gate, moe_w_up, moe_w_down, sh_w_gate, sh_w_up, sh_w_down):
    bp, tp, d = x_prompt.shape
    bs, ts, _ = x_sample.shape
    n_prompt = bp * tp
    assert bs + 1 <= COND_ROWS and d == D_MODEL
    assert n_prompt % ROW_BLOCK == 0 and ts % ROW_BLOCK == 0 and tp % DELTA_ROWS == 0

    cond = jnp.zeros((COND_ROWS, d), F32).at[0].set(c_ctx).at[1:1 + bs].set(c)
    mod = _modulation(cond, w_mod, b_mod)
    x_p = x_prompt.reshape(n_prompt, d)
    x_s = x_sample.reshape(bs * ts, d)

    def moe(x, i, split_streams=False):
        return _moe(x, mod, i, n_prompt, ts, moe_router[i], moe_bias[i], moe_w_gate[i], moe_w_up[i], moe_w_down[i],
                    sh_w_gate[i], sh_w_up[i], sh_w_down[i], ln_g[i, 1], ln_b[i, 1], split_streams)

    proj, gbc, gbr = _la_in_proj(x_p, x_s, mod, 0, ts, la_w_in[0], la_a_log[0], la_dt_bias[0])
    qkv_p = _conv_qkv(proj, la_conv[0], 0, bp, tp)
    qkv_s = _conv_qkv(proj, la_conv[0], n_prompt, bs, ts)
    opf, opb, s_fin = _delta_scan(_delta_prep(qkv_p, gbc, gbr, 0), None, bp, tp)
    s0 = state_la[:, 0].reshape(bs, N_DIRS * LA_HEADS, LA_DK, LA_DV)
    osf, osb, _ = _delta_scan(_delta_prep(qkv_s, gbc, gbr, n_prompt), s0, bs, ts)
    x = _la_out((opf, opb), (osf, osb), proj, x_p, x_s, mod, 0, ts, la_norm_g[0], la_w_out[0], ln_g[0, 0],
                ln_b[0, 0])
    new_la = s_fin.reshape(bp, 1, N_DIRS, LA_HEADS, LA_DK, LA_DV)
    x = moe(x, 0)

    qkv = _mod_proj(x, mod, 1, n_prompt, ts, swa_w_qkv[0], "swa_qkv_proj")
    qw = SWA_HEADS * SWA_DH
    new_k = qkv[:n_prompt, qw:qw + SWA_KVW].reshape(bp, 1, tp, SWA_KV_HEADS, SWA_DH)
    new_v = qkv[:n_prompt, qw + SWA_KVW:].reshape(bp, 1, tp, SWA_KV_HEADS, SWA_DH)
    a_p = _ctx_attention(qkv, swa_sink[0], bp, tp)
    qk_rope = _rope_qk(qkv, n_prompt, bs, ts)
    past = cache_k.shape[2]
    a_s = _lat_attention(qk_rope, qkv, cache_k[:, 0].reshape(bs * past, SWA_KVW),
                         cache_v[:, 0].reshape(bs * past, SWA_KVW), swa_sink[0], n_prompt, bs, ts)
    x = _swa_out(a_p, a_s, x, mod, 1, n_prompt, ts, swa_w_out[0], ln_g[1, 0], ln_b[1, 0])
    y_p, y_s = moe(x, 1, split_streams=True)
    return y_p.reshape(bp, tp, d), y_s.reshape(bs, ts, d), new_la, new_k, new_v
```

```python
import functools
import math

import jax
import jax.numpy as jnp
from jax import lax
from jax.experimental import pallas as pl
from jax.experimental.pallas import tpu as pltpu

F32 = jnp.float32
BF16 = jnp.bfloat16
HIGHEST = lax.Precision.HIGHEST

LANES = 128
SUBLANES = 8
VMEM_LIMIT_BYTES = 56 * 1024 * 1024

D_MODEL = 1024
DEPTH = 2
GRID_W = 64
LA_DK = 128
LA_DV = 128
LA_HEADS = D_MODEL // LA_DK
LA_QKV = LA_HEADS * (2 * LA_DK + LA_DV)
CONV_K = 5
CHUNK = 64
SUB = 16
SWA_DH = 64
SWA_HEADS = D_MODEL // SWA_DH
SWA_KV_HEADS = SWA_HEADS // 4
SWA_GROUP = SWA_HEADS // SWA_KV_HEADS
SWA_KVW = SWA_KV_HEADS * SWA_DH
WINDOW = 128
Q_BLOCK = 128
ROT_FREQS = SWA_DH // 4
ROPE_BASE = 10000.0
NEG_INF = -1e30
N_EXPERTS = 64
TOP_K = 8
N_GROUPS = 8
TOPK_GROUPS = 4
D_EXPERT = 256
D_SHARED = 256
ROUTED_SCALE = 2.5
ALPHA = (2 * DEPTH) ** 0.25
LN_EPS = 1e-5
RMS_EPS = 1e-6

ROW_BLOCK = 256
MOE_ROWS = 512
COND_ROWS = 8


def _cparams(*sem):
    return pltpu.CompilerParams(dimension_semantics=sem, vmem_limit_bytes=VMEM_LIMIT_BYTES)


def _bdot(a, b):
    return jnp.dot(a.astype(BF16), b.astype(BF16), preferred_element_type=F32)


def _bdot_nt(a, b):
    return lax.dot_general(a.astype(BF16), b.astype(BF16), (((1,), (1,)), ((), ())), preferred_element_type=F32)


def _bdot_tn(a, b):
    return lax.dot_general(a.astype(BF16), b.astype(BF16), (((0,), (0,)), ((), ())), preferred_element_type=F32)


def _silu(x):
    return x * jax.nn.sigmoid(x)


def _layer_norm_rows(x, g, b):
    mu = jnp.mean(x, axis=-1, keepdims=True)
    xc = x - mu
    var = jnp.mean(xc * xc, axis=-1, keepdims=True)
    return xc * lax.rsqrt(var + LN_EPS) * g + b


MOD_COLS = 512


def _mod_kernel(c_ref, w_ref, b_ref, o_ref):
    s = _silu(c_ref[...])
    o_ref[0] = jnp.dot(s, w_ref[0], precision=HIGHEST, preferred_element_type=F32) + b_ref[0]


def _modulation(cond, w_mod, b_mod):
    depth, d, n6 = w_mod.shape
    out = pl.pallas_call(
        _mod_kernel,
        grid=(depth, n6 // MOD_COLS),
        in_specs=[
            pl.BlockSpec((COND_ROWS, d), lambda l, j: (0, 0)),
            pl.BlockSpec((1, d, MOD_COLS), lambda l, j: (l, 0, j)),
            pl.BlockSpec((1, 1, MOD_COLS), lambda l, j: (l, 0, j)),
        ],
        out_specs=pl.BlockSpec((1, COND_ROWS, MOD_COLS), lambda l, j: (l, 0, j)),
        out_shape=jax.ShapeDtypeStruct((depth, COND_ROWS, n6), F32),
        compiler_params=_cparams("parallel", "parallel"),
        name="adaln_modulation",
    )(cond, w_mod, b_mod.reshape(depth, 1, n6))
    return out.reshape(depth * COND_ROWS * 6, 1, d)


def _mod_spec(layer, chunk, n_prompt, dec_seq, rows, blk0=0):
    def index(i, *_):
        tok = (blk0 + i) * rows
        row = jnp.where(tok < n_prompt, 0, 1 + (tok - n_prompt) // dec_seq)
        return ((layer * COND_ROWS + row) * 6 + chunk, 0, 0)

    return pl.BlockSpec((1, 1, D_MODEL), index)


def _la_in_kernel(xp_ref, xs_ref, sc_ref, sh_ref, w_ref, wab_ref, wabt_ref, pc_ref, pr_ref, o_ref, gbc_ref, gbr_ref,
                  *, npb):
    x = jnp.where(pl.program_id(0) < npb, xp_ref[...], xs_ref[...])
    xm = x * (1.0 + sc_ref[0]) + sh_ref[0]
    o_ref[...] = _bdot(xm, w_ref[...])
    ab = jnp.dot(xm, wab_ref[...], precision=HIGHEST, preferred_element_type=F32)
    lane = lax.broadcasted_iota(jnp.int32, ab.shape, 1)
    neg_a = pc_ref[0:1, :]
    dt_b = pc_ref[1:2, :]
    z = ab + dt_b
    softplus = jnp.maximum(z, 0.0) + jnp.log1p(jnp.exp(-jnp.abs(z)))
    gbc_ref[...] = jnp.where(lane < 2 * LA_HEADS, neg_a * softplus, jax.nn.sigmoid(ab))
    rows = xm.shape[0]
    for c in range(rows // CHUNK):
        xc = xm[c * CHUNK:(c + 1) * CHUNK, :]
        abt = lax.dot_general(wabt_ref[...], xc, (((1,), (1,)), ((), ())), precision=HIGHEST,
                              preferred_element_type=F32)
        zt = abt + pr_ref[:, 1:2]
        spt = jnp.maximum(zt, 0.0) + jnp.log1p(jnp.exp(-jnp.abs(zt)))
        gbr_ref[c] = pr_ref[:, 0:1] * spt


def _la_in_proj(x_p, x_s, mod, layer, dec_seq, w_in, a_log, dt_bias):
    n_prompt, d = x_p.shape
    n = n_prompt + x_s.shape[0]
    wide = LA_QKV + LA_HEADS * LA_DV
    w_main = w_in[:, :wide].astype(BF16)
    w_ab = w_in[:, wide:]
    n_ab = w_ab.shape[1]
    w_ab_pad = jnp.zeros((d, LANES), F32).at[:, :n_ab].set(w_ab)
    neg_a = -jnp.exp(a_log.astype(F32)).reshape(-1)
    dtb = dt_bias.astype(F32).reshape(-1)
    pc = jnp.zeros((SUBLANES, LANES), F32).at[0, :2 * LA_HEADS].set(neg_a).at[1, :2 * LA_HEADS].set(dtb)
    pr = jnp.zeros((n_ab, LANES), F32).at[:2 * LA_HEADS, 0].set(neg_a).at[:2 * LA_HEADS, 1].set(dtb)
    r = ROW_BLOCK
    return pl.pallas_call(
        functools.partial(_la_in_kernel, npb=n_prompt // r),
        grid=(n // r,),
        in_specs=_stream_specs(n_prompt, r, d) + [
            _mod_spec(layer, 1, n_prompt, dec_seq, r),
            _mod_spec(layer, 0, n_prompt, dec_seq, r),
            pl.BlockSpec((d, wide), lambda i: (0, 0)),
            pl.BlockSpec((d, LANES), lambda i: (0, 0)),
            pl.BlockSpec((n_ab, d), lambda i: (0, 0)),
            pl.BlockSpec((SUBLANES, LANES), lambda i: (0, 0)),
            pl.BlockSpec((n_ab, LANES), lambda i: (0, 0)),
        ],
        out_specs=[
            pl.BlockSpec((r, wide), lambda i: (i, 0)),
            pl.BlockSpec((r, LANES), lambda i: (i, 0)),
            pl.BlockSpec((r // CHUNK, n_ab, CHUNK), lambda i: (i, 0, 0)),
        ],
        out_shape=[
            jax.ShapeDtypeStruct((n, wide), F32),
            jax.ShapeDtypeStruct((n, LANES), F32),
            jax.ShapeDtypeStruct((n // CHUNK, n_ab, CHUNK), F32),
        ],
        compiler_params=_cparams("parallel"),
        name="deltanet_in_proj",
    )(x_p, x_s, mod, mod, w_main, w_ab_pad, w_ab.T, pc, pr)


def _proj_kernel(x_ref, sc_ref, sh_ref, w_ref, o_ref):
    xm = x_ref[...] * (1.0 + sc_ref[0]) + sh_ref[0]
    o_ref[...] = _bdot(xm, w_ref[...])


def _mod_proj(x, mod, layer, n_prompt, dec_seq, w, name):
    n, d = x.shape
    cols = w.shape[1]
    r = ROW_BLOCK
    return pl.pallas_call(
        _proj_kernel,
        grid=(n // r,),
        in_specs=[
            pl.BlockSpec((r, d), lambda i: (i, 0)),
            _mod_spec(layer, 1, n_prompt, dec_seq, r),
            _mod_spec(layer, 0, n_prompt, dec_seq, r),
            pl.BlockSpec((d, cols), lambda i: (0, 0)),
        ],
        out_specs=pl.BlockSpec((r, cols), lambda i: (i, 0)),
        out_shape=jax.ShapeDtypeStruct((n, cols), F32),
        compiler_params=_cparams("parallel"),
        name=name,
    )(x, mod, mod, w.astype(BF16))


CONV_PAD = SUBLANES
CONV_ROWS = 256


CONV_BLOCK_ELEMS = 4096 * LANES


def _conv_kernel(x_ref, w_ref, o_ref, pad_ref, *, seq, heads):
    zeros = jnp.zeros((CONV_PAD, heads * LANES), F32)
    pad_ref[0:CONV_PAD, :] = zeros
    pad_ref[CONV_PAD + seq:, :] = zeros
    pad_ref[CONV_PAD:CONV_PAD + seq, :] = x_ref[...]
    half = CONV_K // 2
    rows = min(CONV_ROWS, seq)
    for j in range(heads):
        c = pl.program_id(1) * heads + j
        lanes = slice(j * LANES, (j + 1) * LANES)
        is_qk = c < 2 * LA_HEADS
        scale = jnp.where(c < LA_HEADS, LA_DK ** -0.5, 1.0).astype(F32)
        for r0 in range(0, seq, rows):
            acc = jnp.zeros((rows, LANES), F32)
            for tap in range(CONV_K):
                start = r0 + CONV_PAD + tap - half
                acc = acc + w_ref[tap:tap + 1, lanes] * pad_ref[start:start + rows, lanes]
            y = _silu(acc)
            nrm = lax.rsqrt(jnp.sum(y * y, axis=-1, keepdims=True) + 1e-6) * scale
            o_ref[r0:r0 + rows, lanes] = y * jnp.where(is_qk, nrm, 1.0)


def _conv_qkv(proj, conv_w, row0, batch, seq):
    w = jnp.zeros((SUBLANES, LA_QKV), F32).at[:CONV_K].set(conv_w.astype(F32))
    blk0 = row0 // seq
    heads = max(1, min(LA_HEADS, CONV_BLOCK_ELEMS // (seq * LANES)))
    cw = heads * LANES
    return pl.pallas_call(
        functools.partial(_conv_kernel, seq=seq, heads=heads),
        grid=(batch, LA_QKV // cw),
        in_specs=[
            pl.BlockSpec((seq, cw), lambda b, c: (blk0 + b, c)),
            pl.BlockSpec((SUBLANES, cw), lambda b, c: (0, c)),
        ],
        out_specs=pl.BlockSpec((seq, cw), lambda b, c: (b, c)),
        out_shape=jax.ShapeDtypeStruct((batch * seq, LA_QKV), F32),
        scratch_shapes=[pltpu.VMEM((seq + 2 * CONV_PAD, cw), F32)],
        compiler_params=_cparams("parallel", "parallel"),
        name="deltanet_conv",
    )(proj, w)


DELTA_ROWS = 256
PAIR = 2 * CHUNK
N_DIRS = 2


PREP_HEADS = 2


def _pair_masks(rev):
    ii = lax.broadcasted_iota(jnp.int32, (PAIR, PAIR), 0)
    jj = lax.broadcasted_iota(jnp.int32, (PAIR, PAIR), 1)
    same_chunk = (ii // CHUNK) == (jj // CHUNK)
    same_sub = (ii // SUB) == (jj // SUB)
    if rev:
        return same_chunk, same_sub, same_chunk & (jj >= ii), same_chunk & (jj > ii), same_chunk & (ii >= jj)
    return same_chunk, same_sub, same_chunk & (jj <= ii), same_chunk & (jj < ii), same_chunk & (ii <= jj)


def _delta_prep_kernel(q_ref, k_ref, v_ref, gbc_ref, gbr_ref, u_ref, wq_ref, kd_ref, qk_ref, gl_ref):
    hp = pl.program_id(1)
    lane = lax.broadcasted_iota(jnp.int32, (PAIR, LANES), 1)
    cpp = PAIR // CHUNK
    masks = [_pair_masks(rev) for rev in (False, True)]
    probs = []
    for hh in range(PREP_HEADS):
        cols = slice(hh * LA_DK, (hh + 1) * LA_DK)
        for pr in range(DELTA_ROWS // PAIR):
            rows = slice(pr * PAIR, (pr + 1) * PAIR)
            q = q_ref[rows, cols]
            k = k_ref[rows, cols]
            v = v_ref[rows, cols]
            gbc = gbc_ref[rows, :]
            gram = _bdot_nt(k, k)
            qk_raw = _bdot_nt(q, k)
            for d in range(N_DIRS):
                same_chunk, same_sub, before, strict, before_t = masks[d]
                h = hp * PREP_HEADS + hh
                g_lane = d * LA_HEADS + h
                b_lane = (N_DIRS + d) * LA_HEADS + h
                g_col = jnp.sum(jnp.where(lane == g_lane, gbc, 0.0), axis=1, keepdims=True)
                b_col = jnp.sum(jnp.where(lane == b_lane, gbc, 0.0), axis=1, keepdims=True)
                g_row = jnp.concatenate([gbr_ref[pr * cpp + c, pl.ds(g_lane, 1), :] for c in range(cpp)], axis=1)
                gam_col = jnp.sum(jnp.where(before, g_row, 0.0), axis=1, keepdims=True)
                gam_row = jnp.sum(jnp.where(before_t, g_col, 0.0), axis=0, keepdims=True)
                tot_col = jnp.sum(jnp.where(same_chunk, g_row, 0.0), axis=1, keepdims=True)
                tot_row = jnp.sum(jnp.where(same_chunk, g_col, 0.0), axis=0, keepdims=True)
                dec = jnp.where(before, jnp.exp(jnp.where(before, gam_col - gam_row, 0.0)), 0.0)
                e_col = jnp.exp(gam_col)
                nm = jnp.where(strict, -(gram * b_col * dec), 0.0)
                nd = jnp.where(same_sub, nm, 0.0)
                probs.append(dict(
                    hh=hh, pr=pr, d=d, rows=rows, nd=nd, ne=nm - nd,
                    rhs=jnp.concatenate([v * b_col, k * (b_col * e_col)], axis=1),
                    qd=q * e_col, kd=k * jnp.exp(tot_col - gam_col), qkm=qk_raw * dec, gl=jnp.exp(tot_row)))
    for p in probs:
        p["yd"] = p["nd"]
        p["p"] = p["nd"]
    for _ in range(3):
        for p in probs:
            p["p"] = _bdot(p["p"], p["p"])
        for p in probs:
            p["yd"] = p["yd"] + p["p"] + _bdot(p["yd"], p["p"])
    for p in probs:
        p["f"] = p["ne"] + _bdot(p["yd"], p["ne"])
    for p in probs:
        p["f2"] = _bdot(p["f"], p["f"])
    for p in probs:
        p["gm"] = p["f"] + p["f2"] + _bdot(p["f"], p["f2"])
    for p in probs:
        p["yt"] = p["gm"] + p["yd"] + _bdot(p["gm"], p["yd"])
    for p in probs:
        p["sol"] = p["rhs"] + _bdot(p["yt"], p["rhs"])
    for p in probs:
        d, hh, pr, rows = p["d"], p["hh"], p["pr"], p["rows"]
        u = p["sol"][:, :LA_DV]
        w = p["sol"][:, LA_DV:]
        u_ref[d, hh, rows, :] = u
        kd_ref[d, hh, rows, :] = p["kd"].T.astype(BF16)
        qk_ref[d, hh, rows, :] = p["qkm"].astype(BF16)
        for c in range(cpp):
            cg = pr * cpp + c
            cr = slice(c * CHUNK, (c + 1) * CHUNK)
            wq_ref[d, hh, cg * PAIR:cg * PAIR + CHUNK, :] = w[cr, :].astype(BF16)
            wq_ref[d, hh, cg * PAIR + CHUNK:(cg + 1) * PAIR, :] = p["qd"][cr, :].astype(BF16)
            gl_ref[d, hh, cg] = p["gl"][:, c * CHUNK:c * CHUNK + 1] + jnp.zeros((1, LANES), F32)


def _delta_prep(qkv, gbc, gbr, row0):
    n = qkv.shape[0]
    r = DELTA_ROWS
    g0 = row0 // r
    n_ab = gbr.shape[1]
    cpb = r // CHUNK
    hg = LA_HEADS // PREP_HEADS
    hw = PREP_HEADS * LA_DK
    hm = lambda rows, dt: jax.ShapeDtypeStruct((N_DIRS, LA_HEADS, rows, LANES), dt)
    hspec = lambda rows: pl.BlockSpec((N_DIRS, PREP_HEADS, rows, LANES), lambda i, h: (0, h, i, 0))
    return pl.pallas_call(
        _delta_prep_kernel,
        grid=(n // r, hg),
        in_specs=[
            pl.BlockSpec((r, hw), lambda i, h: (i, h)),
            pl.BlockSpec((r, hw), lambda i, h: (i, hg + h)),
            pl.BlockSpec((r, hw), lambda i, h: (i, 2 * hg + h)),
            pl.BlockSpec((r, LANES), lambda i, h: (g0 + i, 0)),
            pl.BlockSpec((cpb, n_ab, CHUNK), lambda i, h: (g0 + i, 0, 0)),
        ],
        out_specs=[
            hspec(r), hspec(2 * r), hspec(r), hspec(r),
            pl.BlockSpec((N_DIRS, PREP_HEADS, cpb, 1, LANES), lambda i, h: (0, h, i, 0, 0)),
        ],
        out_shape=[
            hm(n, F32), hm(2 * n, BF16), hm(n, BF16), hm(n, BF16),
            jax.ShapeDtypeStruct((N_DIRS, LA_HEADS, n // CHUNK, 1, LANES), F32),
        ],
        compiler_params=_cparams("parallel", "parallel"),
        name="deltanet_prep",
    )(qkv, qkv, qkv, gbc, gbr)


def _delta_scan_kernel(*refs, has_s0):
    u_refs, wq_refs, kd_refs, qk_refs, gl_refs = (refs[0:2], refs[2:4], refs[4:6], refs[6:8], refs[8:10])
    if has_s0:
        s0_ref, of_ref, ob_ref, sfin_ref, s_ref = refs[10:]
    else:
        of_ref, ob_ref, sfin_ref, s_ref = refs[10:]
    o_refs = (of_ref, ob_ref)
    t = pl.program_id(1)

    @pl.when(t == 0)
    def _():
        if has_s0:
            s_ref[...] = s0_ref[0]
        else:
            s_ref[...] = jnp.zeros_like(s_ref)

    n_pairs = DELTA_ROWS // PAIR
    cpp = PAIR // CHUNK
    chains = [(d, h) for d in range(N_DIRS) for h in range(LA_HEADS)]
    zeros = jnp.zeros((CHUNK, LA_DV), F32)
    for step in range(n_pairs):
        pair_of = [n_pairs - 1 - step if d == 1 else step for d in range(N_DIRS)]
        s = [s_ref[d * LA_HEADS + h] for d, h in chains]
        v_new = [[None] * cpp for _ in chains]
        o_inter = [[None] * cpp for _ in chains]
        for j in range(cpp):
            chunk_of = [cpp - 1 - j if d == 1 else j for d in range(N_DIRS)]
            r = []
            for i, (d, h) in enumerate(chains):
                cg = pair_of[d] * cpp + chunk_of[d]
                r.append(jnp.dot(wq_refs[d][0, h, cg * PAIR:(cg + 1) * PAIR, :], s[i].astype(BF16),
                                 preferred_element_type=F32))
            for i, (d, h) in enumerate(chains):
                c = chunk_of[d]
                cg = pair_of[d] * cpp + c
                v_new[i][c] = u_refs[d][0, h, cg * CHUNK:(cg + 1) * CHUNK, :] - r[i][:CHUNK]
                o_inter[i][c] = r[i][CHUNK:]
                vz = jnp.concatenate([v_new[i][c] if m == c else zeros for m in range(cpp)], axis=0)
                kd_t = kd_refs[d][0, h, pair_of[d] * PAIR:(pair_of[d] + 1) * PAIR, :]
                s[i] = s[i] * gl_refs[d][0, h, cg] + jnp.dot(kd_t, vz.astype(BF16), preferred_element_type=F32)
        for i, (d, h) in enumerate(chains):
            s_ref[d * LA_HEADS + h] = s[i]
            rows = slice(pair_of[d] * PAIR, (pair_of[d] + 1) * PAIR)
            o_intra = jnp.dot(qk_refs[d][0, h, rows, :], jnp.concatenate(v_new[i], axis=0).astype(BF16),
                              preferred_element_type=F32)
            o_refs[d][h, rows, :] = jnp.concatenate(o_inter[i], axis=0) + o_intra

    @pl.when(t == pl.num_programs(1) - 1)
    def _():
        sfin_ref[0] = s_ref[...]


def _delta_scan(prep, s0, batch, seq):
    u, wq, kd, qk, gl = prep
    nt = seq // DELTA_ROWS
    r = DELTA_ROWS
    cpb = r // CHUNK
    n_chain = N_DIRS * LA_HEADS

    def tt(d, t):
        return nt - 1 - t if d == 1 else t

    def dspecs(rows):
        return [pl.BlockSpec((1, LA_HEADS, rows, LANES), functools.partial(
            lambda b, t, d: (d, 0, b * nt + tt(d, t), 0), d=d)) for d in range(N_DIRS)]

    gl_specs = [pl.BlockSpec((1, LA_HEADS, cpb, 1, LANES), functools.partial(
        lambda b, t, d: (d, 0, b * nt + tt(d, t), 0, 0), d=d)) for d in range(N_DIRS)]
    in_specs = dspecs(r) + dspecs(2 * r) + dspecs(r) + dspecs(r) + gl_specs
    args = [u, u, wq, wq, kd, kd, qk, qk, gl, gl]
    has_s0 = s0 is not None
    if has_s0:
        in_specs.append(pl.BlockSpec((1, n_chain, LA_DK, LA_DV), lambda b, t: (b, 0, 0, 0)))
        args.append(s0)
    o_shape = jax.ShapeDtypeStruct((LA_HEADS, batch * seq, LA_DV), F32)
    o_specs = [pl.BlockSpec((LA_HEADS, r, LANES), functools.partial(
        lambda b, t, d: (0, b * nt + tt(d, t), 0), d=d)) for d in range(N_DIRS)]
    return pl.pallas_call(
        functools.partial(_delta_scan_kernel, has_s0=has_s0),
        grid=(batch, nt),
        in_specs=in_specs,
        out_specs=o_specs + [pl.BlockSpec((1, n_chain, LA_DK, LA_DV), lambda b, t: (b, 0, 0, 0))],
        out_shape=[o_shape, o_shape, jax.ShapeDtypeStruct((batch, n_chain, LA_DK, LA_DV), F32)],
        scratch_shapes=[pltpu.VMEM((n_chain, LA_DK, LA_DV), F32)],
        compiler_params=_cparams("parallel", "arbitrary"),
        name="deltanet_scan",
    )(*args)


def _stream_specs(n_prompt, rows, cols):
    npb = n_prompt // rows
    return [
        pl.BlockSpec((rows, cols), lambda i: (jnp.minimum(i, npb - 1), 0)),
        pl.BlockSpec((rows, cols), lambda i: (jnp.maximum(i - npb, 0), 0)),
    ]


def _la_out_kernel(ofp_ref, obp_ref, ofs_ref, obs_ref, z_ref, xp_ref, xs_ref, g1_ref, ng_ref, w_ref, lng_ref,
                   lnb_ref, y_ref, *, npb):
    is_prompt = pl.program_id(0) < npb
    x = jnp.where(is_prompt, xp_ref[...], xs_ref[...])
    ng = ng_ref[0:1, :]
    parts = []
    for h in range(LA_HEADS):
        oh = jnp.where(is_prompt, ofp_ref[h] + obp_ref[h], ofs_ref[h] + obs_ref[h])
        r = lax.rsqrt(jnp.mean(oh * oh, axis=-1, keepdims=True) + RMS_EPS)
        parts.append(oh * r * ng)
    on = jnp.concatenate(parts, axis=1) * _silu(z_ref[...])
    out = _bdot(on, w_ref[...])
    xr = ALPHA * x + g1_ref[0] * out
    y_ref[...] = _layer_norm_rows(xr, lng_ref[0:1, :], lnb_ref[0:1, :])


def _pad_rows(v):
    return jnp.zeros((SUBLANES, v.shape[0]), F32).at[0].set(v.astype(F32))


def _la_out(o_p, o_s, proj, x_p, x_s, mod, layer, dec_seq, norm_g, w_out, ln_g, ln_b):
    n_prompt, d = x_p.shape
    n = n_prompt + x_s.shape[0]
    r = ROW_BLOCK
    zblk = LA_QKV // d
    npb = n_prompt // r
    sp = pl.BlockSpec((LA_HEADS, r, LA_DV), lambda i: (0, jnp.minimum(i, npb - 1), 0))
    ss = pl.BlockSpec((LA_HEADS, r, LA_DV), lambda i: (0, jnp.maximum(i - npb, 0), 0))
    return pl.pallas_call(
        functools.partial(_la_out_kernel, npb=n_prompt // r),
        grid=(n // r,),
        in_specs=[
            sp, sp, ss, ss,
            pl.BlockSpec((r, d), lambda i: (i, zblk)),
            *_stream_specs(n_prompt, r, d),
            _mod_spec(layer, 2, n_prompt, dec_seq, r),
            pl.BlockSpec((SUBLANES, LA_DV), lambda i: (0, 0)),
            pl.BlockSpec((d, d), lambda i: (0, 0)),
            pl.BlockSpec((SUBLANES, d), lambda i: (0, 0)),
            pl.BlockSpec((SUBLANES, d), lambda i: (0, 0)),
        ],
        out_specs=pl.BlockSpec((r, d), lambda i: (i, 0)),
        out_shape=jax.ShapeDtypeStruct((n, d), F32),
        compiler_params=_cparams("parallel"),
        name="deltanet_out",
    )(o_p[0], o_p[1], o_s[0], o_s[1], proj, x_p, x_s, mod, _pad_rows(norm_g), w_out.astype(BF16), _pad_rows(ln_g),
      _pad_rows(ln_b))


def _out_kernel(op_ref, os_ref, x_ref, g1_ref, w_ref, lng_ref, lnb_ref, y_ref, *, npb):
    is_prompt = pl.program_id(0) < npb
    o = jnp.where(is_prompt, op_ref[...], os_ref[...])
    out = _bdot(o, w_ref[...])
    xr = ALPHA * x_ref[...] + g1_ref[0] * out
    y_ref[...] = _layer_norm_rows(xr, lng_ref[0:1, :], lnb_ref[0:1, :])


def _swa_out(o_p, o_s, x, mod, layer, n_prompt, dec_seq, w_out, ln_g, ln_b):
    n, d = x.shape
    r = ROW_BLOCK
    sp, ss = _stream_specs(n_prompt, r, d)
    return pl.pallas_call(
        functools.partial(_out_kernel, npb=n_prompt // r),
        grid=(n // r,),
        in_specs=[
            sp, ss,
            pl.BlockSpec((r, d), lambda i: (i, 0)),
            _mod_spec(layer, 2, n_prompt, dec_seq, r),
            pl.BlockSpec((d, d), lambda i: (0, 0)),
            pl.BlockSpec((SUBLANES, d), lambda i: (0, 0)),
            pl.BlockSpec((SUBLANES, d), lambda i: (0, 0)),
        ],
        out_specs=pl.BlockSpec((r, d), lambda i: (i, 0)),
        out_shape=jax.ShapeDtypeStruct((n, d), F32),
        compiler_params=_cparams("parallel"),
        name="swa_out",
    )(o_p, o_s, x, mod, w_out.astype(BF16), _pad_rows(ln_g), _pad_rows(ln_b))


SWA_SCALE = SWA_DH ** -0.5


def _stack_group(q, kv):
    base = kv * SWA_GROUP * SWA_DH
    return jnp.concatenate([q[:, base + g * SWA_DH: base + (g + 1) * SWA_DH] for g in range(SWA_GROUP)], axis=0)


def _sink_col(sink_ref, kv, rows):
    return jnp.concatenate(
        [jnp.full((rows, 1), sink_ref[kv * SWA_GROUP + g], F32) for g in range(SWA_GROUP)], axis=0)


def _ctx_attn_kernel(sink_ref, q_ref, k_ref, v_ref, o_ref):
    q = q_ref[...]
    k = k_ref[...]
    v = v_ref[...]
    t = q.shape[0]
    outs = []
    for kv in range(SWA_KV_HEADS):
        kh = k[:, kv * SWA_DH:(kv + 1) * SWA_DH]
        vh = v[:, kv * SWA_DH:(kv + 1) * SWA_DH]
        s = _bdot_nt(_stack_group(q, kv), kh) * SWA_SCALE
        sink = _sink_col(sink_ref, kv, t)
        m = jnp.maximum(jnp.max(s, axis=1, keepdims=True), sink)
        p = jnp.exp(s - m)
        den = jnp.sum(p, axis=1, keepdims=True) + jnp.exp(sink - m)
        o = _bdot(p, vh) / den
        outs.extend(o[g * t:(g + 1) * t, :] for g in range(SWA_GROUP))
    o_ref[...] = jnp.concatenate(outs, axis=1)


def _ctx_attention(qkv, sink, batch, seq):
    kcol = SWA_HEADS * SWA_DH // SWA_KVW
    return pl.pallas_call(
        _ctx_attn_kernel,
        grid=(batch,),
        in_specs=[
            pl.BlockSpec(memory_space=pltpu.SMEM),
            pl.BlockSpec((seq, SWA_HEADS * SWA_DH), lambda b: (b, 0)),
            pl.BlockSpec((seq, SWA_KVW), lambda b: (b, kcol)),
            pl.BlockSpec((seq, SWA_KVW), lambda b: (b, kcol + 1)),
        ],
        out_specs=pl.BlockSpec((seq, SWA_HEADS * SWA_DH), lambda b: (b, 0)),
        out_shape=jax.ShapeDtypeStruct((batch * seq, SWA_HEADS * SWA_DH), F32),
        compiler_params=_cparams("parallel"),
        name="swa_context_attention",
    )(sink.astype(F32), qkv, qkv, qkv)


ROPE_COLS = SWA_HEADS * SWA_DH + SWA_KVW
ROPE_SHIFT = SWA_DH // 4


def _rope_tables(t_len):
    t = jnp.arange(t_len)
    r = (t // GRID_W).astype(F32)
    col = (t % GRID_W).astype(F32)
    inv = ROPE_BASE ** (-jnp.arange(ROT_FREQS, dtype=F32) / ROT_FREQS)
    ang_r = r[:, None] * inv
    ang_c = col[:, None] * inv
    ang = jnp.concatenate([ang_r, ang_r, ang_c, ang_c], -1)
    cos = jnp.cos(ang)
    sin = jnp.sin(ang)
    first = (jnp.arange(SWA_DH) % (2 * ROPE_SHIFT)) < ROPE_SHIFT
    sin_up = jnp.where(first, -sin, 0.0)
    sin_dn = jnp.where(first, 0.0, sin)
    reps = LANES // SWA_DH
    return tuple(jnp.tile(a, (1, reps)) for a in (cos, sin_up, sin_dn))


def _rope_kernel(x_ref, cos_ref, su_ref, sd_ref, o_ref):
    cos = cos_ref[...]
    su = su_ref[...]
    sd = sd_ref[...]
    for j in range(ROPE_COLS // LANES):
        x = x_ref[:, j * LANES:(j + 1) * LANES]
        up = pltpu.roll(x, LANES - ROPE_SHIFT, 1)
        dn = pltpu.roll(x, ROPE_SHIFT, 1)
        o_ref[:, j * LANES:(j + 1) * LANES] = x * cos + up * su + dn * sd


def _rope_qk(qkv, row0, batch, seq):
    r = ROW_BLOCK
    nb = seq // r
    blk0 = row0 // r
    tables = _rope_tables(seq)
    tspec = pl.BlockSpec((r, LANES), lambda b, i: (i, 0))
    return pl.pallas_call(
        _rope_kernel,
        grid=(batch, nb),
        in_specs=[pl.BlockSpec((r, ROPE_COLS), lambda b, i: (blk0 + b * nb + i, 0)), tspec, tspec, tspec],
        out_specs=pl.BlockSpec((r, ROPE_COLS), lambda b, i: (b * nb + i, 0)),
        out_shape=jax.ShapeDtypeStruct((batch * seq, ROPE_COLS), F32),
        compiler_params=_cparams("parallel", "parallel"),
        name="swa_rope",
    )(qkv, *tables)


def _band_bias():
    r = jnp.arange(SWA_GROUP * Q_BLOCK)[:, None] % Q_BLOCK
    c = jnp.arange(3 * Q_BLOCK)[None, :] - Q_BLOCK
    return jnp.where(jnp.abs(r - c) <= WINDOW, 0.0, NEG_INF).astype(F32)


def _lat_attn_kernel(sink_ref, band_ref, q_ref, kp_ref, kc_ref, kn_ref, vp_ref, vc_ref, vn_ref, ck_ref, cv_ref,
                     o_ref):
    i = pl.program_id(1)
    q = q_ref[...] * SWA_SCALE
    kw = jnp.concatenate([kp_ref[...], kc_ref[...], kn_ref[...]], axis=0)
    vw = jnp.concatenate([vp_ref[...], vc_ref[...], vn_ref[...]], axis=0)
    ck = ck_ref[...]
    cv = cv_ref[...]
    col = lax.broadcasted_iota(jnp.int32, (1, 3 * Q_BLOCK), 1)
    prev_bias = jnp.where(i > 0, 0.0, NEG_INF).astype(F32)
    next_bias = jnp.where(i < pl.num_programs(1) - 1, 0.0, NEG_INF).astype(F32)
    bias = band_ref[...] + jnp.where(col < Q_BLOCK, prev_bias, jnp.where(col >= 2 * Q_BLOCK, next_bias, 0.0))
    heads = range(SWA_KV_HEADS)
    cols = [slice(kv * SWA_DH, (kv + 1) * SWA_DH) for kv in heads]
    qg = [_stack_group(q, kv) for kv in heads]
    s_win = [_bdot_nt(qg[kv], kw[:, cols[kv]]) for kv in heads]
    s_ctx = [_bdot_nt(qg[kv], ck[:, cols[kv]]) for kv in heads]
    p_win, p_ctx, den = [], [], []
    for kv in heads:
        sw = s_win[kv] + bias
        sc = s_ctx[kv]
        sink = _sink_col(sink_ref, kv, Q_BLOCK)
        m = jnp.maximum(jnp.maximum(jnp.max(sw, axis=1, keepdims=True), jnp.max(sc, axis=1, keepdims=True)), sink)
        p_win.append(jnp.exp(sw - m))
        p_ctx.append(jnp.exp(sc - m))
        den.append(jnp.sum(p_win[kv], axis=1, keepdims=True) + jnp.sum(p_ctx[kv], axis=1, keepdims=True)
                   + jnp.exp(sink - m))
    o_win = [_bdot(p_win[kv], vw[:, cols[kv]]) for kv in heads]
    o_ctx = [_bdot(p_ctx[kv], cv[:, cols[kv]]) for kv in heads]
    outs = []
    for kv in heads:
        o = (o_win[kv] + o_ctx[kv]) / den[kv]
        outs.extend(o[g * Q_BLOCK:(g + 1) * Q_BLOCK, :] for g in range(SWA_GROUP))
    o_ref[...] = jnp.concatenate(outs, axis=1)


def _lat_attention(qk_rope, qkv, cache_k, cache_v, sink, row0, batch, seq):
    assert WINDOW == Q_BLOCK and seq % Q_BLOCK == 0
    nq = seq // Q_BLOCK
    blk0 = row0 // Q_BLOCK
    qw = SWA_HEADS * SWA_DH
    kcol = qw // SWA_KVW
    past = cache_k.shape[0] // batch

    def kspec(off):
        return pl.BlockSpec((Q_BLOCK, SWA_KVW), lambda b, i: (b * nq + jnp.clip(i + off, 0, nq - 1), kcol))

    def vspec(off):
        return pl.BlockSpec((Q_BLOCK, SWA_KVW),
                            lambda b, i: (blk0 + b * nq + jnp.clip(i + off, 0, nq - 1), kcol + 1))

    cspec = pl.BlockSpec((past, SWA_KVW), lambda b, i: (b, 0))
    return pl.pallas_call(
        _lat_attn_kernel,
        grid=(batch, nq),
        in_specs=[
            pl.BlockSpec(memory_space=pltpu.SMEM),
            pl.BlockSpec((SWA_GROUP * Q_BLOCK, 3 * Q_BLOCK), lambda b, i: (0, 0)),
            pl.BlockSpec((Q_BLOCK, qw), lambda b, i: (b * nq + i, 0)),
            kspec(-1), kspec(0), kspec(1),
            vspec(-1), vspec(0), vspec(1),
            cspec, cspec,
        ],
        out_specs=pl.BlockSpec((Q_BLOCK, qw), lambda b, i: (b * nq + i, 0)),
        out_shape=jax.ShapeDtypeStruct((batch * seq, qw), F32),
        compiler_params=_cparams("parallel", "parallel"),
        name="swa_latent_attention",
    )(sink.astype(F32), _band_bias(), qk_rope, qk_rope, qk_rope, qk_rope, qkv, qkv, qkv, cache_k, cache_v)


GROUP_SIZE = N_EXPERTS // N_GROUPS
U32 = jnp.uint32
HALF_D = D_MODEL // 2
TILE_ROWS = HALF_D // LANES


def _bf16_bits_high(x):
    return pltpu.bitcast(x.astype(BF16).astype(F32), U32)


def _to_tiles(ref, x):
    rows = x.shape[0]
    for s in range(TILE_ROWS):
        lo = _bf16_bits_high(x[:, s * LANES:(s + 1) * LANES]) >> 16
        hi = _bf16_bits_high(x[:, HALF_D + s * LANES:HALF_D + (s + 1) * LANES])
        ref[pl.ds(s, rows, stride=TILE_ROWS), :] = lo | hi


def _from_tiles(ref, rows):
    words = [ref[pl.ds(s, rows, stride=TILE_ROWS), :] for s in range(TILE_ROWS)]
    lo = [pltpu.bitcast(w << 16, F32) for w in words]
    hi = [pltpu.bitcast(w & jnp.uint32(0xFFFF0000), F32) for w in words]
    return jnp.concatenate(lo + hi, axis=1)


def _first_argmax(v, iota, axis, size):
    m = jnp.max(v, axis=axis, keepdims=True)
    idx = jnp.min(jnp.where(v == m, iota, size), axis=axis, keepdims=True)
    return m, idx


def _router_kernel(x_ref, sc_ref, sh_ref, rt_ref, bias_ref, hs_ref, idx_ref, w_ref, pos_ref, wtm_ref, cnt_ref):
    i = pl.program_id(0)
    rows = x_ref.shape[0]

    @pl.when(i == 0)
    def _():
        cnt_ref[...] = jnp.zeros_like(cnt_ref)

    hs = x_ref[...] * (1.0 + sc_ref[0]) + sh_ref[0]
    _to_tiles(hs_ref, hs)
    logits = lax.dot_general(rt_ref[...], hs, (((1,), (1,)), ((), ())), precision=HIGHEST,
                             preferred_element_type=F32)
    scores = jax.nn.sigmoid(logits)
    biased = scores + bias_ref[:, 0:1]
    b3 = biased.reshape(N_GROUPS, GROUP_SIZE, rows)
    mem = lax.broadcasted_iota(jnp.int32, b3.shape, 1)
    m1, i1 = _first_argmax(b3, mem, 1, GROUP_SIZE)
    m2 = jnp.max(jnp.where(mem == i1, -jnp.inf, b3), axis=1, keepdims=True)
    grp = (m1 + m2).reshape(N_GROUPS, rows)
    giota = lax.broadcasted_iota(jnp.int32, grp.shape, 0)
    gsel = jnp.zeros(grp.shape, jnp.bool_)
    for _ in range(TOPK_GROUPS):
        _, gi = _first_argmax(grp, giota, 0, N_GROUPS)
        hit = giota == gi
        gsel = gsel | hit
        grp = jnp.where(hit, -jnp.inf, grp)
    gmask = jnp.broadcast_to(gsel.reshape(N_GROUPS, 1, rows), b3.shape).reshape(N_EXPERTS, rows)
    sel = jnp.where(gmask, biased, -jnp.inf)
    eiota = lax.broadcasted_iota(jnp.int32, sel.shape, 0)
    chosen_any = jnp.zeros(sel.shape, jnp.bool_)
    idx_rows, w_rows, hits = [], [], []
    for _ in range(TOP_K):
        _, ei = _first_argmax(sel, eiota, 0, N_EXPERTS)
        hit = eiota == ei
        chosen_any = chosen_any | hit
        idx_rows.append(ei)
        w_rows.append(jnp.sum(jnp.where(hit, scores, 0.0), axis=0, keepdims=True))
        hits.append(hit)
        sel = jnp.where(hit, -jnp.inf, sel)
    wsum = w_rows[0]
    for w in w_rows[1:]:
        wsum = wsum + w
    w_rows = [w / wsum * ROUTED_SCALE for w in w_rows]
    onehot = chosen_any.astype(F32)
    s_i = lax.broadcasted_iota(jnp.int32, (rows, rows), 0)
    t_i = lax.broadcasted_iota(jnp.int32, (rows, rows), 1)
    earlier = (s_i < t_i).astype(F32)
    rank = cnt_ref[:, 0:1] + _bdot(onehot, earlier)
    cnt_ref[...] = cnt_ref[...] + jnp.sum(onehot, axis=1, keepdims=True)
    pos_rows = [jnp.sum(jnp.where(hit, rank, 0.0), axis=0, keepdims=True) for hit in hits]
    idx_ref[...] = jnp.concatenate(idx_rows, axis=0)
    w8 = jnp.concatenate(w_rows, axis=0)
    w_ref[...] = w8
    pos_ref[...] = jnp.concatenate(pos_rows, axis=0).astype(jnp.int32)
    wtm_ref[...] = jnp.concatenate([w8, jnp.zeros((LANES - TOP_K, rows), F32)], axis=0).T


def _router(x, mod, layer, n_prompt, dec_seq, router, bias):
    n, d = x.shape
    r = ROW_BLOCK
    bias_pad = jnp.zeros((N_EXPERTS, LANES), F32).at[:, 0].set(bias.astype(F32))
    kspec = pl.BlockSpec((TOP_K, r), lambda i: (0, i))
    return pl.pallas_call(
        _router_kernel,
        grid=(n // r,),
        in_specs=[
            pl.BlockSpec((r, d), lambda i: (i, 0)),
            _mod_spec(layer, 4, n_prompt, dec_seq, r),
            _mod_spec(layer, 3, n_prompt, dec_seq, r),
            pl.BlockSpec((N_EXPERTS, d), lambda i: (0, 0)),
            pl.BlockSpec((N_EXPERTS, LANES), lambda i: (0, 0)),
        ],
        out_specs=[
            pl.BlockSpec((r * TILE_ROWS, LANES), lambda i: (i, 0)),
            kspec, kspec, kspec,
            pl.BlockSpec((r, LANES), lambda i: (i, 0)),
            pl.BlockSpec((N_EXPERTS, LANES), lambda i: (0, 0)),
        ],
        out_shape=[
            jax.ShapeDtypeStruct((n * TILE_ROWS, LANES), U32),
            jax.ShapeDtypeStruct((TOP_K, n), jnp.int32),
            jax.ShapeDtypeStruct((TOP_K, n), F32),
            jax.ShapeDtypeStruct((TOP_K, n), jnp.int32),
            jax.ShapeDtypeStruct((n, LANES), F32),
            jax.ShapeDtypeStruct((N_EXPERTS, LANES), F32),
        ],
        compiler_params=_cparams("arbitrary"),
        name="moe_router",
    )(x, mod, mod, router.T.astype(F32), bias_pad)


DISPATCH_TOKENS = 512


def _dest_kernel(start_ref, idx_ref, pos_ref, o_ref):
    idx = idx_ref[...]
    start = jnp.zeros(idx.shape, jnp.int32)
    for e in range(N_EXPERTS):
        start = jnp.where(idx == e, start_ref[e], start)
    o_ref[...] = (start + pos_ref[...]) * TILE_ROWS


def _dest_rows(pad_start, idx, pos):
    k, n = idx.shape
    t = min(2048, n)
    spec = pl.BlockSpec((k, t), lambda i: (0, i))
    return pl.pallas_call(
        _dest_kernel,
        grid=(n // t,),
        in_specs=[pl.BlockSpec(memory_space=pltpu.SMEM), spec, spec],
        out_specs=spec,
        out_shape=jax.ShapeDtypeStruct((k, n), jnp.int32),
        compiler_params=_cparams("parallel"),
        name="moe_dest_rows",
    )(pad_start, idx, pos)


def _tile_copy(src_ref, src_row, dst_ref, dst_row, sem):
    return pltpu.make_async_copy(src_ref.at[pl.ds(pl.multiple_of(src_row, TILE_ROWS), TILE_ROWS), :],
                                 dst_ref.at[pl.ds(pl.multiple_of(dst_row, TILE_ROWS), TILE_ROWS), :], sem)


def _dispatch_kernel(dest_ref, hs_ref, xs_ref, sem):
    tokens = hs_ref.shape[0] // TILE_ROWS

    def copy(k, t):
        return _tile_copy(hs_ref, t * TILE_ROWS, xs_ref, dest_ref[k, t], sem)

    def start(t, carry):
        for k in range(TOP_K):
            copy(k, t).start(priority=k % 2)
        return carry

    lax.fori_loop(0, tokens, start, 0)

    def wait(t, carry):
        for k in range(TOP_K):
            copy(k, t).wait()
        return carry

    lax.fori_loop(0, tokens, wait, 0)


def _dispatch(hs, dest, n_rows):
    t = DISPATCH_TOKENS
    n = hs.shape[0] // TILE_ROWS
    return pl.pallas_call(
        _dispatch_kernel,
        grid=(n // t,),
        in_specs=[
            pl.BlockSpec((TOP_K, t), lambda i: (0, i), memory_space=pltpu.SMEM),
            pl.BlockSpec((t * TILE_ROWS, LANES), lambda i: (i, 0)),
        ],
        out_specs=pl.BlockSpec(memory_space=pl.ANY),
        out_shape=jax.ShapeDtypeStruct((n_rows * TILE_ROWS, LANES), U32),
        scratch_shapes=[pltpu.SemaphoreType.DMA],
        compiler_params=_cparams("arbitrary"),
        name="moe_dispatch",
    )(dest, hs)


def _expert_kernel(be_ref, bv_ref, xs_ref, wg_ref, wu_ref, wd_ref, y_ref):
    del be_ref
    valid = bv_ref[pl.program_id(0)]

    @pl.when(valid > 0)
    def _():
        x = _from_tiles(xs_ref, MOE_ROWS)
        row = lax.broadcasted_iota(jnp.int32, x.shape, 0)
        x = jnp.where(row < valid, x, 0.0)
        h = _silu(_bdot(x, wg_ref[0])) * _bdot(x, wu_ref[0])
        _to_tiles(y_ref, _bdot(h, wd_ref[0]))

    @pl.when(valid <= 0)
    def _():
        y_ref[...] = jnp.zeros_like(y_ref)


def _experts(xs, blk_expert, blk_valid, w_gate, w_up, w_down, layer):
    nb = xs.shape[0] // (MOE_ROWS * TILE_ROWS)
    _, _, d, de = w_gate.shape
    blk = pl.BlockSpec((MOE_ROWS * TILE_ROWS, LANES), lambda i, be, bv: (i, 0))
    grid_spec = pltpu.PrefetchScalarGridSpec(
        num_scalar_prefetch=2,
        grid=(nb,),
        in_specs=[
            blk,
            pl.BlockSpec((None, 1, d, de), lambda i, be, bv: (layer, be[i], 0, 0)),
            pl.BlockSpec((None, 1, d, de), lambda i, be, bv: (layer, be[i], 0, 0)),
            pl.BlockSpec((None, 1, de, d), lambda i, be, bv: (layer, be[i], 0, 0)),
        ],
        out_specs=blk,
    )
    return pl.pallas_call(
        _expert_kernel,
        grid_spec=grid_spec,
        out_shape=jax.ShapeDtypeStruct(xs.shape, U32),
        compiler_params=_cparams("arbitrary"),
        name="moe_experts",
    )(blk_expert, blk_valid, xs, w_gate, w_up, w_down)


COMBINE_TOKENS = 256


def _combine_kernel(dest_ref, y_ref, wtm_ref, hs_ref, x_ref, g2_ref, sg_ref, su_ref, sd_ref,
                    lng_ref, lnb_ref, o_ref, ybuf, sem):
    tokens = x_ref.shape[0]

    def copy(k, t):
        return _tile_copy(y_ref, dest_ref[k, t], ybuf.at[k], t * TILE_ROWS, sem)

    def start(t, carry):
        for k in range(TOP_K):
            copy(k, t).start(priority=k % 2)
        return carry

    lax.fori_loop(0, tokens, start, 0)
    hs = _from_tiles(hs_ref, tokens)
    shared = _bdot(_silu(_bdot(hs, sg_ref[...])) * _bdot(hs, su_ref[...]), sd_ref[...])

    def wait(t, carry):
        for k in range(TOP_K):
            copy(k, t).wait()
        return carry

    lax.fori_loop(0, tokens, wait, 0)
    wtm = wtm_ref[...]
    routed = _from_tiles(ybuf.at[0], tokens) * wtm[:, 0:1]
    for k in range(1, TOP_K):
        routed = routed + _from_tiles(ybuf.at[k], tokens) * wtm[:, k:k + 1]
    xr = ALPHA * x_ref[...] + g2_ref[0] * (routed + shared)
    o_ref[...] = _layer_norm_rows(xr, lng_ref[0:1, :], lnb_ref[0:1, :])


def _combine(y, dest, wtm, hs, x, mod, layer, n_prompt, dec_seq, s_gate, s_up, s_down, ln_g, ln_b, row0, n_rows):
    d = x.shape[1]
    t = COMBINE_TOKENS
    ds = s_gate.shape[1]
    b0 = row0 // t
    row = pl.BlockSpec((t, d), lambda i: (i, 0))
    return pl.pallas_call(
        _combine_kernel,
        grid=(n_rows // t,),
        in_specs=[
            pl.BlockSpec((TOP_K, t), lambda i: (0, b0 + i), memory_space=pltpu.SMEM),
            pl.BlockSpec(memory_space=pl.ANY),
            pl.BlockSpec((t, LANES), lambda i: (b0 + i, 0)),
            pl.BlockSpec((t * TILE_ROWS, LANES), lambda i: (b0 + i, 0)),
            pl.BlockSpec((t, d), lambda i: (b0 + i, 0)),
            _mod_spec(layer, 5, n_prompt, dec_seq, t, b0),
            pl.BlockSpec((d, ds), lambda i: (0, 0)),
            pl.BlockSpec((d, ds), lambda i: (0, 0)),
            pl.BlockSpec((ds, d), lambda i: (0, 0)),
            pl.BlockSpec((SUBLANES, d), lambda i: (0, 0)),
            pl.BlockSpec((SUBLANES, d), lambda i: (0, 0)),
        ],
        out_specs=row,
        out_shape=jax.ShapeDtypeStruct((n_rows, d), F32),
        scratch_shapes=[pltpu.VMEM((TOP_K, t * TILE_ROWS, LANES), U32), pltpu.SemaphoreType.DMA],
        compiler_params=_cparams("arbitrary"),
        name="moe_combine",
    )(dest, y, wtm, hs, x, mod, s_gate.astype(BF16), s_up.astype(BF16), s_down.astype(BF16),
      _pad_rows(ln_g), _pad_rows(ln_b))


def _moe(x, mod, layer, n_prompt, dec_seq, router, bias, w_gate, w_up, w_down, s_gate, s_up, s_down, ln_g, ln_b,
         split_streams=False):
    n, d = x.shape
    hs, idx, _, pos, wtm, cnt = _router(x, mod, layer, n_prompt, dec_seq, router, bias)
    counts = cnt[:, 0].astype(jnp.int32)
    padded = (counts + MOE_ROWS - 1) // MOE_ROWS * MOE_ROWS
    pad_end = jnp.cumsum(padded)
    pad_start = (pad_end - padded).astype(jnp.int32)
    nb = n * TOP_K // MOE_ROWS + N_EXPERTS
    blk_row = jnp.arange(nb, dtype=jnp.int32) * MOE_ROWS
    blk_expert = jnp.minimum(jnp.sum(pad_end[None, :] <= blk_row[:, None], axis=1), N_EXPERTS - 1).astype(jnp.int32)
    blk_valid = jnp.clip(pad_start[blk_expert] + counts[blk_expert] - blk_row, 0, MOE_ROWS).astype(jnp.int32)
    dest = _dest_rows(pad_start, idx, pos)
    xs = _dispatch(hs, dest, nb * MOE_ROWS)
    y = _experts(xs, blk_expert, blk_valid, w_gate, w_up, w_down, layer)
    parts = ((0, n_prompt), (n_prompt, n - n_prompt)) if split_streams else ((0, n),)
    outs = [_combine(y, dest, wtm, hs, x, mod, layer, n_prompt, dec_seq, s_gate, s_up, s_down, ln_g, ln_b, r0, nr)
            for r0, nr in parts]
    return tuple(outs) if split_streams else outs[0]


def kernel(x_prompt, x_sample, state_la, cache_k, cache_v, c, c_ctx, w_mod, b_mod, ln_g, ln_b,
           la_w_in, la_conv, la_a_log, la_dt_bias, la_norm_g, la_w_out,
           swa_w_qkv, swa_sink, swa_w_out,
           moe_router, moe_bias, moe_w_gate, moe_w_up, moe_w_down, sh_w_gate, sh_w_up, sh_w_down):
    bp, tp, d = x_prompt.shape
    bs, ts, _ = x_sample.shape
    n_prompt = bp * tp
    assert bs + 1 <= COND_ROWS and d == D_MODEL
    assert n_prompt % ROW_BLOCK == 0 and ts % ROW_BLOCK == 0 and tp % DELTA_ROWS == 0

    cond = jnp.zeros((COND_ROWS, d), F32).at[0].set(c_ctx).at[1:1 + bs].set(c)
    mod = _modulation(cond, w_mod, b_mod)
    x_p = x_prompt.reshape(n_prompt, d)
    x_s = x_sample.reshape(bs * ts, d)

    def moe(x, i, split_streams=False):
        return _moe(x, mod, i, n_prompt, ts, moe_router[i], moe_bias[i], moe_w_gate, moe_w_up, moe_w_down,
                    sh_w_gate[i], sh_w_up[i], sh_w_down[i], ln_g[i, 1], ln_b[i, 1], split_streams)

    proj, gbc, gbr = _la_in_proj(x_p, x_s, mod, 0, ts, la_w_in[0], la_a_log[0], la_dt_bias[0])
    qkv_p = _conv_qkv(proj, la_conv[0], 0, bp, tp)
    qkv_s = _conv_qkv(proj, la_conv[0], n_prompt, bs, ts)
    opf, opb, s_fin = _delta_scan(_delta_prep(qkv_p, gbc, gbr, 0), None, bp, tp)
    s0 = state_la[:, 0].reshape(bs, N_DIRS * LA_HEADS, LA_DK, LA_DV)
    osf, osb, _ = _delta_scan(_delta_prep(qkv_s, gbc, gbr, n_prompt), s0, bs, ts)
    x = _la_out((opf, opb), (osf, osb), proj, x_p, x_s, mod, 0, ts, la_norm_g[0], la_w_out[0], ln_g[0, 0],
                ln_b[0, 0])
    new_la = s_fin.reshape(bp, 1, N_DIRS, LA_HEADS, LA_DK, LA_DV)
    x = moe(x, 0)

    qkv = _mod_proj(x, mod, 1, n_prompt, ts, swa_w_qkv[0], "swa_qkv_proj")
    qw = SWA_HEADS * SWA_DH
    new_k = qkv[:n_prompt, qw:qw + SWA_KVW].reshape(bp, 1, tp, SWA_KV_HEADS, SWA_DH)
    new_v = qkv[:n_prompt, qw + SWA_KVW:].reshape(bp, 1, tp, SWA_KV_HEADS, SWA_DH)
    a_p = _ctx_attention(qkv, swa_sink[0], bp, tp)
    qk_rope = _rope_qk(qkv, n_prompt, bs, ts)
    past = cache_k.shape[2]
    a_s = _lat_attention(qk_rope, qkv, cache_k[:, 0].reshape(bs * past, SWA_KVW),
                         cache_v[:, 0].reshape(bs * past, SWA_KVW), swa_sink[0], n_prompt, bs, ts)
    x = _swa_out(a_p, a_s, x, mod, 1, n_prompt, ts, swa_w_out[0], ln_g[1, 0], ln_b[1, 0])
    y_p, y_s = moe(x, 1, split_streams=True)
    return y_p.reshape(bp, tp, d), y_s.reshape(bs, ts, d), new_la, new_k, new_v
```

```python
import functools
import math

import jax
import jax.numpy as jnp
from jax import lax
from jax.experimental import pallas as pl
from jax.experimental.pallas import tpu as pltpu

F32 = jnp.float32
BF16 = jnp.bfloat16
HIGHEST = lax.Precision.HIGHEST

LANES = 128
SUBLANES = 8
VMEM_LIMIT_BYTES = 56 * 1024 * 1024

D_MODEL = 1024
DEPTH = 2
GRID_W = 64
LA_DK = 128
LA_DV = 128
LA_HEADS = D_MODEL // LA_DK
LA_QKV = LA_HEADS * (2 * LA_DK + LA_DV)
CONV_K = 5
CHUNK = 64
SUB = 16
SWA_DH = 64
SWA_HEADS = D_MODEL // SWA_DH
SWA_KV_HEADS = SWA_HEADS // 4
SWA_GROUP = SWA_HEADS // SWA_KV_HEADS
SWA_KVW = SWA_KV_HEADS * SWA_DH
WINDOW = 128
Q_BLOCK = 128
ROT_FREQS = SWA_DH // 4
ROPE_BASE = 10000.0
NEG_INF = -1e30
N_EXPERTS = 64
TOP_K = 8
N_GROUPS = 8
TOPK_GROUPS = 4
D_EXPERT = 256
D_SHARED = 256
ROUTED_SCALE = 2.5
ALPHA = (2 * DEPTH) ** 0.25
LN_EPS = 1e-5
RMS_EPS = 1e-6

ROW_BLOCK = 256
MOE_ROWS = 512
COND_ROWS = 8


def _cparams(*sem):
    return pltpu.CompilerParams(dimension_semantics=sem, vmem_limit_bytes=VMEM_LIMIT_BYTES)


def _bdot(a, b):
    return jnp.dot(a.astype(BF16), b.astype(BF16), preferred_element_type=F32)


def _bdot_nt(a, b):
    return lax.dot_general(a.astype(BF16), b.astype(BF16), (((1,), (1,)), ((), ())), preferred_element_type=F32)


def _bdot_tn(a, b):
    return lax.dot_general(a.astype(BF16), b.astype(BF16), (((0,), (0,)), ((), ())), preferred_element_type=F32)


def _silu(x):
    return x * jax.nn.sigmoid(x)


def _layer_norm_rows(x, g, b):
    mu = jnp.mean(x, axis=-1, keepdims=True)
    xc = x - mu
    var = jnp.mean(xc * xc, axis=-1, keepdims=True)
    return xc * lax.rsqrt(var + LN_EPS) * g + b


MOD_COLS = 512


def _mod_kernel(c_ref, w_ref, b_ref, o_ref):
    s = _silu(c_ref[...])
    o_ref[0] = jnp.dot(s, w_ref[0], precision=HIGHEST, preferred_element_type=F32) + b_ref[0]


def _modulation(cond, w_mod, b_mod):
    depth, d, n6 = w_mod.shape
    out = pl.pallas_call(
        _mod_kernel,
        grid=(depth, n6 // MOD_COLS),
        in_specs=[
            pl.BlockSpec((COND_ROWS, d), lambda l, j: (0, 0)),
            pl.BlockSpec((1, d, MOD_COLS), lambda l, j: (l, 0, j)),
            pl.BlockSpec((1, 1, MOD_COLS), lambda l, j: (l, 0, j)),
        ],
        out_specs=pl.BlockSpec((1, COND_ROWS, MOD_COLS), lambda l, j: (l, 0, j)),
        out_shape=jax.ShapeDtypeStruct((depth, COND_ROWS, n6), F32),
        compiler_params=_cparams("parallel", "parallel"),
        name="adaln_modulation",
    )(cond, w_mod, b_mod.reshape(depth, 1, n6))
    return out.reshape(depth * COND_ROWS * 6, 1, d)


def _mod_spec(layer, chunk, n_prompt, dec_seq, rows, blk0=0):
    def index(i, *_):
        tok = (blk0 + i) * rows
        row = jnp.where(tok < n_prompt, 0, 1 + (tok - n_prompt) // dec_seq)
        return ((layer * COND_ROWS + row) * 6 + chunk, 0, 0)

    return pl.BlockSpec((1, 1, D_MODEL), index)


def _la_in_kernel(xp_ref, xs_ref, sc_ref, sh_ref, w_ref, wab_ref, wabt_ref, pc_ref, pr_ref, o_ref, gbc_ref, gbr_ref,
                  *, npb):
    x = jnp.where(pl.program_id(0) < npb, xp_ref[...], xs_ref[...])
    xm = x * (1.0 + sc_ref[0]) + sh_ref[0]
    o_ref[...] = _bdot(xm, w_ref[...])
    ab = jnp.dot(xm, wab_ref[...], precision=HIGHEST, preferred_element_type=F32)
    lane = lax.broadcasted_iota(jnp.int32, ab.shape, 1)
    neg_a = pc_ref[0:1, :]
    dt_b = pc_ref[1:2, :]
    z = ab + dt_b
    softplus = jnp.maximum(z, 0.0) + jnp.log1p(jnp.exp(-jnp.abs(z)))
    gbc_ref[...] = jnp.where(lane < 2 * LA_HEADS, neg_a * softplus, jax.nn.sigmoid(ab))
    rows = xm.shape[0]
    for c in range(rows // CHUNK):
        xc = xm[c * CHUNK:(c + 1) * CHUNK, :]
        abt = lax.dot_general(wabt_ref[...], xc, (((1,), (1,)), ((), ())), precision=HIGHEST,
                              preferred_element_type=F32)
        zt = abt + pr_ref[:, 1:2]
        spt = jnp.maximum(zt, 0.0) + jnp.log1p(jnp.exp(-jnp.abs(zt)))
        gbr_ref[c] = pr_ref[:, 0:1] * spt


def _la_in_proj(x_p, x_s, mod, layer, dec_seq, w_in, a_log, dt_bias):
    n_prompt, d = x_p.shape
    n = n_prompt + x_s.shape[0]
    wide = LA_QKV + LA_HEADS * LA_DV
    w_main = w_in[:, :wide].astype(BF16)
    w_ab = w_in[:, wide:]
    n_ab = w_ab.shape[1]
    w_ab_pad = jnp.zeros((d, LANES), F32).at[:, :n_ab].set(w_ab)
    neg_a = -jnp.exp(a_log.astype(F32)).reshape(-1)
    dtb = dt_bias.astype(F32).reshape(-1)
    pc = jnp.zeros((SUBLANES, LANES), F32).at[0, :2 * LA_HEADS].set(neg_a).at[1, :2 * LA_HEADS].set(dtb)
    pr = jnp.zeros((n_ab, LANES), F32).at[:2 * LA_HEADS, 0].set(neg_a).at[:2 * LA_HEADS, 1].set(dtb)
    r = ROW_BLOCK
    return pl.pallas_call(
        functools.partial(_la_in_kernel, npb=n_prompt // r),
        grid=(n // r,),
        in_specs=_stream_specs(n_prompt, r, d) + [
            _mod_spec(layer, 1, n_prompt, dec_seq, r),
            _mod_spec(layer, 0, n_prompt, dec_seq, r),
            pl.BlockSpec((d, wide), lambda i: (0, 0)),
            pl.BlockSpec((d, LANES), lambda i: (0, 0)),
            pl.BlockSpec((n_ab, d), lambda i: (0, 0)),
            pl.BlockSpec((SUBLANES, LANES), lambda i: (0, 0)),
            pl.BlockSpec((n_ab, LANES), lambda i: (0, 0)),
        ],
        out_specs=[
            pl.BlockSpec((r, wide), lambda i: (i, 0)),
            pl.BlockSpec((r, LANES), lambda i: (i, 0)),
            pl.BlockSpec((r // CHUNK, n_ab, CHUNK), lambda i: (i, 0, 0)),
        ],
        out_shape=[
            jax.ShapeDtypeStruct((n, wide), F32),
            jax.ShapeDtypeStruct((n, LANES), F32),
            jax.ShapeDtypeStruct((n // CHUNK, n_ab, CHUNK), F32),
        ],
        compiler_params=_cparams("parallel"),
        name="deltanet_in_proj",
    )(x_p, x_s, mod, mod, w_main, w_ab_pad, w_ab.T, pc, pr)


def _proj_kernel(x_ref, sc_ref, sh_ref, w_ref, o_ref):
    xm = x_ref[...] * (1.0 + sc_ref[0]) + sh_ref[0]
    o_ref[...] = _bdot(xm, w_ref[...])


def _mod_proj(x, mod, layer, n_prompt, dec_seq, w, name):
    n, d = x.shape
    cols = w.shape[1]
    r = ROW_BLOCK
    return pl.pallas_call(
        _proj_kernel,
        grid=(n // r,),
        in_specs=[
            pl.BlockSpec((r, d), lambda i: (i, 0)),
            _mod_spec(layer, 1, n_prompt, dec_seq, r),
            _mod_spec(layer, 0, n_prompt, dec_seq, r),
            pl.BlockSpec((d, cols), lambda i: (0, 0)),
        ],
        out_specs=pl.BlockSpec((r, cols), lambda i: (i, 0)),
        out_shape=jax.ShapeDtypeStruct((n, cols), F32),
        compiler_params=_cparams("parallel"),
        name=name,
    )(x, mod, mod, w.astype(BF16))


CONV_PAD = SUBLANES
CONV_ROWS = 256


CONV_BLOCK_ELEMS = 4096 * LANES


def _conv_kernel(x_ref, w_ref, o_ref, pad_ref, *, seq, heads):
    zeros = jnp.zeros((CONV_PAD, heads * LANES), F32)
    pad_ref[0:CONV_PAD, :] = zeros
    pad_ref[CONV_PAD + seq:, :] = zeros
    pad_ref[CONV_PAD:CONV_PAD + seq, :] = x_ref[...]
    half = CONV_K // 2
    rows = min(CONV_ROWS, seq)
    for j in range(heads):
        c = pl.program_id(1) * heads + j
        lanes = slice(j * LANES, (j + 1) * LANES)
        is_qk = c < 2 * LA_HEADS
        scale = jnp.where(c < LA_HEADS, LA_DK ** -0.5, 1.0).astype(F32)
        for r0 in range(0, seq, rows):
            acc = jnp.zeros((rows, LANES), F32)
            for tap in range(CONV_K):
                start = r0 + CONV_PAD + tap - half
                acc = acc + w_ref[tap:tap + 1, lanes] * pad_ref[start:start + rows, lanes]
            y = _silu(acc)
            nrm = lax.rsqrt(jnp.sum(y * y, axis=-1, keepdims=True) + 1e-6) * scale
            o_ref[r0:r0 + rows, lanes] = y * jnp.where(is_qk, nrm, 1.0)


def _conv_qkv(proj, conv_w, row0, batch, seq):
    w = jnp.zeros((SUBLANES, LA_QKV), F32).at[:CONV_K].set(conv_w.astype(F32))
    blk0 = row0 // seq
    heads = max(1, min(LA_HEADS, CONV_BLOCK_ELEMS // (seq * LANES)))
    cw = heads * LANES
    return pl.pallas_call(
        functools.partial(_conv_kernel, seq=seq, heads=heads),
        grid=(batch, LA_QKV // cw),
        in_specs=[
            pl.BlockSpec((seq, cw), lambda b, c: (blk0 + b, c)),
            pl.BlockSpec((SUBLANES, cw), lambda b, c: (0, c)),
        ],
        out_specs=pl.BlockSpec((seq, cw), lambda b, c: (b, c)),
        out_shape=jax.ShapeDtypeStruct((batch * seq, LA_QKV), F32),
        scratch_shapes=[pltpu.VMEM((seq + 2 * CONV_PAD, cw), F32)],
        compiler_params=_cparams("parallel", "parallel"),
        name="deltanet_conv",
    )(proj, w)


DELTA_ROWS = 256
PAIR = 2 * CHUNK
N_DIRS = 2


PREP_HEADS = 2


def _pair_masks(rev):
    ii = lax.broadcasted_iota(jnp.int32, (PAIR, PAIR), 0)
    jj = lax.broadcasted_iota(jnp.int32, (PAIR, PAIR), 1)
    same_chunk = (ii // CHUNK) == (jj // CHUNK)
    same_sub = (ii // SUB) == (jj // SUB)
    if rev:
        return same_chunk, same_sub, same_chunk & (jj >= ii), same_chunk & (jj > ii), same_chunk & (ii >= jj)
    return same_chunk, same_sub, same_chunk & (jj <= ii), same_chunk & (jj < ii), same_chunk & (ii <= jj)


def _delta_prep_kernel(q_ref, k_ref, v_ref, gbc_ref, gbr_ref, u_ref, wq_ref, kd_ref, qk_ref, gl_ref):
    hp = pl.program_id(1)
    lane = lax.broadcasted_iota(jnp.int32, (PAIR, LANES), 1)
    cpp = PAIR // CHUNK
    masks = [_pair_masks(rev) for rev in (False, True)]
    probs = []
    for hh in range(PREP_HEADS):
        cols = slice(hh * LA_DK, (hh + 1) * LA_DK)
        for pr in range(DELTA_ROWS // PAIR):
            rows = slice(pr * PAIR, (pr + 1) * PAIR)
            q = q_ref[rows, cols]
            k = k_ref[rows, cols]
            v = v_ref[rows, cols]
            gbc = gbc_ref[rows, :]
            gram = _bdot_nt(k, k)
            qk_raw = _bdot_nt(q, k)
            for d in range(N_DIRS):
                same_chunk, same_sub, before, strict, before_t = masks[d]
                h = hp * PREP_HEADS + hh
                g_lane = d * LA_HEADS + h
                b_lane = (N_DIRS + d) * LA_HEADS + h
                g_col = jnp.sum(jnp.where(lane == g_lane, gbc, 0.0), axis=1, keepdims=True)
                b_col = jnp.sum(jnp.where(lane == b_lane, gbc, 0.0), axis=1, keepdims=True)
                g_row = jnp.concatenate([gbr_ref[pr * cpp + c, pl.ds(g_lane, 1), :] for c in range(cpp)], axis=1)
                gam_col = jnp.sum(jnp.where(before, g_row, 0.0), axis=1, keepdims=True)
                gam_row = jnp.sum(jnp.where(before_t, g_col, 0.0), axis=0, keepdims=True)
                tot_col = jnp.sum(jnp.where(same_chunk, g_row, 0.0), axis=1, keepdims=True)
                tot_row = jnp.sum(jnp.where(same_chunk, g_col, 0.0), axis=0, keepdims=True)
                dec = jnp.where(before, jnp.exp(jnp.where(before, gam_col - gam_row, 0.0)), 0.0)
                e_col = jnp.exp(gam_col)
                nm = jnp.where(strict, -(gram * b_col * dec), 0.0)
                nd = jnp.where(same_sub, nm, 0.0)
                probs.append(dict(
                    hh=hh, pr=pr, d=d, rows=rows, nd=nd, ne=nm - nd,
                    rhs=jnp.concatenate([v * b_col, k * (b_col * e_col)], axis=1),
                    qd=q * e_col, kd=k * jnp.exp(tot_col - gam_col), qkm=qk_raw * dec, gl=jnp.exp(tot_row)))
    for p in probs:
        p["yd"] = p["nd"]
        p["p"] = p["nd"]
    for _ in range(3):
        for p in probs:
            p["p"] = _bdot(p["p"], p["p"])
        for p in probs:
            p["yd"] = p["yd"] + p["p"] + _bdot(p["yd"], p["p"])
    for p in probs:
        p["f"] = p["ne"] + _bdot(p["yd"], p["ne"])
    for p in probs:
        p["f2"] = _bdot(p["f"], p["f"])
    for p in probs:
        p["gm"] = p["f"] + p["f2"] + _bdot(p["f"], p["f2"])
    for p in probs:
        p["yt"] = p["gm"] + p["yd"] + _bdot(p["gm"], p["yd"])
    for p in probs:
        p["sol"] = p["rhs"] + _bdot(p["yt"], p["rhs"])
    for p in probs:
        d, hh, pr, rows = p["d"], p["hh"], p["pr"], p["rows"]
        u = p["sol"][:, :LA_DV]
        w = p["sol"][:, LA_DV:]
        u_ref[d, hh, rows, :] = u
        kd_ref[d, hh, rows, :] = p["kd"].T.astype(BF16)
        qk_ref[d, hh, rows, :] = p["qkm"].astype(BF16)
        for c in range(cpp):
            cg = pr * cpp + c
            cr = slice(c * CHUNK, (c + 1) * CHUNK)
            wq_ref[d, hh, cg * PAIR:cg * PAIR + CHUNK, :] = w[cr, :].astype(BF16)
            wq_ref[d, hh, cg * PAIR + CHUNK:(cg + 1) * PAIR, :] = p["qd"][cr, :].astype(BF16)
            gl_ref[d, hh, cg] = p["gl"][:, c * CHUNK:c * CHUNK + 1] + jnp.zeros((1, LANES), F32)


def _delta_prep(qkv, gbc, gbr, row0):
    n = qkv.shape[0]
    r = DELTA_ROWS
    g0 = row0 // r
    n_ab = gbr.shape[1]
    cpb = r // CHUNK
    hg = LA_HEADS // PREP_HEADS
    hw = PREP_HEADS * LA_DK
    hm = lambda rows, dt: jax.ShapeDtypeStruct((N_DIRS, LA_HEADS, rows, LANES), dt)
    hspec = lambda rows: pl.BlockSpec((N_DIRS, PREP_HEADS, rows, LANES), lambda i, h: (0, h, i, 0))
    return pl.pallas_call(
        _delta_prep_kernel,
        grid=(n // r, hg),
        in_specs=[
            pl.BlockSpec((r, hw), lambda i, h: (i, h)),
            pl.BlockSpec((r, hw), lambda i, h: (i, hg + h)),
            pl.BlockSpec((r, hw), lambda i, h: (i, 2 * hg + h)),
            pl.BlockSpec((r, LANES), lambda i, h: (g0 + i, 0)),
            pl.BlockSpec((cpb, n_ab, CHUNK), lambda i, h: (g0 + i, 0, 0)),
        ],
        out_specs=[
            hspec(r), hspec(2 * r), hspec(r), hspec(r),
            pl.BlockSpec((N_DIRS, PREP_HEADS, cpb, 1, LANES), lambda i, h: (0, h, i, 0, 0)),
        ],
        out_shape=[
            hm(n, F32), hm(2 * n, BF16), hm(n, BF16), hm(n, BF16),
            jax.ShapeDtypeStruct((N_DIRS, LA_HEADS, n // CHUNK, 1, LANES), F32),
        ],
        compiler_params=_cparams("parallel", "parallel"),
        name="deltanet_prep",
    )(qkv, qkv, qkv, gbc, gbr)


def _delta_scan_kernel(*refs, has_s0):
    u_refs, wq_refs, kd_refs, qk_refs, gl_refs = (refs[0:2], refs[2:4], refs[4:6], refs[6:8], refs[8:10])
    if has_s0:
        s0_ref, of_ref, ob_ref, sfin_ref, s_ref = refs[10:]
    else:
        of_ref, ob_ref, sfin_ref, s_ref = refs[10:]
    o_refs = (of_ref, ob_ref)
    t = pl.program_id(1)

    @pl.when(t == 0)
    def _():
        if has_s0:
            s_ref[...] = s0_ref[0]
        else:
            s_ref[...] = jnp.zeros_like(s_ref)

    n_pairs = DELTA_ROWS // PAIR
    cpp = PAIR // CHUNK
    chains = [(d, h) for d in range(N_DIRS) for h in range(LA_HEADS)]
    zeros = jnp.zeros((CHUNK, LA_DV), F32)
    for step in range(n_pairs):
        pair_of = [n_pairs - 1 - step if d == 1 else step for d in range(N_DIRS)]
        s = [s_ref[d * LA_HEADS + h] for d, h in chains]
        v_new = [[None] * cpp for _ in chains]
        o_inter = [[None] * cpp for _ in chains]
        for j in range(cpp):
            chunk_of = [cpp - 1 - j if d == 1 else j for d in range(N_DIRS)]
            r = []
            for i, (d, h) in enumerate(chains):
                cg = pair_of[d] * cpp + chunk_of[d]
                r.append(jnp.dot(wq_refs[d][0, h, cg * PAIR:(cg + 1) * PAIR, :], s[i].astype(BF16),
                                 preferred_element_type=F32))
            for i, (d, h) in enumerate(chains):
                c = chunk_of[d]
                cg = pair_of[d] * cpp + c
                v_new[i][c] = u_refs[d][0, h, cg * CHUNK:(cg + 1) * CHUNK, :] - r[i][:CHUNK]
                o_inter[i][c] = r[i][CHUNK:]
                vz = jnp.concatenate([v_new[i][c] if m == c else zeros for m in range(cpp)], axis=0)
                kd_t = kd_refs[d][0, h, pair_of[d] * PAIR:(pair_of[d] + 1) * PAIR, :]
                s[i] = s[i] * gl_refs[d][0, h, cg] + jnp.dot(kd_t, vz.astype(BF16), preferred_element_type=F32)
        for i, (d, h) in enumerate(chains):
            s_ref[d * LA_HEADS + h] = s[i]
            rows = slice(pair_of[d] * PAIR, (pair_of[d] + 1) * PAIR)
            o_intra = jnp.dot(qk_refs[d][0, h, rows, :], jnp.concatenate(v_new[i], axis=0).astype(BF16),
                              preferred_element_type=F32)
            o_refs[d][h, rows, :] = jnp.concatenate(o_inter[i], axis=0) + o_intra

    @pl.when(t == pl.num_programs(1) - 1)
    def _():
        sfin_ref[0] = s_ref[...]


def _delta_scan(prep, s0, batch, seq):
    u, wq, kd, qk, gl = prep
    nt = seq // DELTA_ROWS
    r = DELTA_ROWS
    cpb = r // CHUNK
    n_chain = N_DIRS * LA_HEADS

    def tt(d, t):
        return nt - 1 - t if d == 1 else t

    def dspecs(rows):
        return [pl.BlockSpec((1, LA_HEADS, rows, LANES), functools.partial(
            lambda b, t, d: (d, 0, b * nt + tt(d, t), 0), d=d)) for d in range(N_DIRS)]

    gl_specs = [pl.BlockSpec((1, LA_HEADS, cpb, 1, LANES), functools.partial(
        lambda b, t, d: (d, 0, b * nt + tt(d, t), 0, 0), d=d)) for d in range(N_DIRS)]
    in_specs = dspecs(r) + dspecs(2 * r) + dspecs(r) + dspecs(r) + gl_specs
    args = [u, u, wq, wq, kd, kd, qk, qk, gl, gl]
    has_s0 = s0 is not None
    if has_s0:
        in_specs.append(pl.BlockSpec((1, n_chain, LA_DK, LA_DV), lambda b, t: (b, 0, 0, 0)))
        args.append(s0)
    o_shape = jax.ShapeDtypeStruct((LA_HEADS, batch * seq, LA_DV), F32)
    o_specs = [pl.BlockSpec((LA_HEADS, r, LANES), functools.partial(
        lambda b, t, d: (0, b * nt + tt(d, t), 0), d=d)) for d in range(N_DIRS)]
    return pl.pallas_call(
        functools.partial(_delta_scan_kernel, has_s0=has_s0),
        grid=(batch, nt),
        in_specs=in_specs,
        out_specs=o_specs + [pl.BlockSpec((1, n_chain, LA_DK, LA_DV), lambda b, t: (b, 0, 0, 0))],
        out_shape=[o_shape, o_shape, jax.ShapeDtypeStruct((batch, n_chain, LA_DK, LA_DV), F32)],
        scratch_shapes=[pltpu.VMEM((n_chain, LA_DK, LA_DV), F32)],
        compiler_params=_cparams("parallel", "arbitrary"),
        name="deltanet_scan",
    )(*args)


def _stream_specs(n_prompt, rows, cols):
    npb = n_prompt // rows
    return [
        pl.BlockSpec((rows, cols), lambda i: (jnp.minimum(i, npb - 1), 0)),
        pl.BlockSpec((rows, cols), lambda i: (jnp.maximum(i - npb, 0), 0)),
    ]


def _la_out_kernel(ofp_ref, obp_ref, ofs_ref, obs_ref, z_ref, xp_ref, xs_ref, g1_ref, ng_ref, w_ref, lng_ref,
                   lnb_ref, y_ref, *, npb):
    is_prompt = pl.program_id(0) < npb
    x = jnp.where(is_prompt, xp_ref[...], xs_ref[...])
    ng = ng_ref[0:1, :]
    parts = []
    for h in range(LA_HEADS):
        oh = jnp.where(is_prompt, ofp_ref[h] + obp_ref[h], ofs_ref[h] + obs_ref[h])
        r = lax.rsqrt(jnp.mean(oh * oh, axis=-1, keepdims=True) + RMS_EPS)
        parts.append(oh * r * ng)
    on = jnp.concatenate(parts, axis=1) * _silu(z_ref[...])
    out = _bdot(on, w_ref[...])
    xr = ALPHA * x + g1_ref[0] * out
    y_ref[...] = _layer_norm_rows(xr, lng_ref[0:1, :], lnb_ref[0:1, :])


def _pad_rows(v):
    return jnp.zeros((SUBLANES, v.shape[0]), F32).at[0].set(v.astype(F32))


def _la_out(o_p, o_s, proj, x_p, x_s, mod, layer, dec_seq, norm_g, w_out, ln_g, ln_b):
    n_prompt, d = x_p.shape
    n = n_prompt + x_s.shape[0]
    r = ROW_BLOCK
    zblk = LA_QKV // d
    npb = n_prompt // r
    sp = pl.BlockSpec((LA_HEADS, r, LA_DV), lambda i: (0, jnp.minimum(i, npb - 1), 0))
    ss = pl.BlockSpec((LA_HEADS, r, LA_DV), lambda i: (0, jnp.maximum(i - npb, 0), 0))
    return pl.pallas_call(
        functools.partial(_la_out_kernel, npb=n_prompt // r),
        grid=(n // r,),
        in_specs=[
            sp, sp, ss, ss,
            pl.BlockSpec((r, d), lambda i: (i, zblk)),
            *_stream_specs(n_prompt, r, d),
            _mod_spec(layer, 2, n_prompt, dec_seq, r),
            pl.BlockSpec((SUBLANES, LA_DV), lambda i: (0, 0)),
            pl.BlockSpec((d, d), lambda i: (0, 0)),
            pl.BlockSpec((SUBLANES, d), lambda i: (0, 0)),
            pl.BlockSpec((SUBLANES, d), lambda i: (0, 0)),
        ],
        out_specs=pl.BlockSpec((r, d), lambda i: (i, 0)),
        out_shape=jax.ShapeDtypeStruct((n, d), F32),
        compiler_params=_cparams("parallel"),
        name="deltanet_out",
    )(o_p[0], o_p[1], o_s[0], o_s[1], proj, x_p, x_s, mod, _pad_rows(norm_g), w_out.astype(BF16), _pad_rows(ln_g),
      _pad_rows(ln_b))


def _out_kernel(op_ref, os_ref, x_ref, g1_ref, w_ref, lng_ref, lnb_ref, y_ref, *, npb):
    is_prompt = pl.program_id(0) < npb
    o = jnp.where(is_prompt, op_ref[...], os_ref[...])
    out = _bdot(o, w_ref[...])
    xr = ALPHA * x_ref[...] + g1_ref[0] * out
    y_ref[...] = _layer_norm_rows(xr, lng_ref[0:1, :], lnb_ref[0:1, :])


def _swa_out(o_p, o_s, x, mod, layer, n_prompt, dec_seq, w_out, ln_g, ln_b):
    n, d = x.shape
    r = ROW_BLOCK
    sp, ss = _stream_specs(n_prompt, r, d)
    return pl.pallas_call(
        functools.partial(_out_kernel, npb=n_prompt // r),
        grid=(n // r,),
        in_specs=[
            sp, ss,
            pl.BlockSpec((r, d), lambda i: (i, 0)),
            _mod_spec(layer, 2, n_prompt, dec_seq, r),
            pl.BlockSpec((d, d), lambda i: (0, 0)),
            pl.BlockSpec((SUBLANES, d), lambda i: (0, 0)),
            pl.BlockSpec((SUBLANES, d), lambda i: (0, 0)),
        ],
        out_specs=pl.BlockSpec((r, d), lambda i: (i, 0)),
        out_shape=jax.ShapeDtypeStruct((n, d), F32),
        compiler_params=_cparams("parallel"),
        name="swa_out",
    )(o_p, o_s, x, mod, w_out.astype(BF16), _pad_rows(ln_g), _pad_rows(ln_b))


SWA_SCALE = SWA_DH ** -0.5


def _stack_group(q, kv):
    base = kv * SWA_GROUP * SWA_DH
    return jnp.concatenate([q[:, base + g * SWA_DH: base + (g + 1) * SWA_DH] for g in range(SWA_GROUP)], axis=0)


def _sink_col(sink_ref, kv, rows):
    return jnp.concatenate(
        [jnp.full((rows, 1), sink_ref[kv * SWA_GROUP + g], F32) for g in range(SWA_GROUP)], axis=0)


def _ctx_attn_kernel(sink_ref, q_ref, k_ref, v_ref, o_ref):
    q = q_ref[...]
    k = k_ref[...]
    v = v_ref[...]
    t = q.shape[0]
    outs = []
    for kv in range(SWA_KV_HEADS):
        kh = k[:, kv * SWA_DH:(kv + 1) * SWA_DH]
        vh = v[:, kv * SWA_DH:(kv + 1) * SWA_DH]
        s = _bdot_nt(_stack_group(q, kv), kh) * SWA_SCALE
        sink = _sink_col(sink_ref, kv, t)
        m = jnp.maximum(jnp.max(s, axis=1, keepdims=True), sink)
        p = jnp.exp(s - m)
        den = jnp.sum(p, axis=1, keepdims=True) + jnp.exp(sink - m)
        o = _bdot(p, vh) / den
        outs.extend(o[g * t:(g + 1) * t, :] for g in range(SWA_GROUP))
    o_ref[...] = jnp.concatenate(outs, axis=1)


def _ctx_attention(qkv, sink, batch, seq):
    kcol = SWA_HEADS * SWA_DH // SWA_KVW
    return pl.pallas_call(
        _ctx_attn_kernel,
        grid=(batch,),
        in_specs=[
            pl.BlockSpec(memory_space=pltpu.SMEM),
            pl.BlockSpec((seq, SWA_HEADS * SWA_DH), lambda b: (b, 0)),
            pl.BlockSpec((seq, SWA_KVW), lambda b: (b, kcol)),
            pl.BlockSpec((seq, SWA_KVW), lambda b: (b, kcol + 1)),
        ],
        out_specs=pl.BlockSpec((seq, SWA_HEADS * SWA_DH), lambda b: (b, 0)),
        out_shape=jax.ShapeDtypeStruct((batch * seq, SWA_HEADS * SWA_DH), F32),
        compiler_params=_cparams("parallel"),
        name="swa_context_attention",
    )(sink.astype(F32), qkv, qkv, qkv)


ROPE_COLS = SWA_HEADS * SWA_DH + SWA_KVW
ROPE_SHIFT = SWA_DH // 4


def _rope_tables(t_len):
    t = jnp.arange(t_len)
    r = (t // GRID_W).astype(F32)
    col = (t % GRID_W).astype(F32)
    inv = ROPE_BASE ** (-jnp.arange(ROT_FREQS, dtype=F32) / ROT_FREQS)
    ang_r = r[:, None] * inv
    ang_c = col[:, None] * inv
    ang = jnp.concatenate([ang_r, ang_r, ang_c, ang_c], -1)
    cos = jnp.cos(ang)
    sin = jnp.sin(ang)
    first = (jnp.arange(SWA_DH) % (2 * ROPE_SHIFT)) < ROPE_SHIFT
    sin_up = jnp.where(first, -sin, 0.0)
    sin_dn = jnp.where(first, 0.0, sin)
    reps = LANES // SWA_DH
    return tuple(jnp.tile(a, (1, reps)) for a in (cos, sin_up, sin_dn))


def _rope_kernel(x_ref, cos_ref, su_ref, sd_ref, o_ref):
    cos = cos_ref[...]
    su = su_ref[...]
    sd = sd_ref[...]
    for j in range(ROPE_COLS // LANES):
        x = x_ref[:, j * LANES:(j + 1) * LANES]
        up = pltpu.roll(x, LANES - ROPE_SHIFT, 1)
        dn = pltpu.roll(x, ROPE_SHIFT, 1)
        o_ref[:, j * LANES:(j + 1) * LANES] = x * cos + up * su + dn * sd


def _rope_qk(qkv, row0, batch, seq):
    r = ROW_BLOCK
    nb = seq // r
    blk0 = row0 // r
    tables = _rope_tables(seq)
    tspec = pl.BlockSpec((r, LANES), lambda b, i: (i, 0))
    return pl.pallas_call(
        _rope_kernel,
        grid=(batch, nb),
        in_specs=[pl.BlockSpec((r, ROPE_COLS), lambda b, i: (blk0 + b * nb + i, 0)), tspec, tspec, tspec],
        out_specs=pl.BlockSpec((r, ROPE_COLS), lambda b, i: (b * nb + i, 0)),
        out_shape=jax.ShapeDtypeStruct((batch * seq, ROPE_COLS), F32),
        compiler_params=_cparams("parallel", "parallel"),
        name="swa_rope",
    )(qkv, *tables)


def _band_bias():
    r = jnp.arange(SWA_GROUP * Q_BLOCK)[:, None] % Q_BLOCK
    c = jnp.arange(3 * Q_BLOCK)[None, :] - Q_BLOCK
    return jnp.where(jnp.abs(r - c) <= WINDOW, 0.0, NEG_INF).astype(F32)


def _lat_attn_kernel(sink_ref, band_ref, q_ref, kp_ref, kc_ref, kn_ref, vp_ref, vc_ref, vn_ref, ck_ref, cv_ref,
                     o_ref):
    i = pl.program_id(1)
    q = q_ref[...] * SWA_SCALE
    kw = jnp.concatenate([kp_ref[...], kc_ref[...], kn_ref[...]], axis=0)
    vw = jnp.concatenate([vp_ref[...], vc_ref[...], vn_ref[...]], axis=0)
    ck = ck_ref[...]
    cv = cv_ref[...]
    col = lax.broadcasted_iota(jnp.int32, (1, 3 * Q_BLOCK), 1)
    prev_bias = jnp.where(i > 0, 0.0, NEG_INF).astype(F32)
    next_bias = jnp.where(i < pl.num_programs(1) - 1, 0.0, NEG_INF).astype(F32)
    bias = band_ref[...] + jnp.where(col < Q_BLOCK, prev_bias, jnp.where(col >= 2 * Q_BLOCK, next_bias, 0.0))
    heads = range(SWA_KV_HEADS)
    cols = [slice(kv * SWA_DH, (kv + 1) * SWA_DH) for kv in heads]
    qg = [_stack_group(q, kv) for kv in heads]
    s_win = [_bdot_nt(qg[kv], kw[:, cols[kv]]) for kv in heads]
    s_ctx = [_bdot_nt(qg[kv], ck[:, cols[kv]]) for kv in heads]
    p_win, p_ctx, den = [], [], []
    for kv in heads:
        sw = s_win[kv] + bias
        sc = s_ctx[kv]
        sink = _sink_col(sink_ref, kv, Q_BLOCK)
        m = jnp.maximum(jnp.maximum(jnp.max(sw, axis=1, keepdims=True), jnp.max(sc, axis=1, keepdims=True)), sink)
        p_win.append(jnp.exp(sw - m))
        p_ctx.append(jnp.exp(sc - m))
        den.append(jnp.sum(p_win[kv], axis=1, keepdims=True) + jnp.sum(p_ctx[kv], axis=1, keepdims=True)
                   + jnp.exp(sink - m))
    o_win = [_bdot(p_win[kv], vw[:, cols[kv]]) for kv in heads]
    o_ctx = [_bdot(p_ctx[kv], cv[:, cols[kv]]) for kv in heads]
    outs = []
    for kv in heads:
        o = (o_win[kv] + o_ctx[kv]) / den[kv]
        outs.extend(o[g * Q_BLOCK:(g + 1) * Q_BLOCK, :] for g in range(SWA_GROUP))
    o_ref[...] = jnp.concatenate(outs, axis=1)


def _lat_attention(qk_rope, qkv, cache_k, cache_v, sink, row0, batch, seq):
    assert WINDOW == Q_BLOCK and seq % Q_BLOCK == 0
    nq = seq // Q_BLOCK
    blk0 = row0 // Q_BLOCK
    qw = SWA_HEADS * SWA_DH
    kcol = qw // SWA_KVW
    past = cache_k.shape[0] // batch

    def kspec(off):
        return pl.BlockSpec((Q_BLOCK, SWA_KVW), lambda b, i: (b * nq + jnp.clip(i + off, 0, nq - 1), kcol))

    def vspec(off):
        return pl.BlockSpec((Q_BLOCK, SWA_KVW),
                            lambda b, i: (blk0 + b * nq + jnp.clip(i + off, 0, nq - 1), kcol + 1))

    cspec = pl.BlockSpec((past, SWA_KVW), lambda b, i: (b, 0))
    return pl.pallas_call(
        _lat_attn_kernel,
        grid=(batch, nq),
        in_specs=[
            pl.BlockSpec(memory_space=pltpu.SMEM),
            pl.BlockSpec((SWA_GROUP * Q_BLOCK, 3 * Q_BLOCK), lambda b, i: (0, 0)),
            pl.BlockSpec((Q_BLOCK, qw), lambda b, i: (b * nq + i, 0)),
            kspec(-1), kspec(0), kspec(1),
            vspec(-1), vspec(0), vspec(1),
            cspec, cspec,
        ],
        out_specs=pl.BlockSpec((Q_BLOCK, qw), lambda b, i: (b * nq + i, 0)),
        out_shape=jax.ShapeDtypeStruct((batch * seq, qw), F32),
        compiler_params=_cparams("parallel", "parallel"),
        name="swa_latent_attention",
    )(sink.astype(F32), _band_bias(), qk_rope, qk_rope, qk_rope, qk_rope, qkv, qkv, qkv, cache_k, cache_v)


GROUP_SIZE = N_EXPERTS // N_GROUPS
U32 = jnp.uint32
HALF_D = D_MODEL // 2
TILE_ROWS = HALF_D // LANES


def _bf16_bits_high(x):
    return pltpu.bitcast(x.astype(BF16).astype(F32), U32)


def _to_tiles(ref, x):
    rows = x.shape[0]
    for s in range(TILE_ROWS):
        lo = _bf16_bits_high(x[:, s * LANES:(s + 1) * LANES]) >> 16
        hi = _bf16_bits_high(x[:, HALF_D + s * LANES:HALF_D + (s + 1) * LANES])
        ref[pl.ds(s, rows, stride=TILE_ROWS), :] = lo | hi


def _from_tiles(ref, rows):
    words = [ref[pl.ds(s, rows, stride=TILE_ROWS), :] for s in range(TILE_ROWS)]
    lo = [pltpu.bitcast(w << 16, F32) for w in words]
    hi = [pltpu.bitcast(w & jnp.uint32(0xFFFF0000), F32) for w in words]
    return jnp.concatenate(lo + hi, axis=1)


def _first_argmax(v, iota, axis, size):
    m = jnp.max(v, axis=axis, keepdims=True)
    idx = jnp.min(jnp.where(v == m, iota, size), axis=axis, keepdims=True)
    return m, idx


def _router_kernel(x_ref, sc_ref, sh_ref, rt_ref, bias_ref, hs_ref, idx_ref, w_ref, pos_ref, wtm_ref, cnt_ref):
    i = pl.program_id(0)
    rows = x_ref.shape[0]

    @pl.when(i == 0)
    def _():
        cnt_ref[...] = jnp.zeros_like(cnt_ref)

    hs = x_ref[...] * (1.0 + sc_ref[0]) + sh_ref[0]
    _to_tiles(hs_ref, hs)
    logits = lax.dot_general(rt_ref[...], hs, (((1,), (1,)), ((), ())), precision=HIGHEST,
                             preferred_element_type=F32)
    scores = jax.nn.sigmoid(logits)
    biased = scores + bias_ref[:, 0:1]
    b3 = biased.reshape(N_GROUPS, GROUP_SIZE, rows)
    mem = lax.broadcasted_iota(jnp.int32, b3.shape, 1)
    m1, i1 = _first_argmax(b3, mem, 1, GROUP_SIZE)
    m2 = jnp.max(jnp.where(mem == i1, -jnp.inf, b3), axis=1, keepdims=True)
    grp = (m1 + m2).reshape(N_GROUPS, rows)
    giota = lax.broadcasted_iota(jnp.int32, grp.shape, 0)
    gsel = jnp.zeros(grp.shape, jnp.bool_)
    for _ in range(TOPK_GROUPS):
        _, gi = _first_argmax(grp, giota, 0, N_GROUPS)
        hit = giota == gi
        gsel = gsel | hit
        grp = jnp.where(hit, -jnp.inf, grp)
    gmask = jnp.broadcast_to(gsel.reshape(N_GROUPS, 1, rows), b3.shape).reshape(N_EXPERTS, rows)
    sel = jnp.where(gmask, biased, -jnp.inf)
    eiota = lax.broadcasted_iota(jnp.int32, sel.shape, 0)
    chosen_any = jnp.zeros(sel.shape, jnp.bool_)
    idx_rows, w_rows, hits = [], [], []
    for _ in range(TOP_K):
        _, ei = _first_argmax(sel, eiota, 0, N_EXPERTS)
        hit = eiota == ei
        chosen_any = chosen_any | hit
        idx_rows.append(ei)
        w_rows.append(jnp.sum(jnp.where(hit, scores, 0.0), axis=0, keepdims=True))
        hits.append(hit)
        sel = jnp.where(hit, -jnp.inf, sel)
    wsum = w_rows[0]
    for w in w_rows[1:]:
        wsum = wsum + w
    w_rows = [w / wsum * ROUTED_SCALE for w in w_rows]
    onehot = chosen_any.astype(F32)
    s_i = lax.broadcasted_iota(jnp.int32, (rows, rows), 0)
    t_i = lax.broadcasted_iota(jnp.int32, (rows, rows), 1)
    earlier = (s_i < t_i).astype(F32)
    rank = cnt_ref[:, 0:1] + _bdot(onehot, earlier)
    cnt_ref[...] = cnt_ref[...] + jnp.sum(onehot, axis=1, keepdims=True)
    pos_rows = [jnp.sum(jnp.where(hit, rank, 0.0), axis=0, keepdims=True) for hit in hits]
    idx_ref[...] = jnp.concatenate(idx_rows, axis=0)
    w8 = jnp.concatenate(w_rows, axis=0)
    w_ref[...] = w8
    pos_ref[...] = jnp.concatenate(pos_rows, axis=0).astype(jnp.int32)
    wtm_ref[...] = jnp.concatenate([w8, jnp.zeros((LANES - TOP_K, rows), F32)], axis=0).T


def _router(x, mod, layer, n_prompt, dec_seq, router, bias):
    n, d = x.shape
    r = ROW_BLOCK
    bias_pad = jnp.zeros((N_EXPERTS, LANES), F32).at[:, 0].set(bias.astype(F32))
    kspec = pl.BlockSpec((TOP_K, r), lambda i: (0, i))
    return pl.pallas_call(
        _router_kernel,
        grid=(n // r,),
        in_specs=[
            pl.BlockSpec((r, d), lambda i: (i, 0)),
            _mod_spec(layer, 4, n_prompt, dec_seq, r),
            _mod_spec(layer, 3, n_prompt, dec_seq, r),
            pl.BlockSpec((N_EXPERTS, d), lambda i: (0, 0)),
            pl.BlockSpec((N_EXPERTS, LANES), lambda i: (0, 0)),
        ],
        out_specs=[
            pl.BlockSpec((r * TILE_ROWS, LANES), lambda i: (i, 0)),
            kspec, kspec, kspec,
            pl.BlockSpec((r, LANES), lambda i: (i, 0)),
            pl.BlockSpec((N_EXPERTS, LANES), lambda i: (0, 0)),
        ],
        out_shape=[
            jax.ShapeDtypeStruct((n * TILE_ROWS, LANES), U32),
            jax.ShapeDtypeStruct((TOP_K, n), jnp.int32),
            jax.ShapeDtypeStruct((TOP_K, n), F32),
            jax.ShapeDtypeStruct((TOP_K, n), jnp.int32),
            jax.ShapeDtypeStruct((n, LANES), F32),
            jax.ShapeDtypeStruct((N_EXPERTS, LANES), F32),
        ],
        compiler_params=_cparams("arbitrary"),
        name="moe_router",
    )(x, mod, mod, router.T.astype(F32), bias_pad)


DISPATCH_TOKENS = 512


def _dest_kernel(start_ref, idx_ref, pos_ref, o_ref):
    idx = idx_ref[...]
    start = jnp.zeros(idx.shape, jnp.int32)
    for e in range(N_EXPERTS):
        start = jnp.where(idx == e, start_ref[e], start)
    o_ref[...] = (start + pos_ref[...]) * TILE_ROWS


def _dest_rows(pad_start, idx, pos):
    k, n = idx.shape
    t = min(2048, n)
    spec = pl.BlockSpec((k, t), lambda i: (0, i))
    return pl.pallas_call(
        _dest_kernel,
        grid=(n // t,),
        in_specs=[pl.BlockSpec(memory_space=pltpu.SMEM), spec, spec],
        out_specs=spec,
        out_shape=jax.ShapeDtypeStruct((k, n), jnp.int32),
        compiler_params=_cparams("parallel"),
        name="moe_dest_rows",
    )(pad_start, idx, pos)


def _tile_copy(src_ref, src_row, dst_ref, dst_row, sem):
    return pltpu.make_async_copy(src_ref.at[pl.ds(pl.multiple_of(src_row, TILE_ROWS), TILE_ROWS), :],
                                 dst_ref.at[pl.ds(pl.multiple_of(dst_row, TILE_ROWS), TILE_ROWS), :], sem)


def _dispatch_kernel(dest_ref, hs_ref, xs_ref, sem):
    tokens = hs_ref.shape[0] // TILE_ROWS

    def copy(k, t):
        return _tile_copy(hs_ref, t * TILE_ROWS, xs_ref, dest_ref[k, t], sem)

    def start(t, carry):
        for k in range(TOP_K):
            copy(k, t).start(priority=k % 2)
        return carry

    lax.fori_loop(0, tokens, start, 0)

    def wait(t, carry):
        for k in range(TOP_K):
            copy(k, t).wait()
        return carry

    lax.fori_loop(0, tokens, wait, 0)


def _dispatch(hs, dest, n_rows):
    t = DISPATCH_TOKENS
    n = hs.shape[0] // TILE_ROWS
    return pl.pallas_call(
        _dispatch_kernel,
        grid=(n // t,),
        in_specs=[
            pl.BlockSpec((TOP_K, t), lambda i: (0, i), memory_space=pltpu.SMEM),
            pl.BlockSpec((t * TILE_ROWS, LANES), lambda i: (i, 0)),
        ],
        out_specs=pl.BlockSpec(memory_space=pl.ANY),
        out_shape=jax.ShapeDtypeStruct((n_rows * TILE_ROWS, LANES), U32),
        scratch_shapes=[pltpu.SemaphoreType.DMA],
        compiler_params=_cparams("arbitrary"),
        name="moe_dispatch",
    )(dest, hs)


def _expert_kernel(be_ref, bv_ref, xs_ref, wg_ref, wu_ref, wd_ref, y_ref):
    del be_ref
    valid = bv_ref[pl.program_id(0)]

    @pl.when(valid > 0)
    def _():
        x = _from_tiles(xs_ref, MOE_ROWS)
        row = lax.broadcasted_iota(jnp.int32, x.shape, 0)
        x = jnp.where(row < valid, x, 0.0)
        h = _silu(_bdot(x, wg_ref[0])) * _bdot(x, wu_ref[0])
        _to_tiles(y_ref, _bdot(h, wd_ref[0]))

    @pl.when(valid <= 0)
    def _():
        y_ref[...] = jnp.zeros_like(y_ref)


def _experts(xs, blk_expert, blk_valid, w_gate, w_up, w_down, layer):
    nb = xs.shape[0] // (MOE_ROWS * TILE_ROWS)
    _, _, d, de = w_gate.shape
    blk = pl.BlockSpec((MOE_ROWS * TILE_ROWS, LANES), lambda i, be, bv: (i, 0))
    grid_spec = pltpu.PrefetchScalarGridSpec(
        num_scalar_prefetch=2,
        grid=(nb,),
        in_specs=[
            blk,
            pl.BlockSpec((None, 1, d, de), lambda i, be, bv: (layer, be[i], 0, 0)),
            pl.BlockSpec((None, 1, d, de), lambda i, be, bv: (layer, be[i], 0, 0)),
            pl.BlockSpec((None, 1, de, d), lambda i, be, bv: (layer, be[i], 0, 0)),
        ],
        out_specs=blk,
    )
    return pl.pallas_call(
        _expert_kernel,
        grid_spec=grid_spec,
        out_shape=jax.ShapeDtypeStruct(xs.shape, U32),
        compiler_params=_cparams("arbitrary"),
        name="moe_experts",
    )(blk_expert, blk_valid, xs, w_gate, w_up, w_down)


COMBINE_TOKENS = 256


def _combine_kernel(dest_ref, dnext_ref, y_ref, wtm_ref, hs_ref, x_ref, g2_ref, sg_ref, su_ref, sd_ref,
                    lng_ref, lnb_ref, o_ref, ybuf, sems):
    tokens = x_ref.shape[0]
    i = pl.program_id(0)
    slot = i % 2

    def copy(d_ref, s, k, t):
        return _tile_copy(y_ref, d_ref[k, t], ybuf.at[s, k], t * TILE_ROWS, sems.at[s])

    def start_all(d_ref, s):
        def body(t, carry):
            for k in range(TOP_K):
                copy(d_ref, s, k, t).start(priority=k % 2)
            return carry

        lax.fori_loop(0, tokens, body, 0)

    @pl.when(i == 0)
    def _():
        start_all(dest_ref, 0)

    @pl.when(i + 1 < pl.num_programs(0))
    def _():
        start_all(dnext_ref, 1 - slot)

    hs = _from_tiles(hs_ref, tokens)
    shared = _bdot(_silu(_bdot(hs, sg_ref[...])) * _bdot(hs, su_ref[...]), sd_ref[...])

    def wait(t, carry):
        for k in range(TOP_K):
            copy(dest_ref, slot, k, t).wait()
        return carry

    lax.fori_loop(0, tokens, wait, 0)
    wtm = wtm_ref[...]
    routed = _from_tiles(ybuf.at[slot, 0], tokens) * wtm[:, 0:1]
    for k in range(1, TOP_K):
        routed = routed + _from_tiles(ybuf.at[slot, k], tokens) * wtm[:, k:k + 1]
    xr = ALPHA * x_ref[...] + g2_ref[0] * (routed + shared)
    o_ref[...] = _layer_norm_rows(xr, lng_ref[0:1, :], lnb_ref[0:1, :])


def _combine(y, dest, wtm, hs, x, mod, layer, n_prompt, dec_seq, s_gate, s_up, s_down, ln_g, ln_b, row0, n_rows):
    d = x.shape[1]
    t = COMBINE_TOKENS
    ds = s_gate.shape[1]
    b0 = row0 // t
    row = pl.BlockSpec((t, d), lambda i: (i, 0))
    steps = n_rows // t
    return pl.pallas_call(
        _combine_kernel,
        grid=(steps,),
        in_specs=[
            pl.BlockSpec((TOP_K, t), lambda i: (0, b0 + i), memory_space=pltpu.SMEM),
            pl.BlockSpec((TOP_K, t), lambda i: (0, b0 + jnp.minimum(i + 1, steps - 1)), memory_space=pltpu.SMEM),
            pl.BlockSpec(memory_space=pl.ANY),
            pl.BlockSpec((t, LANES), lambda i: (b0 + i, 0)),
            pl.BlockSpec((t * TILE_ROWS, LANES), lambda i: (b0 + i, 0)),
            pl.BlockSpec((t, d), lambda i: (b0 + i, 0)),
            _mod_spec(layer, 5, n_prompt, dec_seq, t, b0),
            pl.BlockSpec((d, ds), lambda i: (0, 0)),
            pl.BlockSpec((d, ds), lambda i: (0, 0)),
            pl.BlockSpec((ds, d), lambda i: (0, 0)),
            pl.BlockSpec((SUBLANES, d), lambda i: (0, 0)),
            pl.BlockSpec((SUBLANES, d), lambda i: (0, 0)),
        ],
        out_specs=row,
        out_shape=jax.ShapeDtypeStruct((n_rows, d), F32),
        scratch_shapes=[pltpu.VMEM((2, TOP_K, t * TILE_ROWS, LANES), U32), pltpu.SemaphoreType.DMA((2,))],
        compiler_params=_cparams("arbitrary"),
        name="moe_combine",
    )(dest, dest, y, wtm, hs, x, mod, s_gate.astype(BF16), s_up.astype(BF16), s_down.astype(BF16),
      _pad_rows(ln_g), _pad_rows(ln_b))


def _moe(x, mod, layer, n_prompt, dec_seq, router, bias, w_gate, w_up, w_down, s_gate, s_up, s_down, ln_g, ln_b,
         split_streams=False):
    n, d = x.shape
    hs, idx, _, pos, wtm, cnt = _router(x, mod, layer, n_prompt, dec_seq, router, bias)
    counts = cnt[:, 0].astype(jnp.int32)
    padded = (counts + MOE_ROWS - 1) // MOE_ROWS * MOE_ROWS
    pad_end = jnp.cumsum(padded)
    pad_start = (pad_end - padded).astype(jnp.int32)
    nb = n * TOP_K // MOE_ROWS + N_EXPERTS
    blk_row = jnp.arange(nb, dtype=jnp.int32) * MOE_ROWS
    blk_expert = jnp.minimum(jnp.sum(pad_end[None, :] <= blk_row[:, None], axis=1), N_EXPERTS - 1).astype(jnp.int32)
    blk_valid = jnp.clip(pad_start[blk_expert] + counts[blk_expert] - blk_row, 0, MOE_ROWS).astype(jnp.int32)
    dest = _dest_rows(pad_start, idx, pos)
    xs = _dispatch(hs, dest, nb * MOE_ROWS)
    y = _experts(xs, blk_expert, blk_valid, w_gate, w_up, w_down, layer)
    parts = ((0, n_prompt), (n_prompt, n - n_prompt)) if split_streams else ((0, n),)
    outs = [_combine(y, dest, wtm, hs, x, mod, layer, n_prompt, dec_seq, s_gate, s_up, s_down, ln_g, ln_b, r0, nr)
            for r0, nr in parts]
    return tuple(outs) if split_streams else outs[0]


def kernel(x_prompt, x_sample, state_la, cache_k, cache_v, c, c_ctx, w_mod, b_mod, ln_g, ln_b,
           la_w_in, la_conv, la_a_log, la_dt_bias, la_norm_g, la_w_out,
           swa_w_qkv, swa_sink, swa_w_out,
           moe_router, moe_bias, moe_w_gate, moe_w_up, moe_w_down, sh_w_gate, sh_w_up, sh_w_down):
    bp, tp, d = x_prompt.shape
    bs, ts, _ = x_sample.shape
    n_prompt = bp * tp
    assert bs + 1 <= COND_ROWS and d == D_MODEL
    assert n_prompt % ROW_BLOCK == 0 and ts % ROW_BLOCK == 0 and tp % DELTA_ROWS == 0

    cond = jnp.zeros((COND_ROWS, d), F32).at[0].set(c_ctx).at[1:1 + bs].set(c)
    mod = _modulation(cond, w_mod, b_mod)
    x_p = x_prompt.reshape(n_prompt, d)
    x_s = x_sample.reshape(bs * ts, d)

    def moe(x, i, split_streams=False):
        return _moe(x, mod, i, n_prompt, ts, moe_router[i], moe_bias[i], moe_w_gate, moe_w_up, moe_w_down,
                    sh_w_gate[i], sh_w_up[i], sh_w_down[i], ln_g[i, 1], ln_b[i, 1], split_streams)

    proj, gbc, gbr = _la_in_proj(x_p, x_s, mod, 0, ts, la_w_in[0], la_a_log[0], la_dt_bias[0])
    qkv_p = _conv_qkv(proj, la_conv[0], 0, bp, tp)
    qkv_s = _conv_qkv(proj, la_conv[0], n_prompt, bs, ts)
    opf, opb, s_fin = _delta_scan(_delta_prep(qkv_p, gbc, gbr, 0), None, bp, tp)
    s0 = state_la[:, 0].reshape(bs, N_DIRS * LA_HEADS, LA_DK, LA_DV)
    osf, osb, _ = _delta_scan(_delta_prep(qkv_s, gbc, gbr, n_prompt), s0, bs, ts)
    x = _la_out((opf, opb), (osf, osb), proj, x_p, x_s, mod, 0, ts, la_norm_g[0], la_w_out[0], ln_g[0, 0],
                ln_b[0, 0])
    new_la = s_fin.reshape(bp, 1, N_DIRS, LA_HEADS, LA_DK, LA_DV)
    x = moe(x, 0)

    qkv = _mod_proj(x, mod, 1, n_prompt, ts, swa_w_qkv[0], "swa_qkv_proj")
    qw = SWA_HEADS * SWA_DH
    new_k = qkv[:n_prompt, qw:qw + SWA_KVW].reshape(bp, 1, tp, SWA_KV_HEADS, SWA_DH)
    new_v = qkv[:n_prompt, qw + SWA_KVW:].reshape(bp, 1, tp, SWA_KV_HEADS, SWA_DH)
    a_p = _ctx_attention(qkv, swa_sink[0], bp, tp)
    qk_rope = _rope_qk(qkv, n_prompt, bs, ts)
    past = cache_k.shape[2]
    a_s = _lat_attention(qk_rope, qkv, cache_k[:, 0].reshape(bs * past, SWA_KVW),
                         cache_v[:, 0].reshape(bs * past, SWA_KVW), swa_sink[0], n_prompt, bs, ts)
    x = _swa_out(a_p, a_s, x, mod, 1, n_prompt, ts, swa_w_out[0], ln_g[1, 0], ln_b[1, 0])
    y_p, y_s = moe(x, 1, split_streams=True)
    return y_p.reshape(bp, tp, d), y_s.reshape(bs, ts, d), new_la, new_k, new_v
```

```python
import functools
import math

import jax
import jax.numpy as jnp
from jax import lax
from jax.experimental import pallas as pl
from jax.experimental.pallas import tpu as pltpu

F32 = jnp.float32
BF16 = jnp.bfloat16
HIGHEST = lax.Precision.HIGHEST

LANES = 128
SUBLANES = 8
VMEM_LIMIT_BYTES = 56 * 1024 * 1024

D_MODEL = 1024
DEPTH = 2
GRID_W = 64
LA_DK = 128
LA_DV = 128
LA_HEADS = D_MODEL // LA_DK
LA_QKV = LA_HEADS * (2 * LA_DK + LA_DV)
CONV_K = 5
CHUNK = 64
SUB = 16
SWA_DH = 64
SWA_HEADS = D_MODEL // SWA_DH
SWA_KV_HEADS = SWA_HEADS // 4
SWA_GROUP = SWA_HEADS // SWA_KV_HEADS
SWA_KVW = SWA_KV_HEADS * SWA_DH
WINDOW = 128
Q_BLOCK = 128
ROT_FREQS = SWA_DH // 4
ROPE_BASE = 10000.0
NEG_INF = -1e30
N_EXPERTS = 64
TOP_K = 8
N_GROUPS = 8
TOPK_GROUPS = 4
D_EXPERT = 256
D_SHARED = 256
ROUTED_SCALE = 2.5
ALPHA = (2 * DEPTH) ** 0.25
LN_EPS = 1e-5
RMS_EPS = 1e-6

ROW_BLOCK = 256
MOE_ROWS = 512
COND_ROWS = 8


def _cparams(*sem):
    return pltpu.CompilerParams(dimension_semantics=sem, vmem_limit_bytes=VMEM_LIMIT_BYTES)


def _bdot(a, b):
    return jnp.dot(a.astype(BF16), b.astype(BF16), preferred_element_type=F32)


def _bdot_nt(a, b):
    return lax.dot_general(a.astype(BF16), b.astype(BF16), (((1,), (1,)), ((), ())), preferred_element_type=F32)


def _bdot_tn(a, b):
    return lax.dot_general(a.astype(BF16), b.astype(BF16), (((0,), (0,)), ((), ())), preferred_element_type=F32)


def _silu(x):
    return x * jax.nn.sigmoid(x)


def _layer_norm_rows(x, g, b):
    mu = jnp.mean(x, axis=-1, keepdims=True)
    xc = x - mu
    var = jnp.mean(xc * xc, axis=-1, keepdims=True)
    return xc * lax.rsqrt(var + LN_EPS) * g + b


MOD_COLS = 512


def _mod_kernel(c_ref, w_ref, b_ref, o_ref):
    s = _silu(c_ref[...])
    o_ref[0] = jnp.dot(s, w_ref[0], precision=HIGHEST, preferred_element_type=F32) + b_ref[0]


def _modulation(cond, w_mod, b_mod):
    depth, d, n6 = w_mod.shape
    out = pl.pallas_call(
        _mod_kernel,
        grid=(depth, n6 // MOD_COLS),
        in_specs=[
            pl.BlockSpec((COND_ROWS, d), lambda l, j: (0, 0)),
            pl.BlockSpec((1, d, MOD_COLS), lambda l, j: (l, 0, j)),
            pl.BlockSpec((1, 1, MOD_COLS), lambda l, j: (l, 0, j)),
        ],
        out_specs=pl.BlockSpec((1, COND_ROWS, MOD_COLS), lambda l, j: (l, 0, j)),
        out_shape=jax.ShapeDtypeStruct((depth, COND_ROWS, n6), F32),
        compiler_params=_cparams("parallel", "parallel"),
        name="adaln_modulation",
    )(cond, w_mod, b_mod.reshape(depth, 1, n6))
    return out.reshape(depth * COND_ROWS * 6, 1, d)


def _mod_spec(layer, chunk, n_prompt, dec_seq, rows, blk0=0):
    def index(i, *_):
        tok = (blk0 + i) * rows
        row = jnp.where(tok < n_prompt, 0, 1 + (tok - n_prompt) // dec_seq)
        return ((layer * COND_ROWS + row) * 6 + chunk, 0, 0)

    return pl.BlockSpec((1, 1, D_MODEL), index)


def _la_in_kernel(xp_ref, xs_ref, sc_ref, sh_ref, w_ref, wab_ref, wabt_ref, pc_ref, pr_ref, o_ref, gbc_ref, gbr_ref,
                  *, npb):
    x = jnp.where(pl.program_id(0) < npb, xp_ref[...], xs_ref[...])
    xm = x * (1.0 + sc_ref[0]) + sh_ref[0]
    o_ref[...] = _bdot(xm, w_ref[...])
    ab = jnp.dot(xm, wab_ref[...], precision=HIGHEST, preferred_element_type=F32)
    lane = lax.broadcasted_iota(jnp.int32, ab.shape, 1)
    neg_a = pc_ref[0:1, :]
    dt_b = pc_ref[1:2, :]
    z = ab + dt_b
    softplus = jnp.maximum(z, 0.0) + jnp.log1p(jnp.exp(-jnp.abs(z)))
    gbc_ref[...] = jnp.where(lane < 2 * LA_HEADS, neg_a * softplus, jax.nn.sigmoid(ab))
    rows = xm.shape[0]
    for c in range(rows // CHUNK):
        xc = xm[c * CHUNK:(c + 1) * CHUNK, :]
        abt = lax.dot_general(wabt_ref[...], xc, (((1,), (1,)), ((), ())), precision=HIGHEST,
                              preferred_element_type=F32)
        zt = abt + pr_ref[:, 1:2]
        spt = jnp.maximum(zt, 0.0) + jnp.log1p(jnp.exp(-jnp.abs(zt)))
        gbr_ref[c] = pr_ref[:, 0:1] * spt


def _la_in_proj(x_p, x_s, mod, layer, dec_seq, w_in, a_log, dt_bias):
    n_prompt, d = x_p.shape
    n = n_prompt + x_s.shape[0]
    wide = LA_QKV + LA_HEADS * LA_DV
    w_main = w_in[:, :wide].astype(BF16)
    w_ab = w_in[:, wide:]
    n_ab = w_ab.shape[1]
    w_ab_pad = jnp.zeros((d, LANES), F32).at[:, :n_ab].set(w_ab)
    neg_a = -jnp.exp(a_log.astype(F32)).reshape(-1)
    dtb = dt_bias.astype(F32).reshape(-1)
    pc = jnp.zeros((SUBLANES, LANES), F32).at[0, :2 * LA_HEADS].set(neg_a).at[1, :2 * LA_HEADS].set(dtb)
    pr = jnp.zeros((n_ab, LANES), F32).at[:2 * LA_HEADS, 0].set(neg_a).at[:2 * LA_HEADS, 1].set(dtb)
    r = ROW_BLOCK
    return pl.pallas_call(
        functools.partial(_la_in_kernel, npb=n_prompt // r),
        grid=(n // r,),
        in_specs=_stream_specs(n_prompt, r, d) + [
            _mod_spec(layer, 1, n_prompt, dec_seq, r),
            _mod_spec(layer, 0, n_prompt, dec_seq, r),
            pl.BlockSpec((d, wide), lambda i: (0, 0)),
            pl.BlockSpec((d, LANES), lambda i: (0, 0)),
            pl.BlockSpec((n_ab, d), lambda i: (0, 0)),
            pl.BlockSpec((SUBLANES, LANES), lambda i: (0, 0)),
            pl.BlockSpec((n_ab, LANES), lambda i: (0, 0)),
        ],
        out_specs=[
            pl.BlockSpec((r, wide), lambda i: (i, 0)),
            pl.BlockSpec((r, LANES), lambda i: (i, 0)),
            pl.BlockSpec((r // CHUNK, n_ab, CHUNK), lambda i: (i, 0, 0)),
        ],
        out_shape=[
            jax.ShapeDtypeStruct((n, wide), F32),
            jax.ShapeDtypeStruct((n, LANES), F32),
            jax.ShapeDtypeStruct((n // CHUNK, n_ab, CHUNK), F32),
        ],
        compiler_params=_cparams("parallel"),
        name="deltanet_in_proj",
    )(x_p, x_s, mod, mod, w_main, w_ab_pad, w_ab.T, pc, pr)


def _proj_kernel(x_ref, sc_ref, sh_ref, w_ref, o_ref):
    xm = x_ref[...] * (1.0 + sc_ref[0]) + sh_ref[0]
    o_ref[...] = _bdot(xm, w_ref[...])


def _mod_proj(x, mod, layer, n_prompt, dec_seq, w, name):
    n, d = x.shape
    cols = w.shape[1]
    r = ROW_BLOCK
    return pl.pallas_call(
        _proj_kernel,
        grid=(n // r,),
        in_specs=[
            pl.BlockSpec((r, d), lambda i: (i, 0)),
            _mod_spec(layer, 1, n_prompt, dec_seq, r),
            _mod_spec(layer, 0, n_prompt, dec_seq, r),
            pl.BlockSpec((d, cols), lambda i: (0, 0)),
        ],
        out_specs=pl.BlockSpec((r, cols), lambda i: (i, 0)),
        out_shape=jax.ShapeDtypeStruct((n, cols), F32),
        compiler_params=_cparams("parallel"),
        name=name,
    )(x, mod, mod, w.astype(BF16))


CONV_PAD = SUBLANES
CONV_ROWS = 256


CONV_BLOCK_ELEMS = 4096 * LANES


def _conv_kernel(x_ref, w_ref, o_ref, pad_ref, *, seq, heads):
    zeros = jnp.zeros((CONV_PAD, heads * LANES), F32)
    pad_ref[0:CONV_PAD, :] = zeros
    pad_ref[CONV_PAD + seq:, :] = zeros
    pad_ref[CONV_PAD:CONV_PAD + seq, :] = x_ref[...]
    half = CONV_K // 2
    rows = min(CONV_ROWS, seq)
    for j in range(heads):
        c = pl.program_id(1) * heads + j
        lanes = slice(j * LANES, (j + 1) * LANES)
        is_qk = c < 2 * LA_HEADS
        scale = jnp.where(c < LA_HEADS, LA_DK ** -0.5, 1.0).astype(F32)
        for r0 in range(0, seq, rows):
            acc = jnp.zeros((rows, LANES), F32)
            for tap in range(CONV_K):
                start = r0 + CONV_PAD + tap - half
                acc = acc + w_ref[tap:tap + 1, lanes] * pad_ref[start:start + rows, lanes]
            y = _silu(acc)
            nrm = lax.rsqrt(jnp.sum(y * y, axis=-1, keepdims=True) + 1e-6) * scale
            o_ref[r0:r0 + rows, lanes] = y * jnp.where(is_qk, nrm, 1.0)


def _conv_qkv(proj, conv_w, row0, batch, seq):
    w = jnp.zeros((SUBLANES, LA_QKV), F32).at[:CONV_K].set(conv_w.astype(F32))
    blk0 = row0 // seq
    heads = max(1, min(LA_HEADS, CONV_BLOCK_ELEMS // (seq * LANES)))
    cw = heads * LANES
    return pl.pallas_call(
        functools.partial(_conv_kernel, seq=seq, heads=heads),
        grid=(batch, LA_QKV // cw),
        in_specs=[
            pl.BlockSpec((seq, cw), lambda b, c: (blk0 + b, c)),
            pl.BlockSpec((SUBLANES, cw), lambda b, c: (0, c)),
        ],
        out_specs=pl.BlockSpec((seq, cw), lambda b, c: (b, c)),
        out_shape=jax.ShapeDtypeStruct((batch * seq, LA_QKV), F32),
        scratch_shapes=[pltpu.VMEM((seq + 2 * CONV_PAD, cw), F32)],
        compiler_params=_cparams("parallel", "parallel"),
        name="deltanet_conv",
    )(proj, w)


DELTA_ROWS = 256
PAIR = 2 * CHUNK
N_DIRS = 2


PREP_HEADS = 2


def _pair_masks(rev):
    ii = lax.broadcasted_iota(jnp.int32, (PAIR, PAIR), 0)
    jj = lax.broadcasted_iota(jnp.int32, (PAIR, PAIR), 1)
    same_chunk = (ii // CHUNK) == (jj // CHUNK)
    same_sub = (ii // SUB) == (jj // SUB)
    if rev:
        return same_chunk, same_sub, same_chunk & (jj >= ii), same_chunk & (jj > ii), same_chunk & (ii >= jj)
    return same_chunk, same_sub, same_chunk & (jj <= ii), same_chunk & (jj < ii), same_chunk & (ii <= jj)


def _delta_prep_kernel(q_ref, k_ref, v_ref, gbc_ref, gbr_ref, u_ref, wq_ref, kd_ref, qk_ref, gl_ref):
    hp = pl.program_id(1)
    lane = lax.broadcasted_iota(jnp.int32, (PAIR, LANES), 1)
    cpp = PAIR // CHUNK
    masks = [_pair_masks(rev) for rev in (False, True)]
    probs = []
    for hh in range(PREP_HEADS):
        cols = slice(hh * LA_DK, (hh + 1) * LA_DK)
        for pr in range(DELTA_ROWS // PAIR):
            rows = slice(pr * PAIR, (pr + 1) * PAIR)
            q = q_ref[rows, cols]
            k = k_ref[rows, cols]
            v = v_ref[rows, cols]
            gbc = gbc_ref[rows, :]
            gram = _bdot_nt(k, k)
            qk_raw = _bdot_nt(q, k)
            for d in range(N_DIRS):
                same_chunk, same_sub, before, strict, before_t = masks[d]
                h = hp * PREP_HEADS + hh
                g_lane = d * LA_HEADS + h
                b_lane = (N_DIRS + d) * LA_HEADS + h
                g_col = jnp.sum(jnp.where(lane == g_lane, gbc, 0.0), axis=1, keepdims=True)
                b_col = jnp.sum(jnp.where(lane == b_lane, gbc, 0.0), axis=1, keepdims=True)
                g_row = jnp.concatenate([gbr_ref[pr * cpp + c, pl.ds(g_lane, 1), :] for c in range(cpp)], axis=1)
                gam_col = jnp.sum(jnp.where(before, g_row, 0.0), axis=1, keepdims=True)
                gam_row = jnp.sum(jnp.where(before_t, g_col, 0.0), axis=0, keepdims=True)
                tot_col = jnp.sum(jnp.where(same_chunk, g_row, 0.0), axis=1, keepdims=True)
                tot_row = jnp.sum(jnp.where(same_chunk, g_col, 0.0), axis=0, keepdims=True)
                dec = jnp.where(before, jnp.exp(jnp.where(before, gam_col - gam_row, 0.0)), 0.0)
                e_col = jnp.exp(gam_col)
                nm = jnp.where(strict, -(gram * b_col * dec), 0.0)
                nd = jnp.where(same_sub, nm, 0.0)
                probs.append(dict(
                    hh=hh, pr=pr, d=d, rows=rows, nd=nd, ne=nm - nd,
                    rhs=jnp.concatenate([v * b_col, k * (b_col * e_col)], axis=1),
                    qd=q * e_col, kd=k * jnp.exp(tot_col - gam_col), qkm=qk_raw * dec, gl=jnp.exp(tot_row)))
    for p in probs:
        p["yd"] = p["nd"]
        p["p"] = p["nd"]
    for _ in range(3):
        for p in probs:
            p["p"] = _bdot(p["p"], p["p"])
        for p in probs:
            p["yd"] = p["yd"] + p["p"] + _bdot(p["yd"], p["p"])
    for p in probs:
        p["f"] = p["ne"] + _bdot(p["yd"], p["ne"])
    for p in probs:
        p["f2"] = _bdot(p["f"], p["f"])
    for p in probs:
        p["gm"] = p["f"] + p["f2"] + _bdot(p["f"], p["f2"])
    for p in probs:
        p["yt"] = p["gm"] + p["yd"] + _bdot(p["gm"], p["yd"])
    for p in probs:
        p["sol"] = p["rhs"] + _bdot(p["yt"], p["rhs"])
    for p in probs:
        d, hh, pr, rows = p["d"], p["hh"], p["pr"], p["rows"]
        u = p["sol"][:, :LA_DV]
        w = p["sol"][:, LA_DV:]
        u_ref[d, hh, rows, :] = u
        kd_ref[d, hh, rows, :] = p["kd"].T.astype(BF16)
        qk_ref[d, hh, rows, :] = p["qkm"].astype(BF16)
        for c in range(cpp):
            cg = pr * cpp + c
            cr = slice(c * CHUNK, (c + 1) * CHUNK)
            wq_ref[d, hh, cg * PAIR:cg * PAIR + CHUNK, :] = w[cr, :].astype(BF16)
            wq_ref[d, hh, cg * PAIR + CHUNK:(cg + 1) * PAIR, :] = p["qd"][cr, :].astype(BF16)
            gl_ref[d, hh, cg] = p["gl"][:, c * CHUNK:c * CHUNK + 1] + jnp.zeros((1, LANES), F32)


def _delta_prep(qkv, gbc, gbr, row0):
    n = qkv.shape[0]
    r = DELTA_ROWS
    g0 = row0 // r
    n_ab = gbr.shape[1]
    cpb = r // CHUNK
    hg = LA_HEADS // PREP_HEADS
    hw = PREP_HEADS * LA_DK
    hm = lambda rows, dt: jax.ShapeDtypeStruct((N_DIRS, LA_HEADS, rows, LANES), dt)
    hspec = lambda rows: pl.BlockSpec((N_DIRS, PREP_HEADS, rows, LANES), lambda i, h: (0, h, i, 0))
    return pl.pallas_call(
        _delta_prep_kernel,
        grid=(n // r, hg),
        in_specs=[
            pl.BlockSpec((r, hw), lambda i, h: (i, h)),
            pl.BlockSpec((r, hw), lambda i, h: (i, hg + h)),
            pl.BlockSpec((r, hw), lambda i, h: (i, 2 * hg + h)),
            pl.BlockSpec((r, LANES), lambda i, h: (g0 + i, 0)),
            pl.BlockSpec((cpb, n_ab, CHUNK), lambda i, h: (g0 + i, 0, 0)),
        ],
        out_specs=[
            hspec(r), hspec(2 * r), hspec(r), hspec(r),
            pl.BlockSpec((N_DIRS, PREP_HEADS, cpb, 1, LANES), lambda i, h: (0, h, i, 0, 0)),
        ],
        out_shape=[
            hm(n, F32), hm(2 * n, BF16), hm(n, BF16), hm(n, BF16),
            jax.ShapeDtypeStruct((N_DIRS, LA_HEADS, n // CHUNK, 1, LANES), F32),
        ],
        compiler_params=_cparams("parallel", "parallel"),
        name="deltanet_prep",
    )(qkv, qkv, qkv, gbc, gbr)


def _delta_scan_kernel(*refs, has_s0):
    u_refs, wq_refs, kd_refs, qk_refs, gl_refs = (refs[0:2], refs[2:4], refs[4:6], refs[6:8], refs[8:10])
    if has_s0:
        s0_ref, of_ref, ob_ref, sfin_ref, s_ref = refs[10:]
    else:
        of_ref, ob_ref, sfin_ref, s_ref = refs[10:]
    o_refs = (of_ref, ob_ref)
    t = pl.program_id(1)

    @pl.when(t == 0)
    def _():
        if has_s0:
            s_ref[...] = s0_ref[0]
        else:
            s_ref[...] = jnp.zeros_like(s_ref)

    n_pairs = DELTA_ROWS // PAIR
    cpp = PAIR // CHUNK
    chains = [(d, h) for d in range(N_DIRS) for h in range(LA_HEADS)]
    zeros = jnp.zeros((CHUNK, LA_DV), F32)
    for step in range(n_pairs):
        pair_of = [n_pairs - 1 - step if d == 1 else step for d in range(N_DIRS)]
        s = [s_ref[d * LA_HEADS + h] for d, h in chains]
        v_new = [[None] * cpp for _ in chains]
        o_inter = [[None] * cpp for _ in chains]
        for j in range(cpp):
            chunk_of = [cpp - 1 - j if d == 1 else j for d in range(N_DIRS)]
            r = []
            for i, (d, h) in enumerate(chains):
                cg = pair_of[d] * cpp + chunk_of[d]
                r.append(jnp.dot(wq_refs[d][0, h, cg * PAIR:(cg + 1) * PAIR, :], s[i].astype(BF16),
                                 preferred_element_type=F32))
            for i, (d, h) in enumerate(chains):
                c = chunk_of[d]
                cg = pair_of[d] * cpp + c
                v_new[i][c] = u_refs[d][0, h, cg * CHUNK:(cg + 1) * CHUNK, :] - r[i][:CHUNK]
                o_inter[i][c] = r[i][CHUNK:]
                vz = jnp.concatenate([v_new[i][c] if m == c else zeros for m in range(cpp)], axis=0)
                kd_t = kd_refs[d][0, h, pair_of[d] * PAIR:(pair_of[d] + 1) * PAIR, :]
                s[i] = s[i] * gl_refs[d][0, h, cg] + jnp.dot(kd_t, vz.astype(BF16), preferred_element_type=F32)
        for i, (d, h) in enumerate(chains):
            s_ref[d * LA_HEADS + h] = s[i]
            rows = slice(pair_of[d] * PAIR, (pair_of[d] + 1) * PAIR)
            o_intra = jnp.dot(qk_refs[d][0, h, rows, :], jnp.concatenate(v_new[i], axis=0).astype(BF16),
                              preferred_element_type=F32)
            o_refs[d][h, rows, :] = jnp.concatenate(o_inter[i], axis=0) + o_intra

    @pl.when(t == pl.num_programs(1) - 1)
    def _():
        sfin_ref[0] = s_ref[...]


def _delta_scan(prep, s0, batch, seq):
    u, wq, kd, qk, gl = prep
    nt = seq // DELTA_ROWS
    r = DELTA_ROWS
    cpb = r // CHUNK
    n_chain = N_DIRS * LA_HEADS

    def tt(d, t):
        return nt - 1 - t if d == 1 else t

    def dspecs(rows):
        return [pl.BlockSpec((1, LA_HEADS, rows, LANES), functools.partial(
            lambda b, t, d: (d, 0, b * nt + tt(d, t), 0), d=d)) for d in range(N_DIRS)]

    gl_specs = [pl.BlockSpec((1, LA_HEADS, cpb, 1, LANES), functools.partial(
        lambda b, t, d: (d, 0, b * nt + tt(d, t), 0, 0), d=d)) for d in range(N_DIRS)]
    in_specs = dspecs(r) + dspecs(2 * r) + dspecs(r) + dspecs(r) + gl_specs
    args = [u, u, wq, wq, kd, kd, qk, qk, gl, gl]
    has_s0 = s0 is not None
    if has_s0:
        in_specs.append(pl.BlockSpec((1, n_chain, LA_DK, LA_DV), lambda b, t: (b, 0, 0, 0)))
        args.append(s0)
    o_shape = jax.ShapeDtypeStruct((LA_HEADS, batch * seq, LA_DV), F32)
    o_specs = [pl.BlockSpec((LA_HEADS, r, LANES), functools.partial(
        lambda b, t, d: (0, b * nt + tt(d, t), 0), d=d)) for d in range(N_DIRS)]
    return pl.pallas_call(
        functools.partial(_delta_scan_kernel, has_s0=has_s0),
        grid=(batch, nt),
        in_specs=in_specs,
        out_specs=o_specs + [pl.BlockSpec((1, n_chain, LA_DK, LA_DV), lambda b, t: (b, 0, 0, 0))],
        out_shape=[o_shape, o_shape, jax.ShapeDtypeStruct((batch, n_chain, LA_DK, LA_DV), F32)],
        scratch_shapes=[pltpu.VMEM((n_chain, LA_DK, LA_DV), F32)],
        compiler_params=_cparams("parallel", "arbitrary"),
        name="deltanet_scan",
    )(*args)


def _stream_specs(n_prompt, rows, cols):
    npb = n_prompt // rows
    return [
        pl.BlockSpec((rows, cols), lambda i: (jnp.minimum(i, npb - 1), 0)),
        pl.BlockSpec((rows, cols), lambda i: (jnp.maximum(i - npb, 0), 0)),
    ]


def _la_out_kernel(ofp_ref, obp_ref, ofs_ref, obs_ref, z_ref, xp_ref, xs_ref, g1_ref, ng_ref, w_ref, lng_ref,
                   lnb_ref, y_ref, *, npb):
    is_prompt = pl.program_id(0) < npb
    x = jnp.where(is_prompt, xp_ref[...], xs_ref[...])
    ng = ng_ref[0:1, :]
    parts = []
    for h in range(LA_HEADS):
        oh = jnp.where(is_prompt, ofp_ref[h] + obp_ref[h], ofs_ref[h] + obs_ref[h])
        r = lax.rsqrt(jnp.mean(oh * oh, axis=-1, keepdims=True) + RMS_EPS)
        parts.append(oh * r * ng)
    on = jnp.concatenate(parts, axis=1) * _silu(z_ref[...])
    out = _bdot(on, w_ref[...])
    xr = ALPHA * x + g1_ref[0] * out
    y_ref[...] = _layer_norm_rows(xr, lng_ref[0:1, :], lnb_ref[0:1, :])


def _pad_rows(v):
    return jnp.zeros((SUBLANES, v.shape[0]), F32).at[0].set(v.astype(F32))


def _la_out(o_p, o_s, proj, x_p, x_s, mod, layer, dec_seq, norm_g, w_out, ln_g, ln_b):
    n_prompt, d = x_p.shape
    n = n_prompt + x_s.shape[0]
    r = ROW_BLOCK
    zblk = LA_QKV // d
    npb = n_prompt // r
    sp = pl.BlockSpec((LA_HEADS, r, LA_DV), lambda i: (0, jnp.minimum(i, npb - 1), 0))
    ss = pl.BlockSpec((LA_HEADS, r, LA_DV), lambda i: (0, jnp.maximum(i - npb, 0), 0))
    return pl.pallas_call(
        functools.partial(_la_out_kernel, npb=n_prompt // r),
        grid=(n // r,),
        in_specs=[
            sp, sp, ss, ss,
            pl.BlockSpec((r, d), lambda i: (i, zblk)),
            *_stream_specs(n_prompt, r, d),
            _mod_spec(layer, 2, n_prompt, dec_seq, r),
            pl.BlockSpec((SUBLANES, LA_DV), lambda i: (0, 0)),
            pl.BlockSpec((d, d), lambda i: (0, 0)),
            pl.BlockSpec((SUBLANES, d), lambda i: (0, 0)),
            pl.BlockSpec((SUBLANES, d), lambda i: (0, 0)),
        ],
        out_specs=pl.BlockSpec((r, d), lambda i: (i, 0)),
        out_shape=jax.ShapeDtypeStruct((n, d), F32),
        compiler_params=_cparams("parallel"),
        name="deltanet_out",
    )(o_p[0], o_p[1], o_s[0], o_s[1], proj, x_p, x_s, mod, _pad_rows(norm_g), w_out.astype(BF16), _pad_rows(ln_g),
      _pad_rows(ln_b))


def _out_kernel(op_ref, os_ref, x_ref, g1_ref, w_ref, lng_ref, lnb_ref, y_ref, *, npb):
    is_prompt = pl.program_id(0) < npb
    o = jnp.where(is_prompt, op_ref[...], os_ref[...])
    out = _bdot(o, w_ref[...])
    xr = ALPHA * x_ref[...] + g1_ref[0] * out
    y_ref[...] = _layer_norm_rows(xr, lng_ref[0:1, :], lnb_ref[0:1, :])


def _swa_out(o_p, o_s, x, mod, layer, n_prompt, dec_seq, w_out, ln_g, ln_b):
    n, d = x.shape
    r = ROW_BLOCK
    sp, ss = _stream_specs(n_prompt, r, d)
    return pl.pallas_call(
        functools.partial(_out_kernel, npb=n_prompt // r),
        grid=(n // r,),
        in_specs=[
            sp, ss,
            pl.BlockSpec((r, d), lambda i: (i, 0)),
            _mod_spec(layer, 2, n_prompt, dec_seq, r),
            pl.BlockSpec((d, d), lambda i: (0, 0)),
            pl.BlockSpec((SUBLANES, d), lambda i: (0, 0)),
            pl.BlockSpec((SUBLANES, d), lambda i: (0, 0)),
        ],
        out_specs=pl.BlockSpec((r, d), lambda i: (i, 0)),
        out_shape=jax.ShapeDtypeStruct((n, d), F32),
        compiler_params=_cparams("parallel"),
        name="swa_out",
    )(o_p, o_s, x, mod, w_out.astype(BF16), _pad_rows(ln_g), _pad_rows(ln_b))


SWA_SCALE = SWA_DH ** -0.5


def _stack_group(q, kv):
    base = kv * SWA_GROUP * SWA_DH
    return jnp.concatenate([q[:, base + g * SWA_DH: base + (g + 1) * SWA_DH] for g in range(SWA_GROUP)], axis=0)


def _sink_col(sink_ref, kv, rows):
    return jnp.concatenate(
        [jnp.full((rows, 1), sink_ref[kv * SWA_GROUP + g], F32) for g in range(SWA_GROUP)], axis=0)


def _ctx_attn_kernel(sink_ref, q_ref, k_ref, v_ref, o_ref):
    q = q_ref[...]
    k = k_ref[...]
    v = v_ref[...]
    t = q.shape[0]
    outs = []
    for kv in range(SWA_KV_HEADS):
        kh = k[:, kv * SWA_DH:(kv + 1) * SWA_DH]
        vh = v[:, kv * SWA_DH:(kv + 1) * SWA_DH]
        s = _bdot_nt(_stack_group(q, kv), kh) * SWA_SCALE
        sink = _sink_col(sink_ref, kv, t)
        m = jnp.maximum(jnp.max(s, axis=1, keepdims=True), sink)
        p = jnp.exp(s - m)
        den = jnp.sum(p, axis=1, keepdims=True) + jnp.exp(sink - m)
        o = _bdot(p, vh) / den
        outs.extend(o[g * t:(g + 1) * t, :] for g in range(SWA_GROUP))
    o_ref[...] = jnp.concatenate(outs, axis=1)


def _ctx_attention(qkv, sink, batch, seq):
    kcol = SWA_HEADS * SWA_DH // SWA_KVW
    return pl.pallas_call(
        _ctx_attn_kernel,
        grid=(batch,),
        in_specs=[
            pl.BlockSpec(memory_space=pltpu.SMEM),
            pl.BlockSpec((seq, SWA_HEADS * SWA_DH), lambda b: (b, 0)),
            pl.BlockSpec((seq, SWA_KVW), lambda b: (b, kcol)),
            pl.BlockSpec((seq, SWA_KVW), lambda b: (b, kcol + 1)),
        ],
        out_specs=pl.BlockSpec((seq, SWA_HEADS * SWA_DH), lambda b: (b, 0)),
        out_shape=jax.ShapeDtypeStruct((batch * seq, SWA_HEADS * SWA_DH), F32),
        compiler_params=_cparams("parallel"),
        name="swa_context_attention",
    )(sink.astype(F32), qkv, qkv, qkv)


ROPE_COLS = SWA_HEADS * SWA_DH + SWA_KVW
ROPE_SHIFT = SWA_DH // 4


def _rope_tables(t_len):
    t = jnp.arange(t_len)
    r = (t // GRID_W).astype(F32)
    col = (t % GRID_W).astype(F32)
    inv = ROPE_BASE ** (-jnp.arange(ROT_FREQS, dtype=F32) / ROT_FREQS)
    ang_r = r[:, None] * inv
    ang_c = col[:, None] * inv
    ang = jnp.concatenate([ang_r, ang_r, ang_c, ang_c], -1)
    cos = jnp.cos(ang)
    sin = jnp.sin(ang)
    first = (jnp.arange(SWA_DH) % (2 * ROPE_SHIFT)) < ROPE_SHIFT
    sin_up = jnp.where(first, -sin, 0.0)
    sin_dn = jnp.where(first, 0.0, sin)
    reps = LANES // SWA_DH
    return tuple(jnp.tile(a, (1, reps)) for a in (cos, sin_up, sin_dn))


def _rope_kernel(x_ref, cos_ref, su_ref, sd_ref, o_ref):
    cos = cos_ref[...]
    su = su_ref[...]
    sd = sd_ref[...]
    for j in range(ROPE_COLS // LANES):
        x = x_ref[:, j * LANES:(j + 1) * LANES]
        up = pltpu.roll(x, LANES - ROPE_SHIFT, 1)
        dn = pltpu.roll(x, ROPE_SHIFT, 1)
        o_ref[:, j * LANES:(j + 1) * LANES] = x * cos + up * su + dn * sd


def _rope_qk(qkv, row0, batch, seq):
    r = ROW_BLOCK
    nb = seq // r
    blk0 = row0 // r
    tables = _rope_tables(seq)
    tspec = pl.BlockSpec((r, LANES), lambda b, i: (i, 0))
    return pl.pallas_call(
        _rope_kernel,
        grid=(batch, nb),
        in_specs=[pl.BlockSpec((r, ROPE_COLS), lambda b, i: (blk0 + b * nb + i, 0)), tspec, tspec, tspec],
        out_specs=pl.BlockSpec((r, ROPE_COLS), lambda b, i: (b * nb + i, 0)),
        out_shape=jax.ShapeDtypeStruct((batch * seq, ROPE_COLS), F32),
        compiler_params=_cparams("parallel", "parallel"),
        name="swa_rope",
    )(qkv, *tables)


def _band_bias():
    r = jnp.arange(SWA_GROUP * Q_BLOCK)[:, None] % Q_BLOCK
    c = jnp.arange(3 * Q_BLOCK)[None, :] - Q_BLOCK
    return jnp.where(jnp.abs(r - c) <= WINDOW, 0.0, NEG_INF).astype(F32)


def _lat_attn_kernel(sink_ref, band_ref, q_ref, kp_ref, kc_ref, kn_ref, vp_ref, vc_ref, vn_ref, ck_ref, cv_ref,
                     o_ref):
    i = pl.program_id(1)
    q = q_ref[...] * SWA_SCALE
    kw = jnp.concatenate([kp_ref[...], kc_ref[...], kn_ref[...]], axis=0)
    vw = jnp.concatenate([vp_ref[...], vc_ref[...], vn_ref[...]], axis=0)
    ck = ck_ref[...]
    cv = cv_ref[...]
    row = lax.broadcasted_iota(jnp.int32, (3 * Q_BLOCK, 1), 0)
    prev_bias = jnp.where(i > 0, 0.0, NEG_INF).astype(F32)
    next_bias = jnp.where(i < pl.num_programs(1) - 1, 0.0, NEG_INF).astype(F32)
    bias = band_ref[...] + jnp.where(row < Q_BLOCK, prev_bias, jnp.where(row >= 2 * Q_BLOCK, next_bias, 0.0))
    heads = range(SWA_KV_HEADS)
    cols = [slice(kv * SWA_DH, (kv + 1) * SWA_DH) for kv in heads]
    qg = [_stack_group(q, kv) for kv in heads]
    s_win = [_bdot_nt(kw[:, cols[kv]], qg[kv]) for kv in heads]
    s_ctx = [_bdot_nt(ck[:, cols[kv]], qg[kv]) for kv in heads]
    p_win, p_ctx, den = [], [], []
    for kv in heads:
        sw = s_win[kv] + bias
        sc = s_ctx[kv]
        sink = jnp.concatenate(
            [jnp.full((1, Q_BLOCK), sink_ref[kv * SWA_GROUP + g], F32) for g in range(SWA_GROUP)], axis=1)
        m = jnp.maximum(jnp.maximum(jnp.max(sw, axis=0, keepdims=True), jnp.max(sc, axis=0, keepdims=True)), sink)
        p_win.append(jnp.exp(sw - m))
        p_ctx.append(jnp.exp(sc - m))
        den.append(jnp.sum(p_win[kv], axis=0, keepdims=True) + jnp.sum(p_ctx[kv], axis=0, keepdims=True)
                   + jnp.exp(sink - m))
    o_win = [_bdot_tn(vw[:, cols[kv]], p_win[kv]) for kv in heads]
    o_ctx = [_bdot_tn(cv[:, cols[kv]], p_ctx[kv]) for kv in heads]
    o_t = jnp.concatenate([(o_win[kv] + o_ctx[kv]) / den[kv] for kv in heads], axis=0)
    o = o_t.T
    outs = [o[g * Q_BLOCK:(g + 1) * Q_BLOCK, cols[kv]] for kv in heads for g in range(SWA_GROUP)]
    o_ref[...] = jnp.concatenate(outs, axis=1)


def _lat_attention(qk_rope, qkv, cache_k, cache_v, sink, row0, batch, seq):
    assert WINDOW == Q_BLOCK and seq % Q_BLOCK == 0
    nq = seq // Q_BLOCK
    blk0 = row0 // Q_BLOCK
    qw = SWA_HEADS * SWA_DH
    kcol = qw // SWA_KVW
    past = cache_k.shape[0] // batch

    def kspec(off):
        return pl.BlockSpec((Q_BLOCK, SWA_KVW), lambda b, i: (b * nq + jnp.clip(i + off, 0, nq - 1), kcol))

    def vspec(off):
        return pl.BlockSpec((Q_BLOCK, SWA_KVW),
                            lambda b, i: (blk0 + b * nq + jnp.clip(i + off, 0, nq - 1), kcol + 1))

    cspec = pl.BlockSpec((past, SWA_KVW), lambda b, i: (b, 0))
    return pl.pallas_call(
        _lat_attn_kernel,
        grid=(batch, nq),
        in_specs=[
            pl.BlockSpec(memory_space=pltpu.SMEM),
            pl.BlockSpec((3 * Q_BLOCK, SWA_GROUP * Q_BLOCK), lambda b, i: (0, 0)),
            pl.BlockSpec((Q_BLOCK, qw), lambda b, i: (b * nq + i, 0)),
            kspec(-1), kspec(0), kspec(1),
            vspec(-1), vspec(0), vspec(1),
            cspec, cspec,
        ],
        out_specs=pl.BlockSpec((Q_BLOCK, qw), lambda b, i: (b * nq + i, 0)),
        out_shape=jax.ShapeDtypeStruct((batch * seq, qw), F32),
        compiler_params=_cparams("parallel", "parallel"),
        name="swa_latent_attention",
    )(sink.astype(F32), _band_bias().T, qk_rope, qk_rope, qk_rope, qk_rope, qkv, qkv, qkv, cache_k, cache_v)


GROUP_SIZE = N_EXPERTS // N_GROUPS
U32 = jnp.uint32
HALF_D = D_MODEL // 2
TILE_ROWS = HALF_D // LANES


def _bf16_bits_high(x):
    return pltpu.bitcast(x.astype(BF16).astype(F32), U32)


def _to_tiles(ref, x):
    rows = x.shape[0]
    for s in range(TILE_ROWS):
        lo = _bf16_bits_high(x[:, s * LANES:(s + 1) * LANES]) >> 16
        hi = _bf16_bits_high(x[:, HALF_D + s * LANES:HALF_D + (s + 1) * LANES])
        ref[pl.ds(s, rows, stride=TILE_ROWS), :] = lo | hi


def _from_tiles(ref, rows):
    words = [ref[pl.ds(s, rows, stride=TILE_ROWS), :] for s in range(TILE_ROWS)]
    lo = [pltpu.bitcast(w << 16, F32) for w in words]
    hi = [pltpu.bitcast(w & jnp.uint32(0xFFFF0000), F32) for w in words]
    return jnp.concatenate(lo + hi, axis=1)


def _first_argmax(v, iota, axis, size):
    m = jnp.max(v, axis=axis, keepdims=True)
    idx = jnp.min(jnp.where(v == m, iota, size), axis=axis, keepdims=True)
    return m, idx


def _router_kernel(x_ref, sc_ref, sh_ref, rt_ref, bias_ref, hs_ref, idx_ref, w_ref, pos_ref, wtm_ref, cnt_ref):
    i = pl.program_id(0)
    rows = x_ref.shape[0]

    @pl.when(i == 0)
    def _():
        cnt_ref[...] = jnp.zeros_like(cnt_ref)

    hs = x_ref[...] * (1.0 + sc_ref[0]) + sh_ref[0]
    _to_tiles(hs_ref, hs)
    logits = lax.dot_general(rt_ref[...], hs, (((1,), (1,)), ((), ())), precision=HIGHEST,
                             preferred_element_type=F32)
    scores = jax.nn.sigmoid(logits)
    biased = scores + bias_ref[:, 0:1]
    b3 = biased.reshape(N_GROUPS, GROUP_SIZE, rows)
    mem = lax.broadcasted_iota(jnp.int32, b3.shape, 1)
    m1, i1 = _first_argmax(b3, mem, 1, GROUP_SIZE)
    m2 = jnp.max(jnp.where(mem == i1, -jnp.inf, b3), axis=1, keepdims=True)
    grp = (m1 + m2).reshape(N_GROUPS, rows)
    giota = lax.broadcasted_iota(jnp.int32, grp.shape, 0)
    gsel = jnp.zeros(grp.shape, jnp.bool_)
    for _ in range(TOPK_GROUPS):
        _, gi = _first_argmax(grp, giota, 0, N_GROUPS)
        hit = giota == gi
        gsel = gsel | hit
        grp = jnp.where(hit, -jnp.inf, grp)
    gmask = jnp.broadcast_to(gsel.reshape(N_GROUPS, 1, rows), b3.shape).reshape(N_EXPERTS, rows)
    sel = jnp.where(gmask, biased, -jnp.inf)
    eiota = lax.broadcasted_iota(jnp.int32, sel.shape, 0)
    chosen_any = jnp.zeros(sel.shape, jnp.bool_)
    idx_rows, w_rows, hits = [], [], []
    for _ in range(TOP_K):
        _, ei = _first_argmax(sel, eiota, 0, N_EXPERTS)
        hit = eiota == ei
        chosen_any = chosen_any | hit
        idx_rows.append(ei)
        w_rows.append(jnp.sum(jnp.where(hit, scores, 0.0), axis=0, keepdims=True))
        hits.append(hit)
        sel = jnp.where(hit, -jnp.inf, sel)
    wsum = w_rows[0]
    for w in w_rows[1:]:
        wsum = wsum + w
    w_rows = [w / wsum * ROUTED_SCALE for w in w_rows]
    onehot = chosen_any.astype(F32)
    s_i = lax.broadcasted_iota(jnp.int32, (rows, rows), 0)
    t_i = lax.broadcasted_iota(jnp.int32, (rows, rows), 1)
    earlier = (s_i < t_i).astype(F32)
    rank = cnt_ref[:, 0:1] + _bdot(onehot, earlier)
    cnt_ref[...] = cnt_ref[...] + jnp.sum(onehot, axis=1, keepdims=True)
    pos_rows = [jnp.sum(jnp.where(hit, rank, 0.0), axis=0, keepdims=True) for hit in hits]
    idx_ref[...] = jnp.concatenate(idx_rows, axis=0)
    w8 = jnp.concatenate(w_rows, axis=0)
    w_ref[...] = w8
    pos_ref[...] = jnp.concatenate(pos_rows, axis=0).astype(jnp.int32)
    wtm_ref[...] = jnp.concatenate([w8, jnp.zeros((LANES - TOP_K, rows), F32)], axis=0).T


def _router(x, mod, layer, n_prompt, dec_seq, router, bias):
    n, d = x.shape
    r = ROW_BLOCK
    bias_pad = jnp.zeros((N_EXPERTS, LANES), F32).at[:, 0].set(bias.astype(F32))
    kspec = pl.BlockSpec((TOP_K, r), lambda i: (0, i))
    return pl.pallas_call(
        _router_kernel,
        grid=(n // r,),
        in_specs=[
            pl.BlockSpec((r, d), lambda i: (i, 0)),
            _mod_spec(layer, 4, n_prompt, dec_seq, r),
            _mod_spec(layer, 3, n_prompt, dec_seq, r),
            pl.BlockSpec((N_EXPERTS, d), lambda i: (0, 0)),
            pl.BlockSpec((N_EXPERTS, LANES), lambda i: (0, 0)),
        ],
        out_specs=[
            pl.BlockSpec((r * TILE_ROWS, LANES), lambda i: (i, 0)),
            kspec, kspec, kspec,
            pl.BlockSpec((r, LANES), lambda i: (i, 0)),
            pl.BlockSpec((N_EXPERTS, LANES), lambda i: (0, 0)),
        ],
        out_shape=[
            jax.ShapeDtypeStruct((n * TILE_ROWS, LANES), U32),
            jax.ShapeDtypeStruct((TOP_K, n), jnp.int32),
            jax.ShapeDtypeStruct((TOP_K, n), F32),
            jax.ShapeDtypeStruct((TOP_K, n), jnp.int32),
            jax.ShapeDtypeStruct((n, LANES), F32),
            jax.ShapeDtypeStruct((N_EXPERTS, LANES), F32),
        ],
        compiler_params=_cparams("arbitrary"),
        name="moe_router",
    )(x, mod, mod, router.T.astype(F32), bias_pad)


DISPATCH_TOKENS = 512


def _dest_kernel(start_ref, idx_ref, pos_ref, o_ref):
    idx = idx_ref[...]
    start = jnp.zeros(idx.shape, jnp.int32)
    for e in range(N_EXPERTS):
        start = jnp.where(idx == e, start_ref[e], start)
    o_ref[...] = (start + pos_ref[...]) * TILE_ROWS


def _dest_rows(pad_start, idx, pos):
    k, n = idx.shape
    t = min(2048, n)
    spec = pl.BlockSpec((k, t), lambda i: (0, i))
    return pl.pallas_call(
        _dest_kernel,
        grid=(n // t,),
        in_specs=[pl.BlockSpec(memory_space=pltpu.SMEM), spec, spec],
        out_specs=spec,
        out_shape=jax.ShapeDtypeStruct((k, n), jnp.int32),
        compiler_params=_cparams("parallel"),
        name="moe_dest_rows",
    )(pad_start, idx, pos)


def _tile_copy(src_ref, src_row, dst_ref, dst_row, sem):
    return pltpu.make_async_copy(src_ref.at[pl.ds(pl.multiple_of(src_row, TILE_ROWS), TILE_ROWS), :],
                                 dst_ref.at[pl.ds(pl.multiple_of(dst_row, TILE_ROWS), TILE_ROWS), :], sem)


def _dispatch_kernel(dest_ref, hs_ref, xs_ref, sem):
    tokens = hs_ref.shape[0] // TILE_ROWS

    def copy(k, t):
        return _tile_copy(hs_ref, t * TILE_ROWS, xs_ref, dest_ref[k, t], sem)

    def start(t, carry):
        for k in range(TOP_K):
            copy(k, t).start(priority=k % 2)
        return carry

    lax.fori_loop(0, tokens, start, 0)

    def wait(t, carry):
        for k in range(TOP_K):
            copy(k, t).wait()
        return carry

    lax.fori_loop(0, tokens, wait, 0)


def _dispatch(hs, dest, n_rows):
    t = DISPATCH_TOKENS
    n = hs.shape[0] // TILE_ROWS
    return pl.pallas_call(
        _dispatch_kernel,
        grid=(n // t,),
        in_specs=[
            pl.BlockSpec((TOP_K, t), lambda i: (0, i), memory_space=pltpu.SMEM),
            pl.BlockSpec((t * TILE_ROWS, LANES), lambda i: (i, 0)),
        ],
        out_specs=pl.BlockSpec(memory_space=pl.ANY),
        out_shape=jax.ShapeDtypeStruct((n_rows * TILE_ROWS, LANES), U32),
        scratch_shapes=[pltpu.SemaphoreType.DMA],
        compiler_params=_cparams("arbitrary"),
        name="moe_dispatch",
    )(dest, hs)


def _expert_kernel(be_ref, bv_ref, xs_ref, wg_ref, wu_ref, wd_ref, y_ref):
    del be_ref
    valid = bv_ref[pl.program_id(0)]

    @pl.when(valid > 0)
    def _():
        x = _from_tiles(xs_ref, MOE_ROWS)
        row = lax.broadcasted_iota(jnp.int32, x.shape, 0)
        x = jnp.where(row < valid, x, 0.0)
        h = _silu(_bdot(x, wg_ref[0])) * _bdot(x, wu_ref[0])
        _to_tiles(y_ref, _bdot(h, wd_ref[0]))

    @pl.when(valid <= 0)
    def _():
        y_ref[...] = jnp.zeros_like(y_ref)


def _experts(xs, blk_expert, blk_valid, w_gate, w_up, w_down, layer):
    nb = xs.shape[0] // (MOE_ROWS * TILE_ROWS)
    _, _, d, de = w_gate.shape
    blk = pl.BlockSpec((MOE_ROWS * TILE_ROWS, LANES), lambda i, be, bv: (i, 0))
    grid_spec = pltpu.PrefetchScalarGridSpec(
        num_scalar_prefetch=2,
        grid=(nb,),
        in_specs=[
            blk,
            pl.BlockSpec((None, 1, d, de), lambda i, be, bv: (layer, be[i], 0, 0)),
            pl.BlockSpec((None, 1, d, de), lambda i, be, bv: (layer, be[i], 0, 0)),
            pl.BlockSpec((None, 1, de, d), lambda i, be, bv: (layer, be[i], 0, 0)),
        ],
        out_specs=blk,
    )
    return pl.pallas_call(
        _expert_kernel,
        grid_spec=grid_spec,
        out_shape=jax.ShapeDtypeStruct(xs.shape, U32),
        compiler_params=_cparams("arbitrary"),
        name="moe_experts",
    )(blk_expert, blk_valid, xs, w_gate, w_up, w_down)


COMBINE_TOKENS = 256


def _combine_kernel(dest_ref, y_ref, wtm_ref, hs_ref, x_ref, g2_ref, sg_ref, su_ref, sd_ref,
                    lng_ref, lnb_ref, o_ref, ybuf, sem):
    tokens = x_ref.shape[0]

    def copy(k, t):
        return _tile_copy(y_ref, dest_ref[k, t], ybuf.at[k], t * TILE_ROWS, sem)

    def start(t, carry):
        for k in range(TOP_K):
            copy(k, t).start(priority=k % 2)
        return carry

    lax.fori_loop(0, tokens, start, 0)
    hs = _from_tiles(hs_ref, tokens)
    shared = _bdot(_silu(_bdot(hs, sg_ref[...])) * _bdot(hs, su_ref[...]), sd_ref[...])

    def wait(t, carry):
        for k in range(TOP_K):
            copy(k, t).wait()
        return carry

    lax.fori_loop(0, tokens, wait, 0)
    wtm = wtm_ref[...]
    routed = _from_tiles(ybuf.at[0], tokens) * wtm[:, 0:1]
    for k in range(1, TOP_K):
        routed = routed + _from_tiles(ybuf.at[k], tokens) * wtm[:, k:k + 1]
    xr = ALPHA * x_ref[...] + g2_ref[0] * (routed + shared)
    o_ref[...] = _layer_norm_rows(xr, lng_ref[0:1, :], lnb_ref[0:1, :])


def _combine(y, dest, wtm, hs, x, mod, layer, n_prompt, dec_seq, s_gate, s_up, s_down, ln_g, ln_b, row0, n_rows):
    d = x.shape[1]
    t = COMBINE_TOKENS
    ds = s_gate.shape[1]
    b0 = row0 // t
    row = pl.BlockSpec((t, d), lambda i: (i, 0))
    return pl.pallas_call(
        _combine_kernel,
        grid=(n_rows // t,),
        in_specs=[
            pl.BlockSpec((TOP_K, t), lambda i: (0, b0 + i), memory_space=pltpu.SMEM),
            pl.BlockSpec(memory_space=pl.ANY),
            pl.BlockSpec((t, LANES), lambda i: (b0 + i, 0)),
            pl.BlockSpec((t * TILE_ROWS, LANES), lambda i: (b0 + i, 0)),
            pl.BlockSpec((t, d), lambda i: (b0 + i, 0)),
            _mod_spec(layer, 5, n_prompt, dec_seq, t, b0),
            pl.BlockSpec((d, ds), lambda i: (0, 0)),
            pl.BlockSpec((d, ds), lambda i: (0, 0)),
            pl.BlockSpec((ds, d), lambda i: (0, 0)),
            pl.BlockSpec((SUBLANES, d), lambda i: (0, 0)),
            pl.BlockSpec((SUBLANES, d), lambda i: (0, 0)),
        ],
        out_specs=row,
        out_shape=jax.ShapeDtypeStruct((n_rows, d), F32),
        scratch_shapes=[pltpu.VMEM((TOP_K, t * TILE_ROWS, LANES), U32), pltpu.SemaphoreType.DMA],
        compiler_params=_cparams("arbitrary"),
        name="moe_combine",
    )(dest, y, wtm, hs, x, mod, s_gate.astype(BF16), s_up.astype(BF16), s_down.astype(BF16),
      _pad_rows(ln_g), _pad_rows(ln_b))


def _moe(x, mod, layer, n_prompt, dec_seq, router, bias, w_gate, w_up, w_down, s_gate, s_up, s_down, ln_g, ln_b,
         split_streams=False):
    n, d = x.shape
    hs, idx, _, pos, wtm, cnt = _router(x, mod, layer, n_prompt, dec_seq, router, bias)
    counts = cnt[:, 0].astype(jnp.int32)
    padded = (counts + MOE_ROWS - 1) // MOE_ROWS * MOE_ROWS
    pad_end = jnp.cumsum(padded)
    pad_start = (pad_end - padded).astype(jnp.int32)
    nb = n * TOP_K // MOE_ROWS + N_EXPERTS
    blk_row = jnp.arange(nb, dtype=jnp.int32) * MOE_ROWS
    blk_expert = jnp.minimum(jnp.sum(pad_end[None, :] <= blk_row[:, None], axis=1), N_EXPERTS - 1).astype(jnp.int32)
    blk_valid = jnp.clip(pad_start[blk_expert] + counts[blk_expert] - blk_row, 0, MOE_ROWS).astype(jnp.int32)
    dest = _dest_rows(pad_start, idx, pos)
    xs = _dispatch(hs, dest, nb * MOE_ROWS)
    y = _experts(xs, blk_expert, blk_valid, w_gate, w_up, w_down, layer)
    parts = ((0, n_prompt), (n_prompt, n - n_prompt)) if split_streams else ((0, n),)
    outs = [_combine(y, dest, wtm, hs, x, mod, layer, n_prompt, dec_seq, s_gate, s_up, s_down, ln_g, ln_b, r0, nr)
            for r0, nr in parts]
    return tuple(outs) if split_streams else outs[0]


def kernel(x_prompt, x_sample, state_la, cache_k, cache_v, c, c_ctx, w_mod, b_mod, ln_g, ln_b,
           la_w_in, la_conv, la_a_log, la_dt_bias, la_norm_g, la_w_out,
           swa_w_qkv, swa_sink, swa_w_out,
           moe_router, moe_bias, moe_w_gate, moe_w_up, moe_w_down, sh_w_gate, sh_w_up, sh_w_down):
    bp, tp, d = x_prompt.shape
    bs, ts, _ = x_sample.shape
    n_prompt = bp * tp
    assert bs + 1 <= COND_ROWS and d == D_MODEL
    assert n_prompt % ROW_BLOCK == 0 and ts % ROW_BLOCK == 0 and tp % DELTA_ROWS == 0

    cond = jnp.zeros((COND_ROWS, d), F32).at[0].set(c_ctx).at[1:1 + bs].set(c)
    mod = _modulation(cond, w_mod, b_mod)
    x_p = x_prompt.reshape(n_prompt, d)
    x_s = x_sample.reshape(bs * ts, d)

    def moe(x, i, split_streams=False):
        return _moe(x, mod, i, n_prompt, ts, moe_router[i], moe_bias[i], moe_w_gate, moe_w_up, moe_w_down,
                    sh_w_gate[i], sh_w_up[i], sh_w_down[i], ln_g[i, 1], ln_b[i, 1], split_streams)

    proj, gbc, gbr = _la_in_proj(x_p, x_s, mod, 0, ts, la_w_in[0], la_a_log[0], la_dt_bias[0])
    qkv_p = _conv_qkv(proj, la_conv[0], 0, bp, tp)
    qkv_s = _conv_qkv(proj, la_conv[0], n_prompt, bs, ts)
    opf, opb, s_fin = _delta_scan(_delta_prep(qkv_p, gbc, gbr, 0), None, bp, tp)
    s0 = state_la[:, 0].reshape(bs, N_DIRS * LA_HEADS, LA_DK, LA_DV)
    osf, osb, _ = _delta_scan(_delta_prep(qkv_s, gbc, gbr, n_prompt), s0, bs, ts)
    x = _la_out((opf, opb), (osf, osb), proj, x_p, x_s, mod, 0, ts, la_norm_g[0], la_w_out[0], ln_g[0, 0],
                ln_b[0, 0])
    new_la = s_fin.reshape(bp, 1, N_DIRS, LA_HEADS, LA_DK, LA_DV)
    x = moe(x, 0)

    qkv = _mod_proj(x, mod, 1, n_prompt, ts, swa_w_qkv[0], "swa_qkv_proj")
    qw = SWA_HEADS * SWA_DH
    new_k = qkv[:n_prompt, qw:qw + SWA_KVW].reshape(bp, 1, tp, SWA_KV_HEADS, SWA_DH)
    new_v = qkv[:n_prompt, qw + SWA_KVW:].reshape(bp, 1, tp, SWA_KV_HEADS, SWA_DH)
    a_p = _ctx_attention(qkv, swa_sink[0], bp, tp)
    qk_rope = _rope_qk(qkv, n_prompt, bs, ts)
    past = cache_k.shape[2]
    a_s = _lat_attention(qk_rope, qkv, cache_k[:, 0].reshape(bs * past, SWA_KVW),
                         cache_v[:, 0].reshape(bs * past, SWA_KVW), swa_sink[0], n_prompt, bs, ts)
    x = _swa_out(a_p, a_s, x, mod, 1, n_prompt, ts, swa_w_out[0], ln_g[1, 0], ln_b[1, 0])
    y_p, y_s = moe(x, 1, split_streams=True)
    return y_p.reshape(bp, tp, d), y_s.reshape(bs, ts, d), new_la, new_k, new_v
```
